```python
import math
import jax, jax.numpy as jnp
from jax import lax
import numpy as np

D_MODEL = 1024
BATCH = 8
SEQ = 8192
DEPTH = 4

N_HEADS = 16
N_KV_HEADS = 4
HEAD_DIM = 64
GROUP = N_HEADS // N_KV_HEADS
ATTN_WIDTH = N_HEADS * HEAD_DIM
KV_WIDTH = N_KV_HEADS * HEAD_DIM
WINDOW = 128
BLOCK = 128
ATTN_IN = 2 * ATTN_WIDTH + 2 * KV_WIDTH
POOL_WIDTH = D_MODEL
POOL_WINDOWS = (2, 4, 8, 16)
N_POOL_GROUPS = len(POOL_WINDOWS)
POOL_GC = POOL_WIDTH // N_POOL_GROUPS
POOL_IN = 2 * POOL_WIDTH
N_MIXERS = 2
N_A = (DEPTH + 1) // 2
N_B = DEPTH // 2
EPS = 1e-6

kernel_name = "hybrid_swa_sink_multiscale_pool"


def rmsnorm(x, g):
    xf = x.astype(jnp.float32)
    y = xf * lax.rsqrt(jnp.mean(xf * xf, axis=-1, keepdims=True) + EPS)
    return (y * g.astype(jnp.float32)).astype(x.dtype)


def swa_with_sinks(q, k, v, sinks):
    B, S = q.shape[0], q.shape[1]
    nb = S // BLOCK
    qb = q.reshape(B, nb, BLOCK, N_KV_HEADS, GROUP, HEAD_DIM)
    def band(t):
        tb = t.reshape(B, nb, BLOCK, N_KV_HEADS, HEAD_DIM)
        prev = jnp.pad(tb, ((0, 0), (1, 0), (0, 0), (0, 0), (0, 0)))[:, :-1]
        return jnp.concatenate([prev, tb], axis=2)
    kc, vc = band(k), band(v)
    s = jnp.einsum('bnqhgd,bnkhd->bnhgqk', qb, kc).astype(jnp.float32)
    s = s * (1.0 / math.sqrt(HEAD_DIM))
    qi = jnp.arange(BLOCK)[:, None]
    kj = jnp.arange(2 * BLOCK)[None, :]
    diff = qi + BLOCK - kj
    in_band = (diff >= 0) & (diff < WINDOW)
    not_pad = (jnp.arange(nb)[:, None, None] > 0) | (kj[None] >= BLOCK)
    valid = in_band[None] & not_pad
    s = jnp.where(valid[None, :, None, None], s, -jnp.inf)
    sink = sinks.astype(jnp.float32).reshape(N_KV_HEADS, GROUP)[:, :, None, None]
    m = jnp.maximum(jnp.max(s, axis=-1, keepdims=True), sink)
    p = jnp.exp(s - m)
    denom = jnp.sum(p, axis=-1, keepdims=True) + jnp.exp(sink - m)
    p = (p / denom).astype(v.dtype)
    o = jnp.einsum('bnhgqk,bnkhd->bnqhgd', p, vc)
    return o.reshape(B, S, ATTN_WIDTH)


def attn_layer(h, w_in, sinks, w_out):
    B, S, _ = h.shape
    proj = h @ w_in
    q, k, v, z = jnp.split(proj, [ATTN_WIDTH, ATTN_WIDTH + KV_WIDTH, ATTN_WIDTH + 2 * KV_WIDTH], axis=-1)
    q = q.reshape(B, S, N_KV_HEADS, GROUP, HEAD_DIM)
    k = k.reshape(B, S, N_KV_HEADS, HEAD_DIM)
    v = v.reshape(B, S, N_KV_HEADS, HEAD_DIM)
    o = swa_with_sinks(q, k, v, sinks)
    return (o * jax.nn.silu(z)) @ w_out


def multiscale_pool(u):
    B, S, _ = u.shape
    uf = u.astype(jnp.float32).reshape(B, S, N_POOL_GROUPS, POOL_GC)
    cs = jnp.pad(jnp.cumsum(uf, axis=1), ((0, 0), (1, 0), (0, 0), (0, 0)))
    outs = []
    for gi, w in enumerate(POOL_WINDOWS):
        c = cs[:, :, gi]
        lower = jnp.pad(c[:, :S + 1 - w], ((0, 0), (w - 1, 0), (0, 0)))
        count = jnp.minimum(jnp.arange(1, S + 1), w).astype(jnp.float32)[None, :, None]
        outs.append((c[:, 1:] - lower) / count - uf[:, :, gi])
    return jnp.stack(outs, axis=2).astype(u.dtype)


def pool_layer(h, w_in, w_mix, scale, w_out):
    B, S, _ = h.shape
    u, z = jnp.split(h @ w_in, [POOL_WIDTH], axis=-1)
    p = multiscale_pool(u)
    m = jnp.einsum('bsgc,gcd->bsgd', p, w_mix).reshape(B, S, POOL_WIDTH) * scale
    return (m * jax.nn.silu(z)) @ w_out


def _fwd_setup_inputs(seed: int = 0) -> dict:
    key = jax.random.key(seed)
    ks = jax.random.split(key, 12)
    out_scale = 1.0 / math.sqrt(2.0 * DEPTH)
    x = jax.random.normal(ks[0], (BATCH, SEQ, D_MODEL), jnp.float32)
    norm_g = 1.0 + 0.05 * jax.random.normal(ks[1], (DEPTH, D_MODEL), jnp.float32)
    attn_w_in = jax.random.normal(ks[2], (N_A, D_MODEL, ATTN_IN), jnp.float32) * D_MODEL ** -0.5
    attn_sinks = 0.5 * jax.random.normal(ks[3], (N_A, N_HEADS), jnp.float32)
    attn_w_out = jax.random.normal(ks[4], (N_A, ATTN_WIDTH, D_MODEL), jnp.float32) * ATTN_WIDTH ** -0.5 * out_scale
    pool_w_in = jax.random.normal(ks[5], (N_B, D_MODEL, POOL_IN), jnp.float32) * D_MODEL ** -0.5
    pool_w_mix = jax.random.normal(ks[6], (N_B, N_POOL_GROUPS, POOL_GC, POOL_GC), jnp.float32) * POOL_GC ** -0.5
    pool_scale = 1.0 + 0.1 * jax.random.normal(ks[7], (N_B, POOL_WIDTH), jnp.float32)
    pool_w_out = jax.random.normal(ks[8], (N_B, POOL_WIDTH, D_MODEL), jnp.float32) * POOL_WIDTH ** -0.5 * out_scale
    final_g = 1.0 + 0.05 * jax.random.normal(ks[9], (D_MODEL,), jnp.float32)
    return {"x": x, "norm_g": norm_g, "attn_w_in": attn_w_in, "attn_sinks": attn_sinks,
            "attn_w_out": attn_w_out, "pool_w_in": pool_w_in, "pool_w_mix": pool_w_mix,
            "pool_scale": pool_scale, "pool_w_out": pool_w_out, "final_g": final_g}


def _fwd_reference(x, norm_g, attn_w_in, attn_sinks, attn_w_out, pool_w_in, pool_w_mix,
              pool_scale, pool_w_out, final_g):
    for i in range(DEPTH):
        h = rmsnorm(x, norm_g[i])
        j = i // N_MIXERS
        if i % N_MIXERS == 0:
            y = attn_layer(h, attn_w_in[j], attn_sinks[j], attn_w_out[j])
        else:
            y = pool_layer(h, pool_w_in[j], pool_w_mix[j], pool_scale[j], pool_w_out[j])
        x = x + y.astype(x.dtype)
    return rmsnorm(x, final_g)


import jax as _jax
import jax.numpy as _jnp

TWIN_FORMAT = 'train_step'
FWD_PARAMS = ['x', 'norm_g', 'attn_w_in', 'attn_sinks', 'attn_w_out', 'pool_w_in', 'pool_w_mix', 'pool_scale', 'pool_w_out', 'final_g']
TWIN_WEIGHTS = ['norm_g', 'attn_w_in', 'attn_sinks', 'attn_w_out', 'pool_w_in', 'pool_w_mix', 'pool_scale', 'pool_w_out', 'final_g']
TWIN_DIFF_INPUT = 'x'
TWIN_INPUTS = ['x', 'norm_g', 'attn_w_in', 'attn_sinks', 'attn_w_out', 'pool_w_in', 'pool_w_mix', 'pool_scale', 'pool_w_out', 'final_g', 'loss_target', 'm_norm_g', 'm_attn_w_in', 'm_attn_sinks', 'm_attn_w_out', 'm_pool_w_in', 'm_pool_w_mix', 'm_pool_scale', 'm_pool_w_out', 'm_final_g', 'v_norm_g', 'v_attn_w_in', 'v_attn_sinks', 'v_attn_w_out', 'v_pool_w_in', 'v_pool_w_mix', 'v_pool_scale', 'v_pool_w_out', 'v_final_g']
TWIN_OUTPUTS = ['loss', 'grad_x', 'grad_norm_g', 'grad_attn_w_in', 'grad_attn_sinks', 'grad_attn_w_out', 'grad_pool_w_in', 'grad_pool_w_mix', 'grad_pool_scale', 'grad_pool_w_out', 'grad_final_g', 'delta_norm_g', 'delta_attn_w_in', 'delta_attn_sinks', 'delta_attn_w_out', 'delta_pool_w_in', 'delta_pool_w_mix', 'delta_pool_scale', 'delta_pool_w_out', 'delta_final_g', 'new_m_norm_g', 'new_m_attn_w_in', 'new_m_attn_sinks', 'new_m_attn_w_out', 'new_m_pool_w_in', 'new_m_pool_w_mix', 'new_m_pool_scale', 'new_m_pool_w_out', 'new_m_final_g', 'new_v_norm_g', 'new_v_attn_w_in', 'new_v_attn_sinks', 'new_v_attn_w_out', 'new_v_pool_w_in', 'new_v_pool_w_mix', 'new_v_pool_scale', 'new_v_pool_w_out', 'new_v_final_g']
TWIN_LEAF_KINDS = {'loss': 'loss', 'grad_x': 'grad_x', 'grad_norm_g': 'grad_w', 'grad_attn_w_in': 'grad_w', 'grad_attn_sinks': 'grad_w', 'grad_attn_w_out': 'grad_w', 'grad_pool_w_in': 'grad_w', 'grad_pool_w_mix': 'grad_w', 'grad_pool_scale': 'grad_w', 'grad_pool_w_out': 'grad_w', 'grad_final_g': 'grad_w', 'delta_norm_g': 'delta_w', 'delta_attn_w_in': 'delta_w', 'delta_attn_sinks': 'delta_w', 'delta_attn_w_out': 'delta_w', 'delta_pool_w_in': 'delta_w', 'delta_pool_w_mix': 'delta_w', 'delta_pool_scale': 'delta_w', 'delta_pool_w_out': 'delta_w', 'delta_final_g': 'delta_w', 'new_m_norm_g': 'new_m', 'new_m_attn_w_in': 'new_m', 'new_m_attn_sinks': 'new_m', 'new_m_attn_w_out': 'new_m', 'new_m_pool_w_in': 'new_m', 'new_m_pool_w_mix': 'new_m', 'new_m_pool_scale': 'new_m', 'new_m_pool_w_out': 'new_m', 'new_m_final_g': 'new_m', 'new_v_norm_g': 'new_v', 'new_v_attn_w_in': 'new_v', 'new_v_attn_sinks': 'new_v', 'new_v_attn_w_out': 'new_v', 'new_v_pool_w_in': 'new_v', 'new_v_pool_w_mix': 'new_v', 'new_v_pool_scale': 'new_v', 'new_v_pool_w_out': 'new_v', 'new_v_final_g': 'new_v'}


def _forward(args):
    return _fwd_reference(*[args[k] for k in FWD_PARAMS])


def _output_shape():
    out = _jax.eval_shape(lambda: _forward(_fwd_setup_inputs(0)))
    return out.shape, out.dtype

N_MICROBATCH = 1
ADAM_LR = 0.001
ADAM_B1 = 0.9
ADAM_B2 = 0.999
ADAM_EPS = 1e-08
ADAM_WD = 0.01
ADAM_STEP = 10
PER_EXAMPLE_BATCH_AXIS = {'x': 0, 'loss_target': 0}
SHARED_INPUTS = []
_WEIGHT_DTYPES = {'norm_g': _jnp.float32, 'attn_w_in': _jnp.float32, 'attn_sinks': _jnp.float32, 'attn_w_out': _jnp.float32, 'pool_w_in': _jnp.float32, 'pool_w_mix': _jnp.float32, 'pool_scale': _jnp.float32, 'pool_w_out': _jnp.float32, 'final_g': _jnp.float32}
MOMENT_SCALE = {'norm_g': 4.961022e-02, 'attn_w_in': 1.341917e-02, 'attn_sinks': 8.880045e-03, 'attn_w_out': 3.190815e-02, 'pool_w_in': 4.692617e-02, 'pool_w_mix': 4.646496e-02, 'pool_scale': 4.918587e-02, 'pool_w_out': 1.311546e-01, 'final_g': 6.396460e+01}


def _to_microbatches(a, axis):
    t = _jnp.moveaxis(a, axis, 0)
    t = t.reshape((N_MICROBATCH, t.shape[0] // N_MICROBATCH) + t.shape[1:])
    return _jnp.moveaxis(t, 1, axis + 1)


def setup_inputs(seed: int = 0) -> dict:
    inp = _fwd_setup_inputs(seed)
    key = _jax.random.fold_in(_jax.random.key(seed), 7919)
    shape, _ = _output_shape()
    out = dict(inp)
    out["loss_target"] = _jax.random.normal(_jax.random.fold_in(key, 0), shape, _jnp.float32)
    for i, name in enumerate(TWIN_WEIGHTS):
        w = inp[name].astype(_jnp.float32)
        if MOMENT_SCALE is None:
            s = _jnp.sqrt(_jnp.mean(_jnp.square(w)) + 1e-30)
        else:
            s = MOMENT_SCALE[name]
        km, kv = _jax.random.split(_jax.random.fold_in(key, i + 1))
        out[name] = w
        out["m_" + name] = s * _jax.random.normal(km, w.shape, _jnp.float32)
        out["v_" + name] = (s * s) * _jax.random.uniform(kv, w.shape, _jnp.float32, 0.5, 1.5)
    if N_MICROBATCH > 1:
        for name, axis in PER_EXAMPLE_BATCH_AXIS.items():
            out[name] = _to_microbatches(out[name], axis)
    return {'x': out['x'], 'norm_g': out['norm_g'], 'attn_w_in': out['attn_w_in'], 'attn_sinks': out['attn_sinks'], 'attn_w_out': out['attn_w_out'], 'pool_w_in': out['pool_w_in'], 'pool_w_mix': out['pool_w_mix'], 'pool_scale': out['pool_scale'], 'pool_w_out': out['pool_w_out'], 'final_g': out['final_g'], 'loss_target': out['loss_target'], 'm_norm_g': out['m_norm_g'], 'm_attn_w_in': out['m_attn_w_in'], 'm_attn_sinks': out['m_attn_sinks'], 'm_attn_w_out': out['m_attn_w_out'], 'm_pool_w_in': out['m_pool_w_in'], 'm_pool_w_mix': out['m_pool_w_mix'], 'm_pool_scale': out['m_pool_scale'], 'm_pool_w_out': out['m_pool_w_out'], 'm_final_g': out['m_final_g'], 'v_norm_g': out['v_norm_g'], 'v_attn_w_in': out['v_attn_w_in'], 'v_attn_sinks': out['v_attn_sinks'], 'v_attn_w_out': out['v_attn_w_out'], 'v_pool_w_in': out['v_pool_w_in'], 'v_pool_w_mix': out['v_pool_w_mix'], 'v_pool_scale': out['v_pool_scale'], 'v_pool_w_out': out['v_pool_w_out'], 'v_final_g': out['v_final_g']}


def _loss(weights, diff, rest, loss_target):
    with _jax.named_scope("forward"):
        args = {**rest, TWIN_DIFF_INPUT: diff, **{k: w.astype(_WEIGHT_DTYPES[k]) for k, w in weights.items()}}
        y = _forward(args)
    with _jax.named_scope("loss_head"):
        err = _jnp.square(y.astype(_jnp.float32) - loss_target)
        return 0.5 * _jnp.sum(_jnp.mean(err, axis=-1)) if err.ndim else 0.5 * err


def _adamw(w, g, m, v):
    m = ADAM_B1 * m + (1.0 - ADAM_B1) * g
    v = ADAM_B2 * v + (1.0 - ADAM_B2) * _jnp.square(g)
    m_hat = m / (1.0 - ADAM_B1 ** ADAM_STEP)
    v_hat = v / (1.0 - ADAM_B2 ** ADAM_STEP)
    delta = -ADAM_LR * (m_hat / (_jnp.sqrt(v_hat) + ADAM_EPS) + ADAM_WD * w)
    return delta, m, v


def reference(x, norm_g, attn_w_in, attn_sinks, attn_w_out, pool_w_in, pool_w_mix, pool_scale, pool_w_out, final_g, loss_target, m_norm_g, m_attn_w_in, m_attn_sinks, m_attn_w_out, m_pool_w_in, m_pool_w_mix, m_pool_scale, m_pool_w_out, m_final_g, v_norm_g, v_attn_w_in, v_attn_sinks, v_attn_w_out, v_pool_w_in, v_pool_w_mix, v_pool_scale, v_pool_w_out, v_final_g):
    given = dict(x=x, norm_g=norm_g, attn_w_in=attn_w_in, attn_sinks=attn_sinks, attn_w_out=attn_w_out, pool_w_in=pool_w_in, pool_w_mix=pool_w_mix, pool_scale=pool_scale, pool_w_out=pool_w_out, final_g=final_g, loss_target=loss_target, m_norm_g=m_norm_g, m_attn_w_in=m_attn_w_in, m_attn_sinks=m_attn_sinks, m_attn_w_out=m_attn_w_out, m_pool_w_in=m_pool_w_in, m_pool_w_mix=m_pool_w_mix, m_pool_scale=m_pool_scale, m_pool_w_out=m_pool_w_out, m_final_g=m_final_g, v_norm_g=v_norm_g, v_attn_w_in=v_attn_w_in, v_attn_sinks=v_attn_sinks, v_attn_w_out=v_attn_w_out, v_pool_w_in=v_pool_w_in, v_pool_w_mix=v_pool_w_mix, v_pool_scale=v_pool_scale, v_pool_w_out=v_pool_w_out, v_final_g=v_final_g)
    weights = {n: given[n] for n in TWIN_WEIGHTS}
    shared = {n: given[n] for n in SHARED_INPUTS}
    per_example = {n: given[n] for n in ['x']}
    grad_fn = _jax.value_and_grad(_loss, argnums=(0, 1))

    def one_microbatch(ex, loss_target):
        ex = dict(ex)
        diff = ex.pop(TWIN_DIFF_INPUT)
        return grad_fn(weights, diff, {**shared, **ex}, loss_target)

    if N_MICROBATCH == 1:
        loss, (grad_w, grad_x) = one_microbatch(per_example, given["loss_target"])
    else:
        def body(carry, xs):
            loss_sum, grad_sum = carry
            l_k, (gw_k, gx_k) = one_microbatch(xs[0], xs[1])
            with _jax.named_scope("update"):
                return (loss_sum + l_k, _jax.tree.map(_jnp.add, grad_sum, gw_k)), gx_k

        init = (_jnp.zeros((), _jnp.float32), _jax.tree.map(_jnp.zeros_like, weights))
        (loss, grad_w), grad_x = _jax.lax.scan(body, init, (per_example, given["loss_target"]))
    with _jax.named_scope("update"):
        delta_w, new_m, new_v = {}, {}, {}
        for n in TWIN_WEIGHTS:
            delta_w[n], new_m[n], new_v[n] = _adamw(weights[n], grad_w[n], given["m_" + n], given["v_" + n])
    return (loss, grad_x, *[grad_w[n] for n in TWIN_WEIGHTS], *[delta_w[n] for n in TWIN_WEIGHTS],
            *[new_m[n] for n in TWIN_WEIGHTS], *[new_v[n] for n in TWIN_WEIGHTS])
```

```python
import functools
import math

import jax
import jax.numpy as jnp
from jax import lax
from jax.experimental import pallas as pl
from jax.experimental.pallas import tpu as pltpu

F32 = jnp.float32
BF16 = jnp.bfloat16

D_MODEL = 1024
N_HEADS = 16
N_KV_HEADS = 4
GROUP = N_HEADS // N_KV_HEADS
HEAD_DIM = 64
KV_WIDTH = N_KV_HEADS * HEAD_DIM
ATTN_BLOCK = 128
POOL_WINDOWS = (2, 4, 8, 16)
POOL_GC = 256
POOL_HALO = 16
EPS = 1e-6
N_CHIPS = 4
ATTN_CHUNK = 640
POOL_CHUNK = 512
ROW_TILE = 512
NEG_BIG = -1e30
VMEM_LIMIT_BYTES = 60 * 1024 * 1024

ADAM_LR = 0.001
ADAM_B1 = 0.9
ADAM_B2 = 0.999
ADAM_EPS = 1e-08
ADAM_WD = 0.01
ADAM_STEP = 10

MESH_ID = pl.DeviceIdType.MESH


def _dot(a, b):
    return jnp.dot(a, b, preferred_element_type=F32)


def _dot_nt(a, b):
    return lax.dot_general(a, b, (((1,), (1,)), ((), ())), preferred_element_type=F32)


def _dot_tn(a, b):
    return lax.dot_general(a, b, (((0,), (0,)), ((), ())), preferred_element_type=F32)


def _vmem():
    return pl.BlockSpec(memory_space=pltpu.VMEM)


def _rows(width, tile=ROW_TILE):
    return pl.BlockSpec((tile, width), lambda i: (i, 0))


def _params(semantics):
    return pltpu.CompilerParams(dimension_semantics=(semantics,), vmem_limit_bytes=VMEM_LIMIT_BYTES)


def _norm(xf, g):
    rstd = lax.rsqrt(jnp.mean(xf * xf, axis=-1, keepdims=True) + EPS)
    xhat = xf * rstd
    return xhat, rstd, xhat * g


def _norm_bwd(dh, xhat, rstd, g):
    dg = jnp.sum(dh * xhat, axis=0, keepdims=True)
    dxhat = dh * g
    dx = rstd * (dxhat - xhat * jnp.mean(dxhat * xhat, axis=-1, keepdims=True))
    return dx, dg


def _silu_parts(zf):
    sig = jax.nn.sigmoid(zf)
    return zf * sig, sig * (1.0 + zf * (1.0 - sig))


def attn_in_fwd(x, g, w_in):
    T = x.shape[0]

    def body(x_ref, g_ref, w_ref, q_ref, k_ref, v_ref, z_ref):
        _, _, h = _norm(x_ref[...], g_ref[...])
        h = h.astype(BF16)
        c0 = _dot(h, w_ref[0])
        c1 = _dot(h, w_ref[1])
        c2 = _dot(h, w_ref[2])
        c3 = _dot(h, w_ref[3])
        scale = 1.0 / math.sqrt(HEAD_DIM)
        q_ref[:, :ATTN_CHUNK] = (c0 * scale).astype(BF16)
        q_ref[:, ATTN_CHUNK:] = (c1[:, :384] * scale).astype(BF16)
        k_ref[...] = c1[:, 384:].astype(BF16)
        v_ref[...] = c2[:, :KV_WIDTH].astype(BF16)
        z_ref[:, :384] = c2[:, KV_WIDTH:].astype(BF16)
        z_ref[:, 384:] = c3.astype(BF16)

    return pl.pallas_call(
        body,
        name="attn_in_fwd",
        grid=(T // ROW_TILE,),
        in_specs=[_rows(D_MODEL), _vmem(), _vmem()],
        out_specs=[_rows(D_MODEL), _rows(KV_WIDTH), _rows(KV_WIDTH), _rows(D_MODEL)],
        out_shape=[
            jax.ShapeDtypeStruct((T, D_MODEL), BF16),
            jax.ShapeDtypeStruct((T, KV_WIDTH), BF16),
            jax.ShapeDtypeStruct((T, KV_WIDTH), BF16),
            jax.ShapeDtypeStruct((T, D_MODEL), BF16),
        ],
        compiler_params=_params("parallel"),
    )(x, g, w_in)


def _band_masks(first_tile):
    shape = (GROUP * ATTN_BLOCK, 2 * ATTN_BLOCK)
    qi = lax.broadcasted_iota(jnp.int32, shape, 0) & (ATTN_BLOCK - 1)
    kj = lax.broadcasted_iota(jnp.int32, shape, 1)
    diff = qi + ATTN_BLOCK - kj
    band = (diff >= 0) & (diff < ATTN_BLOCK)
    return band & ((kj >= ATTN_BLOCK) | jnp.logical_not(first_tile)), band


def _stack_heads(ref, rows, hk):
    return jnp.concatenate(
        [ref[rows, (hk * GROUP + gi) * HEAD_DIM:(hk * GROUP + gi + 1) * HEAD_DIM] for gi in range(GROUP)], axis=0)


def _band(ref, halo_ref, b, hk):
    cols = slice(hk * HEAD_DIM, (hk + 1) * HEAD_DIM)
    prev = halo_ref[:, cols] if b == 0 else ref[(b - 1) * ATTN_BLOCK:b * ATTN_BLOCK, cols]
    return jnp.concatenate([prev, ref[b * ATTN_BLOCK:(b + 1) * ATTN_BLOCK, cols]], axis=0)


def _sink_column(sink_ref, hk):
    return jnp.concatenate(
        [jnp.full((ATTN_BLOCK, 1), sink_ref[hk * GROUP + gi], F32) for gi in range(GROUP)], axis=0)


def _halo_prev(width):
    per_tile = ROW_TILE // ATTN_BLOCK
    return pl.BlockSpec((ATTN_BLOCK, width), lambda i: (jnp.maximum(i * per_tile - 1, 0), 0))


def attn_core_fwd(q, k, v, z, x, w_out, sinks):
    T = x.shape[0]
    blocks = ROW_TILE // ATTN_BLOCK

    def body(q_ref, k_ref, kh_ref, v_ref, vh_ref, z_ref, x_ref, w_ref, sink_ref, xo_ref, o_ref, lse_ref, oacc):
        first_mask, band_mask = _band_masks(pl.program_id(0) == 0)
        for hk in range(N_KV_HEADS):
            sink = _sink_column(sink_ref, hk)
            for b in range(blocks):
                rows = slice(b * ATTN_BLOCK, (b + 1) * ATTN_BLOCK)
                qg = _stack_heads(q_ref, rows, hk)
                s = _dot_nt(qg, _band(k_ref, kh_ref, b, hk))
                s = jnp.where(first_mask if b == 0 else band_mask, s, NEG_BIG)
                m = jnp.maximum(jnp.max(s, axis=-1, keepdims=True), sink)
                p = jnp.exp(s - m)
                denom = jnp.sum(p, axis=-1, keepdims=True) + jnp.exp(sink - m)
                o = _dot(p.astype(BF16), _band(v_ref, vh_ref, b, hk)) / denom
                lse = m + jnp.log(denom)
                for gi in range(GROUP):
                    h = hk * GROUP + gi
                    part = slice(gi * ATTN_BLOCK, (gi + 1) * ATTN_BLOCK)
                    oacc[rows, h * HEAD_DIM:(h + 1) * HEAD_DIM] = o[part]
                    lse_ref[rows, h:h + 1] = lse[part]
        of = oacc[...]
        silu, _ = _silu_parts(z_ref[...].astype(F32))
        y = _dot((of * silu).astype(BF16), w_ref[...])
        xo_ref[...] = x_ref[...] + y
        o_ref[...] = of.astype(BF16)

    return pl.pallas_call(
        body,
        name="attn_core_fwd",
        grid=(T // ROW_TILE,),
        in_specs=[_rows(D_MODEL), _rows(KV_WIDTH), _halo_prev(KV_WIDTH), _rows(KV_WIDTH), _halo_prev(KV_WIDTH),
                  _rows(D_MODEL), _rows(D_MODEL), _vmem(), pl.BlockSpec(memory_space=pltpu.SMEM)],
        out_specs=[_rows(D_MODEL), _rows(D_MODEL), _rows(N_HEADS)],
        out_shape=[
            jax.ShapeDtypeStruct((T, D_MODEL), F32),
            jax.ShapeDtypeStruct((T, D_MODEL), BF16),
            jax.ShapeDtypeStruct((T, N_HEADS), F32),
        ],
        scratch_shapes=[pltpu.VMEM((ROW_TILE, D_MODEL), F32)],
        compiler_params=_params("parallel"),
    )(q, k, k, v, v, z, x, w_out, sinks)


def _inv_count(first_row, rows, window):
    t = first_row + lax.broadcasted_iota(jnp.int32, (rows, 1), 0)
    return 1.0 / jnp.minimum(t + 1, window).astype(F32)


def pool_fwd(x, g, w_in, w_mix, scale, w_out):
    T = x.shape[0]

    def body(x_ref, g_ref, win_ref, wmix_ref, scale_ref, wout_ref, xo_ref, p_ref, z_ref, carry):
        i = pl.program_id(0)

        @pl.when(i == 0)
        def _():
            carry[...] = jnp.zeros_like(carry)

        _, _, h = _norm(x_ref[...], g_ref[...])
        h = h.astype(BF16)
        u = jnp.concatenate([_dot(h, win_ref[0]), _dot(h, win_ref[1])], axis=1)
        z = jnp.concatenate([_dot(h, win_ref[2]), _dot(h, win_ref[3])], axis=1)
        ext = jnp.concatenate([carry[...], u], axis=0)
        carry[...] = u[ROW_TILE - POOL_HALO:]
        mixed = []
        for gi, window in enumerate(POOL_WINDOWS):
            cols = slice(gi * POOL_GC, (gi + 1) * POOL_GC)
            s = ext[:, cols]
            shift = 1
            while shift < window:
                s = s + pltpu.roll(s, shift, 0)
                shift *= 2
            p = s[POOL_HALO:] * _inv_count(i * ROW_TILE, ROW_TILE, window) - u[:, cols]
            p = p.astype(BF16)
            p_ref[:, cols] = p
            mixed.append(_dot(p, wmix_ref[gi]))
        m = jnp.concatenate(mixed, axis=1) * scale_ref[...]
        silu, _ = _silu_parts(z)
        y = _dot((m * silu).astype(BF16), wout_ref[...])
        xo_ref[...] = x_ref[...] + y
        z_ref[...] = z.astype(BF16)

    return pl.pallas_call(
        body,
        name="pool_fwd",
        grid=(T // ROW_TILE,),
        in_specs=[_rows(D_MODEL), _vmem(), _vmem(), _vmem(), _vmem(), _vmem()],
        out_specs=[_rows(D_MODEL), _rows(D_MODEL), _rows(D_MODEL)],
        out_shape=[
            jax.ShapeDtypeStruct((T, D_MODEL), F32),
            jax.ShapeDtypeStruct((T, D_MODEL), BF16),
            jax.ShapeDtypeStruct((T, D_MODEL), BF16),
        ],
        scratch_shapes=[pltpu.VMEM((POOL_HALO, D_MODEL), F32)],
        compiler_params=_params("arbitrary"),
    )(x, g, w_in, w_mix, scale, w_out)


def loss_head(x, g, target):
    T = x.shape[0]

    def body(x_ref, g_ref, t_ref, dx_ref, sq_ref, dg_ref):
        @pl.when(pl.program_id(0) == 0)
        def _():
            sq_ref[...] = jnp.zeros_like(sq_ref)
            dg_ref[...] = jnp.zeros_like(dg_ref)

        gv = g_ref[...]
        xhat, rstd, out = _norm(x_ref[...], gv)
        err = out - t_ref[...]
        sq_ref[...] += jnp.sum(err * err, axis=0, keepdims=True)
        dx, dg = _norm_bwd(err * (1.0 / D_MODEL), xhat, rstd, gv)
        dx_ref[...] = dx
        dg_ref[...] += dg

    return pl.pallas_call(
        body,
        name="loss_head",
        grid=(T // ROW_TILE,),
        in_specs=[_rows(D_MODEL), _vmem(), _rows(D_MODEL)],
        out_specs=[_rows(D_MODEL), _vmem(), _vmem()],
        out_shape=[
            jax.ShapeDtypeStruct((T, D_MODEL), F32),
            jax.ShapeDtypeStruct((1, D_MODEL), F32),
            jax.ShapeDtypeStruct((1, D_MODEL), F32),
        ],
        compiler_params=_params("arbitrary"),
    )(x, g, target)


def pool_bwd_mix(dy, p, z, w_mix, scale, w_out):
    T = dy.shape[0]

    def body(dy_ref, p_ref, z_ref, wmix_ref, scale_ref, wout_ref, dp_ref, dz_ref, dwout_ref, dwmix_ref, dscale_ref):
        @pl.when(pl.program_id(0) == 0)
        def _():
            dwout_ref[...] = jnp.zeros_like(dwout_ref)
            dwmix_ref[...] = jnp.zeros_like(dwmix_ref)
            dscale_ref[...] = jnp.zeros_like(dscale_ref)

        dyb = dy_ref[...].astype(BF16)
        da = _dot_nt(dyb, wout_ref[...])
        m_pre = jnp.concatenate(
            [_dot(p_ref[:, gi * POOL_GC:(gi + 1) * POOL_GC], wmix_ref[gi]) for gi in range(len(POOL_WINDOWS))], axis=1)
        sc = scale_ref[...]
        m = m_pre * sc
        zf = z_ref[...].astype(F32)
        silu, dsilu = _silu_parts(zf)
        dwout_ref[...] += _dot_tn((m * silu).astype(BF16), dyb)
        dm = da * silu
        dz_ref[...] = (da * m * dsilu).astype(BF16)
        dscale_ref[...] += jnp.sum(dm * m_pre, axis=0, keepdims=True)
        dmp = (dm * sc).astype(BF16)
        for gi in range(len(POOL_WINDOWS)):
            cols = slice(gi * POOL_GC, (gi + 1) * POOL_GC)
            dwmix_ref[gi] += _dot_tn(p_ref[:, cols], dmp[:, cols])
            dp_ref[:, cols] = _dot_nt(dmp[:, cols], wmix_ref[gi]).astype(BF16)

    return pl.pallas_call(
        body,
        name="pool_bwd_mix",
        grid=(T // ROW_TILE,),
        in_specs=[_rows(D_MODEL), _rows(D_MODEL), _rows(D_MODEL), _vmem(), _vmem(), _vmem()],
        out_specs=[_rows(D_MODEL), _rows(D_MODEL), _vmem(), _vmem(), _vmem()],
        out_shape=[
            jax.ShapeDtypeStruct((T, D_MODEL), BF16),
            jax.ShapeDtypeStruct((T, D_MODEL), BF16),
            jax.ShapeDtypeStruct((D_MODEL, D_MODEL), F32),
            jax.ShapeDtypeStruct((len(POOL_WINDOWS), POOL_GC, POOL_GC), F32),
            jax.ShapeDtypeStruct((1, D_MODEL), F32),
        ],
        compiler_params=_params("arbitrary"),
    )(dy, p, z, w_mix, scale, w_out)


def pool_bwd_in(dp, dz, x, dy, g, w_in):
    T = x.shape[0]
    halo_blocks = ROW_TILE // POOL_HALO
    last_halo = T // POOL_HALO - 1

    def body(dp_ref, dph_ref, dz_ref, x_ref, dy_ref, g_ref, win_ref, dx_ref, dwin_ref, dg_ref):
        i = pl.program_id(0)

        @pl.when(i == 0)
        def _():
            dwin_ref[...] = jnp.zeros_like(dwin_ref)
            dg_ref[...] = jnp.zeros_like(dg_ref)

        rows = ROW_TILE + POOL_HALO
        ext = jnp.concatenate([dp_ref[...], dph_ref[...]], axis=0).astype(F32)
        t = i * ROW_TILE + lax.broadcasted_iota(jnp.int32, (rows, 1), 0)
        inside = (t < T).astype(F32)
        du = []
        for gi, window in enumerate(POOL_WINDOWS):
            cols = slice(gi * POOL_GC, (gi + 1) * POOL_GC)
            s = ext[:, cols] * (_inv_count(i * ROW_TILE, rows, window) * inside)
            shift = 1
            while shift < window:
                s = s + pltpu.roll(s, rows - shift, 0)
                shift *= 2
            du.append(s[:ROW_TILE] - ext[:ROW_TILE, cols])
        du = jnp.concatenate(du, axis=1).astype(BF16)
        chunks = [du[:, :POOL_CHUNK], du[:, POOL_CHUNK:], dz_ref[:, :POOL_CHUNK], dz_ref[:, POOL_CHUNK:]]
        gv = g_ref[...]
        xhat, rstd, h = _norm(x_ref[...], gv)
        h = h.astype(BF16)
        dh = jnp.zeros((ROW_TILE, D_MODEL), F32)
        for c in range(N_CHIPS):
            dwin_ref[c] += _dot_tn(h, chunks[c])
            dh = dh + _dot_nt(chunks[c], win_ref[c])
        dx, dg = _norm_bwd(dh, xhat, rstd, gv)
        dx_ref[...] = dx + dy_ref[...]
        dg_ref[...] += dg

    return pl.pallas_call(
        body,
        name="pool_bwd_in",
        grid=(T // ROW_TILE,),
        in_specs=[_rows(D_MODEL),
                  pl.BlockSpec((POOL_HALO, D_MODEL), lambda i: (jnp.minimum((i + 1) * halo_blocks, last_halo), 0)),
                  _rows(D_MODEL), _rows(D_MODEL), _rows(D_MODEL), _vmem(), _vmem()],
        out_specs=[_rows(D_MODEL), _vmem(), _vmem()],
        out_shape=[
            jax.ShapeDtypeStruct((T, D_MODEL), F32),
            jax.ShapeDtypeStruct((N_CHIPS, D_MODEL, POOL_CHUNK), F32),
            jax.ShapeDtypeStruct((1, D_MODEL), F32),
        ],
        compiler_params=_params("arbitrary"),
    )(dp, dp, dz, x, dy, g, w_in)


def attn_bwd_core(dy, q, k, v, z, o, lse, w_out, sinks):
    T = dy.shape[0]
    tiles = T // ROW_TILE
    blocks = ROW_TILE // ATTN_BLOCK

    def body(dy_ref, q_ref, k_ref, kh_ref, v_ref, vh_ref, z_ref, o_ref, lse_ref, w_ref, sink_ref,
             dq_ref, dk_ref, dv_ref, dkh_ref, dvh_ref, dz_ref, dwout_ref, dsink_ref, do_s, dq_s, dk_s, dv_s):
        @pl.when(pl.program_id(0) == 0)
        def _():
            dwout_ref[...] = jnp.zeros_like(dwout_ref)
            dsink_ref[...] = jnp.zeros_like(dsink_ref)

        dyb = dy_ref[...].astype(BF16)
        da = _dot_nt(dyb, w_ref[...])
        of = o_ref[...].astype(F32)
        silu, dsilu = _silu_parts(z_ref[...].astype(F32))
        dwout_ref[...] += _dot_tn((of * silu).astype(BF16), dyb)
        do = da * silu
        dz_ref[...] = (da * of * dsilu).astype(BF16)
        do_s[...] = do.astype(BF16)
        dof = do * of
        dk_s[...] = jnp.zeros_like(dk_s)
        dv_s[...] = jnp.zeros_like(dv_s)
        first_mask, band_mask = _band_masks(pl.program_id(0) == 0)
        for hk in range(N_KV_HEADS):
            sink = _sink_column(sink_ref, hk)
            kv_cols = slice(hk * HEAD_DIM, (hk + 1) * HEAD_DIM)
            for b in range(blocks):
                rows = slice(b * ATTN_BLOCK, (b + 1) * ATTN_BLOCK)
                heads = [hk * GROUP + gi for gi in range(GROUP)]
                qg = _stack_heads(q_ref, rows, hk)
                dog = _stack_heads(do_s, rows, hk)
                lse_c = jnp.concatenate([lse_ref[rows, h:h + 1] for h in heads], axis=0)
                delta = jnp.concatenate(
                    [jnp.sum(dof[rows, h * HEAD_DIM:(h + 1) * HEAD_DIM], axis=-1, keepdims=True) for h in heads], axis=0)
                kb = _band(k_ref, kh_ref, b, hk)
                vb = _band(v_ref, vh_ref, b, hk)
                s = jnp.where(first_mask if b == 0 else band_mask, _dot_nt(qg, kb), NEG_BIG)
                p = jnp.exp(s - lse_c)
                dp = _dot_nt(dog, vb)
                ds = (p * (dp - delta)).astype(BF16)
                dqg = _dot(ds, kb) * (1.0 / math.sqrt(HEAD_DIM))
                band_rows = slice(b * ATTN_BLOCK, (b + 2) * ATTN_BLOCK)
                dk_s[band_rows, kv_cols] += _dot_tn(ds, qg)
                dv_s[band_rows, kv_cols] += _dot_tn(p.astype(BF16), dog)
                dsink = -jnp.exp(sink - lse_c) * delta
                for gi, h in enumerate(heads):
                    part = slice(gi * ATTN_BLOCK, (gi + 1) * ATTN_BLOCK)
                    dq_s[rows, h * HEAD_DIM:(h + 1) * HEAD_DIM] = dqg[part]
                    dsink_ref[0:1, h:h + 1] += jnp.sum(dsink[part], axis=0, keepdims=True)
        dq_ref[...] = dq_s[...].astype(BF16)
        dk_ref[...] = dk_s[ATTN_BLOCK:]
        dv_ref[...] = dv_s[ATTN_BLOCK:]
        dkh_ref[0] = dk_s[:ATTN_BLOCK]
        dvh_ref[0] = dv_s[:ATTN_BLOCK]

    halo_out = pl.BlockSpec((1, ATTN_BLOCK, KV_WIDTH), lambda i: (i, 0, 0))
    return pl.pallas_call(
        body,
        name="attn_bwd_core",
        grid=(tiles,),
        in_specs=[_rows(D_MODEL), _rows(D_MODEL), _rows(KV_WIDTH), _halo_prev(KV_WIDTH), _rows(KV_WIDTH),
                  _halo_prev(KV_WIDTH), _rows(D_MODEL), _rows(D_MODEL), _rows(N_HEADS), _vmem(),
                  pl.BlockSpec(memory_space=pltpu.SMEM)],
        out_specs=[_rows(D_MODEL), _rows(KV_WIDTH), _rows(KV_WIDTH), halo_out, halo_out, _rows(D_MODEL), _vmem(), _vmem()],
        out_shape=[
            jax.ShapeDtypeStruct((T, D_MODEL), BF16),
            jax.ShapeDtypeStruct((T, KV_WIDTH), F32),
            jax.ShapeDtypeStruct((T, KV_WIDTH), F32),
            jax.ShapeDtypeStruct((tiles, ATTN_BLOCK, KV_WIDTH), F32),
            jax.ShapeDtypeStruct((tiles, ATTN_BLOCK, KV_WIDTH), F32),
            jax.ShapeDtypeStruct((T, D_MODEL), BF16),
            jax.ShapeDtypeStruct((D_MODEL, D_MODEL), F32),
            jax.ShapeDtypeStruct((8, 128), F32),
        ],
        scratch_shapes=[
            pltpu.VMEM((ROW_TILE, D_MODEL), BF16),
            pltpu.VMEM((ROW_TILE, D_MODEL), F32),
            pltpu.VMEM((ROW_TILE + ATTN_BLOCK, KV_WIDTH), F32),
            pltpu.VMEM((ROW_TILE + ATTN_BLOCK, KV_WIDTH), F32),
        ],
        compiler_params=_params("arbitrary"),
    )(dy, q, k, k, v, v, z, o, lse, w_out, sinks)


def attn_bwd_in(dq, dk, dv, dkh, dvh, dz, x, dy, g, w_in):
    T = x.shape[0]
    tiles = T // ROW_TILE

    def body(dq_ref, dk_ref, dv_ref, dkh_ref, dvh_ref, dz_ref, x_ref, dy_ref, g_ref, win_ref, dx_ref, dwin_ref, dg_ref):
        i = pl.program_id(0)

        @pl.when(i == 0)
        def _():
            dwin_ref[...] = jnp.zeros_like(dwin_ref)
            dg_ref[...] = jnp.zeros_like(dg_ref)

        more = (i < tiles - 1).astype(F32)
        tail = jnp.concatenate([jnp.zeros((ROW_TILE - ATTN_BLOCK, KV_WIDTH), F32), dkh_ref[0] * more], axis=0)
        dkb = (dk_ref[...] + tail).astype(BF16)
        tail = jnp.concatenate([jnp.zeros((ROW_TILE - ATTN_BLOCK, KV_WIDTH), F32), dvh_ref[0] * more], axis=0)
        dvb = (dv_ref[...] + tail).astype(BF16)
        chunks = [
            dq_ref[:, :ATTN_CHUNK],
            jnp.concatenate([dq_ref[:, ATTN_CHUNK:], dkb], axis=1),
            jnp.concatenate([dvb, dz_ref[:, :384]], axis=1),
            dz_ref[:, 384:],
        ]
        gv = g_ref[...]
        xhat, rstd, h = _norm(x_ref[...], gv)
        h = h.astype(BF16)
        dh = jnp.zeros((ROW_TILE, D_MODEL), F32)
        for c in range(N_CHIPS):
            dwin_ref[c] += _dot_tn(h, chunks[c])
            dh = dh + _dot_nt(chunks[c], win_ref[c])
        dx, dg = _norm_bwd(dh, xhat, rstd, gv)
        dx_ref[...] = dx + dy_ref[...]
        dg_ref[...] += dg

    halo_next = pl.BlockSpec((1, ATTN_BLOCK, KV_WIDTH), lambda i: (jnp.minimum(i + 1, tiles - 1), 0, 0))
    return pl.pallas_call(
        body,
        name="attn_bwd_in",
        grid=(tiles,),
        in_specs=[_rows(D_MODEL), _rows(KV_WIDTH), _rows(KV_WIDTH), halo_next, halo_next, _rows(D_MODEL),
                  _rows(D_MODEL), _rows(D_MODEL), _vmem(), _vmem()],
        out_specs=[_rows(D_MODEL), _vmem(), _vmem()],
        out_shape=[
            jax.ShapeDtypeStruct((T, D_MODEL), F32),
            jax.ShapeDtypeStruct((N_CHIPS, D_MODEL, ATTN_CHUNK), F32),
            jax.ShapeDtypeStruct((1, D_MODEL), F32),
        ],
        compiler_params=_params("arbitrary"),
    )(dq, dk, dv, dkh, dvh, dz, x, dy, g, w_in)


def local_step(x, target, norm_g, sinks, final_g, attn_in_w, attn_out_w, pool_in_w, pool_mix_w, pool_scale, pool_out_w):
    saved = []
    for layer in range(4):
        j = layer // 2
        g = norm_g[layer][None, :]
        if layer % 2 == 0:
            q, k, v, z = attn_in_fwd(x, g, attn_in_w[j])
            x_new, o, lse = attn_core_fwd(q, k, v, z, x, attn_out_w[j], sinks[j])
            saved.append((x, g, q, k, v, z, o, lse))
        else:
            x_new, p, z = pool_fwd(x, g, pool_in_w[j], pool_mix_w[j], pool_scale[j], pool_out_w[j])
            saved.append((x, g, p, z))
        x = x_new
    dx, sq, d_final_g = loss_head(x, final_g[None, :], target)
    grads = {"norm_g": [None] * 4, "attn_w_in": [None] * 2, "attn_sinks": [None] * 2, "attn_w_out": [None] * 2,
             "pool_w_in": [None] * 2, "pool_w_mix": [None] * 2, "pool_scale": [None] * 2, "pool_w_out": [None] * 2,
             "final_g": d_final_g}
    for layer in reversed(range(4)):
        j = layer // 2
        if layer % 2 == 0:
            x_in, g, q, k, v, z, o, lse = saved[layer]
            dq, dk, dv, dkh, dvh, dz, dwout, dsink = attn_bwd_core(dx, q, k, v, z, o, lse, attn_out_w[j], sinks[j])
            dx, dwin, dg = attn_bwd_in(dq, dk, dv, dkh, dvh, dz, x_in, dx, g, attn_in_w[j])
            grads["attn_w_in"][j], grads["attn_w_out"][j], grads["attn_sinks"][j] = dwin, dwout, dsink[0, :N_HEADS]
        else:
            x_in, g, p, z = saved[layer]
            dp, dz, dwout, dwmix, dscale = pool_bwd_mix(dx, p, z, pool_mix_w[j], pool_scale[j], pool_out_w[j])
            dx, dwin, dg = pool_bwd_in(dp, dz, x_in, dx, g, pool_in_w[j])
            grads["pool_w_in"][j], grads["pool_w_out"][j] = dwin, dwout
            grads["pool_w_mix"][j], grads["pool_scale"][j] = dwmix, dscale
        grads["norm_g"][layer] = dg
    return sq, dx, grads


SHARD_PIECES = (("attn_w_in", 640), ("attn_w_out", 256), ("pool_w_in", 512), ("pool_w_mix", 64), ("pool_w_out", 256))
SHARD_ROWS = sum(rows for _, rows in SHARD_PIECES)
GATHER_ROWS = SHARD_ROWS + 16
SMALL_ROWS = 8


def _piece_offsets():
    out, at = {}, 0
    for name, rows in SHARD_PIECES:
        out[name] = (at, at + rows)
        at += rows
    return out


PIECE_AT = _piece_offsets()


def _pack_shard(attn_w_in, attn_w_out, pool_w_in, pool_w_mix, pool_w_out):
    given = {"attn_w_in": attn_w_in, "attn_w_out": attn_w_out, "pool_w_in": pool_w_in, "pool_w_mix": pool_w_mix,
             "pool_w_out": pool_w_out}
    return jnp.concatenate([given[name].reshape(2, rows, D_MODEL) for name, rows in SHARD_PIECES], axis=1)


def _unpack_shard(flat):
    shapes = {"attn_w_in": (2, D_MODEL, ATTN_CHUNK), "attn_w_out": (2, 256, D_MODEL), "pool_w_in": (2, D_MODEL, POOL_CHUNK),
              "pool_w_mix": (2, 4, 64, POOL_GC), "pool_w_out": (2, 256, D_MODEL)}
    return {name: flat[:, PIECE_AT[name][0]:PIECE_AT[name][1]].reshape(shapes[name]) for name, _ in SHARD_PIECES}


def _mesh_position():
    return lax.axis_index("x"), lax.axis_index("y"), lax.axis_index("c")


def _other_chips(x, y):
    return [(1 - x, y), (x, 1 - y), (1 - x, 1 - y)]


def _any():
    return pl.BlockSpec(memory_space=pl.ANY)


def gather_weights(shard):
    def body(w_ref, out_ref, send_sems, recv_sems, local_sem):
        x, y, c = _mesh_position()
        me = 2 * x + y
        sibling = (x, y, 1 - c)
        chips = _other_chips(x, y)

        def copy(sem, chip_index, half, to, src=None):
            dst = out_ref.at[chip_index, half]
            return pltpu.make_async_remote_copy(
                src_ref=dst if src is None else src, dst_ref=dst, send_sem=send_sems.at[sem], recv_sem=recv_sems.at[sem],
                device_id=to, device_id_type=MESH_ID)

        mine = pltpu.make_async_copy(w_ref, out_ref.at[me], local_sem)
        mine.start()
        first = [copy(j, me, c, (cx, cy, c), src=w_ref.at[c]) for j, (cx, cy) in enumerate(chips)]
        for cp in first:
            cp.start()
        passed = [copy(3 + j, 2 * cx + cy, c, sibling) for j, (cx, cy) in enumerate(chips)]
        for j, (cx, cy) in enumerate(chips):
            copy(j, 2 * cx + cy, c, (x, y, c)).wait_recv()
            passed[j].start()
        for j, (cx, cy) in enumerate(chips):
            copy(3 + j, 2 * cx + cy, 1 - c, (x, y, c)).wait_recv()
        for cp in first + passed:
            cp.wait_send()
        mine.wait()

    return pl.pallas_call(
        body,
        name="gather_weights",
        in_specs=[_any()],
        out_specs=_any(),
        out_shape=jax.ShapeDtypeStruct((N_CHIPS,) + shard.shape, shard.dtype),
        scratch_shapes=[pltpu.SemaphoreType.DMA((6,)), pltpu.SemaphoreType.DMA((6,)), pltpu.SemaphoreType.DMA],
    )(shard)


def sibling_swap_halves(grads):
    n = grads.shape[0]

    def body(g_ref, out_ref, send_sems, recv_sems):
        x, y, c = _mesh_position()
        copies = [
            pltpu.make_async_remote_copy(
                src_ref=g_ref.at[k, 1 - c], dst_ref=out_ref.at[k], send_sem=send_sems.at[k], recv_sem=recv_sems.at[k],
                device_id=(x, y, 1 - c), device_id_type=MESH_ID)
            for k in range(n)
        ]
        for cp in copies:
            cp.start()
        for cp in copies:
            cp.wait()

    return pl.pallas_call(
        body,
        name="sibling_swap_halves",
        in_specs=[_any()],
        out_specs=_any(),
        out_shape=jax.ShapeDtypeStruct((n,) + grads.shape[2:], grads.dtype),
        scratch_shapes=[pltpu.SemaphoreType.DMA((n,)), pltpu.SemaphoreType.DMA((n,))],
    )(grads)


SUM_TILE = 432


def chip_sum(grads, from_sibling, core):
    n, _, rows, width = grads.shape

    def body(core_ref, a_ref, b_ref, o_ref):
        o_ref[0] = (a_ref[0, 0] + b_ref[0]).astype(BF16)

    return pl.pallas_call(
        body,
        name="chip_sum",
        grid_spec=pltpu.PrefetchScalarGridSpec(
            num_scalar_prefetch=1,
            grid=(n, rows // SUM_TILE),
            in_specs=[pl.BlockSpec((1, 1, SUM_TILE, width), lambda k, r, core_ref: (k, core_ref[0], r, 0)),
                      pl.BlockSpec((1, SUM_TILE, width), lambda k, r, core_ref: (k, r, 0))],
            out_specs=pl.BlockSpec((1, SUM_TILE, width), lambda k, r, core_ref: (k, r, 0)),
        ),
        out_shape=jax.ShapeDtypeStruct((n, rows, width), BF16),
    )(core, grads, from_sibling)


def scatter_to_chips(parts):
    def body(p_ref, out_ref, send_sems, recv_sems, local_sem):
        x, y, c = _mesh_position()
        me = 2 * x + y
        chips = _other_chips(x, y)
        mine = pltpu.make_async_copy(p_ref.at[me], out_ref.at[me], local_sem)
        mine.start()
        sends = [
            pltpu.make_async_remote_copy(
                src_ref=p_ref.at[2 * cx + cy], dst_ref=out_ref.at[me], send_sem=send_sems.at[j], recv_sem=recv_sems.at[j],
                device_id=(cx, cy, c), device_id_type=MESH_ID)
            for j, (cx, cy) in enumerate(chips)
        ]
        for cp in sends:
            cp.start()
        for j, (cx, cy) in enumerate(chips):
            pltpu.make_async_remote_copy(
                src_ref=p_ref.at[me], dst_ref=out_ref.at[2 * cx + cy], send_sem=send_sems.at[j], recv_sem=recv_sems.at[j],
                device_id=(cx, cy, c), device_id_type=MESH_ID).wait_recv()
        for cp in sends:
            cp.wait_send()
        mine.wait()

    return pl.pallas_call(
        body,
        name="scatter_to_chips",
        in_specs=[_any()],
        out_specs=_any(),
        out_shape=jax.ShapeDtypeStruct(parts.shape, parts.dtype),
        scratch_shapes=[pltpu.SemaphoreType.DMA((3,)), pltpu.SemaphoreType.DMA((3,)), pltpu.SemaphoreType.DMA],
    )(parts)


def sum_chips(parts):
    n, rows, width = parts.shape

    def body(p_ref, o_ref):
        total = p_ref[0].astype(F32)
        for k in range(1, n):
            total = total + p_ref[k].astype(F32)
        o_ref[...] = total

    return pl.pallas_call(
        body,
        name="sum_chips",
        grid=(rows // SUM_TILE,),
        in_specs=[pl.BlockSpec((n, SUM_TILE, width), lambda r: (0, r, 0))],
        out_specs=pl.BlockSpec((SUM_TILE, width), lambda r: (r, 0)),
        out_shape=jax.ShapeDtypeStruct((rows, width), F32),
    )(parts)


def sibling_join_halves(half):
    def body(h_ref, out_ref, send_sem, recv_sem, local_sem):
        x, y, c = _mesh_position()
        mine = pltpu.make_async_copy(h_ref, out_ref.at[c], local_sem)
        mine.start()
        cp = pltpu.make_async_remote_copy(
            src_ref=h_ref, dst_ref=out_ref.at[c], send_sem=send_sem, recv_sem=recv_sem,
            device_id=(x, y, 1 - c), device_id_type=MESH_ID)
        cp.start()
        pltpu.make_async_remote_copy(
            src_ref=h_ref, dst_ref=out_ref.at[1 - c], send_sem=send_sem, recv_sem=recv_sem,
            device_id=(x, y, 1 - c), device_id_type=MESH_ID).wait_recv()
        cp.wait_send()
        mine.wait()

    return pl.pallas_call(
        body,
        name="sibling_join_halves",
        in_specs=[_any()],
        out_specs=_any(),
        out_shape=jax.ShapeDtypeStruct((2,) + half.shape, half.dtype),
        scratch_shapes=[pltpu.SemaphoreType.DMA, pltpu.SemaphoreType.DMA, pltpu.SemaphoreType.DMA],
    )(half)


def all_reduce_small(v):
    n_dev = 8

    def body(v_ref, out_ref, slots, send_sems, recv_sems):
        x, y, c = _mesh_position()
        me = 4 * x + 2 * y + c
        slots[me] = v_ref[...]
        peers = []
        for r in range(1, n_dev):
            fx, fy, fc = (r >> 2) & 1, (r >> 1) & 1, r & 1
            peers.append((1 - x if fx else x, 1 - y if fy else y, 1 - c if fc else c))
        sends = [
            pltpu.make_async_remote_copy(
                src_ref=v_ref, dst_ref=slots.at[me], send_sem=send_sems.at[r], recv_sem=recv_sems.at[r],
                device_id=peer, device_id_type=MESH_ID)
            for r, peer in enumerate(peers)
        ]
        for cp in sends:
            cp.start()
        for r, (px, py, pc) in enumerate(peers):
            pltpu.make_async_remote_copy(
                src_ref=v_ref, dst_ref=slots.at[4 * px + 2 * py + pc], send_sem=send_sems.at[r], recv_sem=recv_sems.at[r],
                device_id=(px, py, pc), device_id_type=MESH_ID).wait_recv()
        for cp in sends:
            cp.wait_send()
        total = slots[0]
        for d in range(1, n_dev):
            total = total + slots[d]
        out_ref[...] = total

    return pl.pallas_call(
        body,
        name="all_reduce_small",
        in_specs=[_vmem()],
        out_specs=_vmem(),
        out_shape=jax.ShapeDtypeStruct(v.shape, v.dtype),
        scratch_shapes=[pltpu.VMEM((n_dev,) + v.shape, v.dtype), pltpu.SemaphoreType.DMA((n_dev - 1,)),
                        pltpu.SemaphoreType.DMA((n_dev - 1,))],
    )(v)


def adamw(w, g, m, v, tile):
    rows, width = w.shape

    def body(w_ref, g_ref, m_ref, v_ref, d_ref, mo_ref, vo_ref):
        gv = g_ref[...]
        m2 = ADAM_B1 * m_ref[...] + (1.0 - ADAM_B1) * gv
        v2 = ADAM_B2 * v_ref[...] + (1.0 - ADAM_B2) * (gv * gv)
        m_hat = m2 / (1.0 - ADAM_B1 ** ADAM_STEP)
        v_hat = v2 / (1.0 - ADAM_B2 ** ADAM_STEP)
        d_ref[...] = -ADAM_LR * (m_hat / (jnp.sqrt(v_hat) + ADAM_EPS) + ADAM_WD * w_ref[...])
        mo_ref[...] = m2
        vo_ref[...] = v2

    spec = pl.BlockSpec((tile, width), lambda r: (r, 0))
    shape = jax.ShapeDtypeStruct((rows, width), F32)
    return pl.pallas_call(
        body, name="adamw", grid=(rows // tile,), in_specs=[spec] * 4, out_specs=[spec] * 3, out_shape=[shape] * 3,
    )(w, g, m, v)


def _chip_major(w, chunk):
    return w.reshape(N_CHIPS, D_MODEL, chunk)


def kernel(x, norm_g, attn_w_in, attn_sinks, attn_w_out, pool_w_in, pool_w_mix, pool_scale, pool_w_out, final_g, loss_target, m_norm_g, m_attn_w_in, m_attn_sinks, m_attn_w_out, m_pool_w_in, m_pool_w_mix, m_pool_scale, m_pool_w_out, m_final_g, v_norm_g, v_attn_w_in, v_attn_sinks, v_attn_w_out, v_pool_w_in, v_pool_w_mix, v_pool_scale, v_pool_w_out, v_final_g):
    cx, cy, cc = _mesh_position()
    chip = 2 * cx + cy

    w_flat = _pack_shard(attn_w_in, attn_w_out, pool_w_in, pool_w_mix, pool_w_out)
    scale_bits = lax.bitcast_convert_type(pool_scale, BF16).reshape(2, 1, 2 * 256)
    extra = jnp.concatenate([scale_bits, jnp.zeros((2, 1, D_MODEL - 512), BF16)], axis=2)
    extra = jnp.concatenate([extra, jnp.zeros((2, GATHER_ROWS - SHARD_ROWS - 1, D_MODEL), BF16)], axis=1)
    gathered = gather_weights(jnp.concatenate([w_flat.astype(BF16), extra], axis=1))

    def piece(name):
        lo, hi = PIECE_AT[name]
        return gathered[:, :, lo:hi]

    attn_in_w = jnp.stack([_chip_major(piece("attn_w_in")[:, j], ATTN_CHUNK) for j in range(2)])
    attn_out_w = jnp.stack([piece("attn_w_out")[:, j].reshape(D_MODEL, D_MODEL) for j in range(2)])
    pool_in_w = jnp.stack([_chip_major(piece("pool_w_in")[:, j], POOL_CHUNK) for j in range(2)])
    pool_mix_w = jnp.stack([
        piece("pool_w_mix")[:, j].reshape(N_CHIPS, 4, 64, POOL_GC).transpose(1, 0, 2, 3).reshape(4, POOL_GC, POOL_GC)
        for j in range(2)])
    pool_out_w = jnp.stack([piece("pool_w_out")[:, j].reshape(D_MODEL, D_MODEL) for j in range(2)])
    scale_full = lax.bitcast_convert_type(gathered[:, :, SHARD_ROWS, :512].reshape(N_CHIPS, 2, 256, 2), F32)
    scale_full = scale_full.transpose(1, 0, 2).reshape(2, 1, D_MODEL)

    sq, grad_x, g = local_step(x[0], loss_target[0], norm_g, attn_sinks, final_g, attn_in_w, attn_out_w, pool_in_w,
                               pool_mix_w, scale_full, pool_out_w)

    last = jnp.concatenate([g["attn_sinks"][0], g["attn_sinks"][1], jnp.sum(sq).reshape(1),
                            jnp.zeros((D_MODEL - 2 * N_HEADS - 1,), F32)])[None, :]
    small = jnp.concatenate(g["norm_g"] + [g["final_g"]] + g["pool_scale"] + [last], axis=0)
    small = all_reduce_small(small)
    loss = 0.5 * small[7, 2 * N_HEADS] / D_MODEL
    g_norm = small[0:4]
    g_final = small[4]
    g_scale = lax.dynamic_slice(small[5:7], (0, chip * 256), (2, 256))
    g_sinks = small[7, :2 * N_HEADS].reshape(2, N_HEADS)

    per_layer = []
    for j in range(2):
        mix = g["pool_w_mix"][j].reshape(4, N_CHIPS, 64, POOL_GC).transpose(1, 0, 2, 3)
        per_layer.append(jnp.concatenate([
            g["attn_w_in"][j].reshape(N_CHIPS, 640, D_MODEL), g["attn_w_out"][j].reshape(N_CHIPS, 256, D_MODEL),
            g["pool_w_in"][j].reshape(N_CHIPS, 512, D_MODEL), mix.reshape(N_CHIPS, 64, D_MODEL),
            g["pool_w_out"][j].reshape(N_CHIPS, 256, D_MODEL)], axis=1))
    g_all = jnp.stack(per_layer, axis=1)
    from_sibling = sibling_swap_halves(g_all)
    parts = chip_sum(g_all, from_sibling, cc.astype(jnp.int32).reshape(1))
    landed = scatter_to_chips(parts)
    g_shard = sibling_join_halves(sum_chips(landed))

    flat_rows = 2 * SHARD_ROWS
    big = [t.reshape(flat_rows, D_MODEL) for t in (
        w_flat, g_shard, _pack_shard(m_attn_w_in, m_attn_w_out, m_pool_w_in, m_pool_w_mix, m_pool_w_out),
        _pack_shard(v_attn_w_in, v_attn_w_out, v_pool_w_in, v_pool_w_mix, v_pool_w_out))]
    d_big, m_big, v_big = (_unpack_shard(t.reshape(2, SHARD_ROWS, D_MODEL)) for t in adamw(*big, tile=SUM_TILE))
    g_big = _unpack_shard(g_shard)

    def small_pack(ng, fg, sc, sk):
        row7 = jnp.concatenate([sk.reshape(2 * N_HEADS), jnp.zeros((D_MODEL - 2 * N_HEADS,), F32)])[None, :]
        sc = jnp.concatenate([sc, jnp.zeros((2, D_MODEL - 256), F32)], axis=1)
        return jnp.concatenate([ng, fg[None, :], sc, row7], axis=0)

    packs = [small_pack(norm_g, final_g, pool_scale, attn_sinks), small_pack(g_norm, g_final, g_scale, g_sinks),
             small_pack(m_norm_g, m_final_g, m_pool_scale, m_attn_sinks), small_pack(v_norm_g, v_final_g, v_pool_scale, v_attn_sinks)]
    small_out = adamw(*packs, tile=SMALL_ROWS)

    def small_unpack(t):
        return {"norm_g": t[0:4], "final_g": t[4], "pool_scale": t[5:7, :256], "attn_sinks": t[7, :2 * N_HEADS].reshape(2, N_HEADS)}

    d_small, m_small, v_small = (small_unpack(t) for t in small_out)
    g_small = {"norm_g": g_norm, "final_g": g_final, "pool_scale": g_scale, "attn_sinks": g_sinks}

    names = ["norm_g", "attn_w_in", "attn_sinks", "attn_w_out", "pool_w_in", "pool_w_mix", "pool_scale", "pool_w_out", "final_g"]

    def ordered(bigs, smalls):
        return [bigs[n] if n in bigs else smalls[n] for n in names]

    return (loss, grad_x[None], *ordered(g_big, g_small), *ordered(d_big, d_small), *ordered(m_big, m_small),
            *ordered(v_big, v_small))
```

```python
import functools
import math

import jax
import jax.numpy as jnp
from jax import lax
from jax.experimental import pallas as pl
from jax.experimental.pallas import tpu as pltpu

F32 = jnp.float32
BF16 = jnp.bfloat16

D_MODEL = 1024
N_HEADS = 16
N_KV_HEADS = 4
GROUP = N_HEADS // N_KV_HEADS
HEAD_DIM = 64
KV_WIDTH = N_KV_HEADS * HEAD_DIM
ATTN_BLOCK = 128
POOL_WINDOWS = (2, 4, 8, 16)
POOL_GC = 256
POOL_HALO = 16
EPS = 1e-6
N_CHIPS = 4
ATTN_CHUNK = 640
POOL_CHUNK = 512
ROW_TILE = 512
NEG_BIG = -1e30
VMEM_LIMIT_BYTES = 60 * 1024 * 1024

ADAM_LR = 0.001
ADAM_B1 = 0.9
ADAM_B2 = 0.999
ADAM_EPS = 1e-08
ADAM_WD = 0.01
ADAM_STEP = 10

MESH_ID = pl.DeviceIdType.MESH


def _dot(a, b):
    return jnp.dot(a, b, preferred_element_type=F32)


def _dot_nt(a, b):
    return lax.dot_general(a, b, (((1,), (1,)), ((), ())), preferred_element_type=F32)


def _dot_tn(a, b):
    return lax.dot_general(a, b, (((0,), (0,)), ((), ())), preferred_element_type=F32)


def _vmem():
    return pl.BlockSpec(memory_space=pltpu.VMEM)


def _rows(width, tile=ROW_TILE):
    return pl.BlockSpec((tile, width), lambda i: (i, 0))


def _params(semantics):
    return pltpu.CompilerParams(dimension_semantics=(semantics,), vmem_limit_bytes=VMEM_LIMIT_BYTES)


def _norm(xf, g):
    rstd = lax.rsqrt(jnp.mean(xf * xf, axis=-1, keepdims=True) + EPS)
    xhat = xf * rstd
    return xhat, rstd, xhat * g


def _norm_bwd(dh, xhat, rstd, g):
    dg = jnp.sum(dh * xhat, axis=0, keepdims=True)
    dxhat = dh * g
    dx = rstd * (dxhat - xhat * jnp.mean(dxhat * xhat, axis=-1, keepdims=True))
    return dx, dg


def _silu_parts(zf):
    sig = jax.nn.sigmoid(zf)
    return zf * sig, sig * (1.0 + zf * (1.0 - sig))


def _cols(height, tile=ROW_TILE):
    return pl.BlockSpec((height, tile), lambda i: (0, i))


def _halo_prev_rows(width):
    per_tile = ROW_TILE // ATTN_BLOCK
    return pl.BlockSpec((ATTN_BLOCK, width), lambda i: (jnp.maximum(i * per_tile - 1, 0), 0))


def _halo_prev_cols(height):
    per_tile = ROW_TILE // ATTN_BLOCK
    return pl.BlockSpec((height, ATTN_BLOCK), lambda i: (0, jnp.maximum(i * per_tile - 1, 0)))


def attn_in_fwd(x, g, wq_t, wkv_t, wz_t, wkv):
    T = x.shape[0]

    def body(x_ref, g_ref, wq_ref, wkvt_ref, wz_ref, wkv_ref, qt_ref, kv_ref, kvt_ref, zt_ref):
        _, _, h = _norm(x_ref[...], g_ref[...])
        h = h.astype(BF16)
        scale = 1.0 / math.sqrt(HEAD_DIM)
        qt_ref[...] = (_dot_nt(wq_ref[...], h) * scale).astype(BF16)
        zt_ref[...] = _dot_nt(wz_ref[...], h).astype(BF16)
        kvt_ref[...] = _dot_nt(wkvt_ref[...], h).astype(BF16)
        kv_ref[...] = _dot(h, wkv_ref[...]).astype(BF16)

    return pl.pallas_call(
        body,
        name="attn_in_fwd",
        grid=(T // ROW_TILE,),
        in_specs=[_rows(D_MODEL), _vmem(), _vmem(), _vmem(), _vmem(), _vmem()],
        out_specs=[_cols(D_MODEL), _rows(2 * KV_WIDTH), _cols(2 * KV_WIDTH), _cols(D_MODEL)],
        out_shape=[
            jax.ShapeDtypeStruct((D_MODEL, T), BF16),
            jax.ShapeDtypeStruct((T, 2 * KV_WIDTH), BF16),
            jax.ShapeDtypeStruct((2 * KV_WIDTH, T), BF16),
            jax.ShapeDtypeStruct((D_MODEL, T), BF16),
        ],
        compiler_params=_params("parallel"),
    )(x, g, wq_t, wkv_t, wz_t, wkv)


def _causal_triangle():
    shape = (ATTN_BLOCK, GROUP * ATTN_BLOCK)
    kj = lax.broadcasted_iota(jnp.int32, shape, 0)
    qi = lax.broadcasted_iota(jnp.int32, shape, 1) & (ATTN_BLOCK - 1)
    return kj <= qi


def _group_cols(ref, hk, cols):
    return jnp.concatenate(
        [ref[(hk * GROUP + gi) * HEAD_DIM:(hk * GROUP + gi + 1) * HEAD_DIM, cols] for gi in range(GROUP)], axis=1)


def _group_row(values):
    return jnp.concatenate(values, axis=1)


def _block_pair(ref, halo_ref, b, lanes):
    cur = ref[b * ATTN_BLOCK:(b + 1) * ATTN_BLOCK, lanes]
    prev = halo_ref[:, lanes] if b == 0 else ref[(b - 1) * ATTN_BLOCK:b * ATTN_BLOCK, lanes]
    return cur, prev


def _block_pair_t(ref, halo_ref, b, rows):
    cur = ref[rows, b * ATTN_BLOCK:(b + 1) * ATTN_BLOCK]
    prev = halo_ref[rows, :] if b == 0 else ref[rows, (b - 1) * ATTN_BLOCK:b * ATTN_BLOCK]
    return cur, prev


def _window_scores(kc, kp, qt, tri, b, first_penalty):
    sp = _dot(kp, qt)
    if b == 0:
        sp = sp + first_penalty
    return jnp.where(tri, _dot(kc, qt), sp)


def attn_core_fwd(q_t, kv, kv_t, z_t, x, w_out, sinks):
    T = x.shape[0]
    blocks = ROW_TILE // ATTN_BLOCK

    def body(qt_ref, kv_ref, kvh_ref, kvt_ref, kvth_ref, zt_ref, x_ref, w_ref, sink_ref, xo_ref, ot_ref, lse_ref, oacc):
        tri = _causal_triangle()
        first_penalty = jnp.where(pl.program_id(0) == 0, NEG_BIG, 0.0)
        for hk in range(N_KV_HEADS):
            heads = [hk * GROUP + gi for gi in range(GROUP)]
            sink = _group_row([jnp.full((1, ATTN_BLOCK), sink_ref[h], F32) for h in heads])
            k_lanes = slice(hk * HEAD_DIM, (hk + 1) * HEAD_DIM)
            v_rows = slice(KV_WIDTH + hk * HEAD_DIM, KV_WIDTH + (hk + 1) * HEAD_DIM)
            for b in range(blocks):
                cols = slice(b * ATTN_BLOCK, (b + 1) * ATTN_BLOCK)
                qt = _group_cols(qt_ref, hk, cols)
                kc, kp = _block_pair(kv_ref, kvh_ref, b, k_lanes)
                s = _window_scores(kc, kp, qt, tri, b, first_penalty)
                m = jnp.maximum(jnp.max(s, axis=0, keepdims=True), sink)
                p = jnp.exp(s - m)
                denom = jnp.sum(p, axis=0, keepdims=True) + jnp.exp(sink - m)
                pb = p.astype(BF16)
                zero = jnp.zeros_like(pb)
                vtc, vtp = _block_pair_t(kvt_ref, kvth_ref, b, v_rows)
                o = (_dot(vtc, jnp.where(tri, pb, zero)) + _dot(vtp, jnp.where(tri, zero, pb))) * (1.0 / denom)
                lse = m + jnp.log(denom)
                for gi, h in enumerate(heads):
                    part = slice(gi * ATTN_BLOCK, (gi + 1) * ATTN_BLOCK)
                    oacc[h * HEAD_DIM:(h + 1) * HEAD_DIM, cols] = o[:, part]
                    lse_ref[h:h + 1, cols] = lse[:, part]
        of = oacc[...]
        silu, _ = _silu_parts(zt_ref[...].astype(F32))
        y = _dot_tn((of * silu).astype(BF16), w_ref[...])
        xo_ref[...] = x_ref[...] + y
        ot_ref[...] = of.astype(BF16)

    return pl.pallas_call(
        body,
        name="attn_core_fwd",
        grid=(T // ROW_TILE,),
        in_specs=[_cols(D_MODEL), _rows(2 * KV_WIDTH), _halo_prev_rows(2 * KV_WIDTH), _cols(2 * KV_WIDTH),
                  _halo_prev_cols(2 * KV_WIDTH), _cols(D_MODEL), _rows(D_MODEL), _vmem(),
                  pl.BlockSpec(memory_space=pltpu.SMEM)],
        out_specs=[_rows(D_MODEL), _cols(D_MODEL), _cols(N_HEADS)],
        out_shape=[
            jax.ShapeDtypeStruct((T, D_MODEL), F32),
            jax.ShapeDtypeStruct((D_MODEL, T), BF16),
            jax.ShapeDtypeStruct((N_HEADS, T), F32),
        ],
        scratch_shapes=[pltpu.VMEM((D_MODEL, ROW_TILE), F32)],
        compiler_params=_params("parallel"),
    )(q_t, kv, kv, kv_t, kv_t, z_t, x, w_out, sinks)


def _inv_count(first_row, rows, window):
    t = first_row + lax.broadcasted_iota(jnp.int32, (rows, 1), 0)
    return 1.0 / jnp.minimum(t + 1, window).astype(F32)


def pool_fwd(x, g, w_in, w_mix, scale, w_out):
    T = x.shape[0]

    def body(x_ref, g_ref, win_ref, wmix_ref, scale_ref, wout_ref, xo_ref, p_ref, z_ref, carry):
        i = pl.program_id(0)

        @pl.when(i == 0)
        def _():
            carry[...] = jnp.zeros_like(carry)

        _, _, h = _norm(x_ref[...], g_ref[...])
        h = h.astype(BF16)
        u = jnp.concatenate([_dot(h, win_ref[0]), _dot(h, win_ref[1])], axis=1)
        z = jnp.concatenate([_dot(h, win_ref[2]), _dot(h, win_ref[3])], axis=1)
        ext = jnp.concatenate([carry[...], u], axis=0)
        carry[...] = u[ROW_TILE - POOL_HALO:]
        mixed = []
        for gi, window in enumerate(POOL_WINDOWS):
            cols = slice(gi * POOL_GC, (gi + 1) * POOL_GC)
            s = ext[:, cols]
            shift = 1
            while shift < window:
                s = s + pltpu.roll(s, shift, 0)
                shift *= 2
            p = s[POOL_HALO:] * _inv_count(i * ROW_TILE, ROW_TILE, window) - u[:, cols]
            p = p.astype(BF16)
            p_ref[:, cols] = p
            mixed.append(_dot(p, wmix_ref[gi]))
        m = jnp.concatenate(mixed, axis=1) * scale_ref[...]
        silu, _ = _silu_parts(z)
        y = _dot((m * silu).astype(BF16), wout_ref[...])
        xo_ref[...] = x_ref[...] + y
        z_ref[...] = z.astype(BF16)

    return pl.pallas_call(
        body,
        name="pool_fwd",
        grid=(T // ROW_TILE,),
        in_specs=[_rows(D_MODEL), _vmem(), _vmem(), _vmem(), _vmem(), _vmem()],
        out_specs=[_rows(D_MODEL), _rows(D_MODEL), _rows(D_MODEL)],
        out_shape=[
            jax.ShapeDtypeStruct((T, D_MODEL), F32),
            jax.ShapeDtypeStruct((T, D_MODEL), BF16),
            jax.ShapeDtypeStruct((T, D_MODEL), BF16),
        ],
        scratch_shapes=[pltpu.VMEM((POOL_HALO, D_MODEL), F32)],
        compiler_params=_params("arbitrary"),
    )(x, g, w_in, w_mix, scale, w_out)


def loss_head(x, g, target):
    T = x.shape[0]

    def body(x_ref, g_ref, t_ref, dx_ref, sq_ref, dg_ref):
        @pl.when(pl.program_id(0) == 0)
        def _():
            sq_ref[...] = jnp.zeros_like(sq_ref)
            dg_ref[...] = jnp.zeros_like(dg_ref)

        gv = g_ref[...]
        xhat, rstd, out = _norm(x_ref[...], gv)
        err = out - t_ref[...]
        sq_ref[...] += jnp.sum(err * err, axis=0, keepdims=True)
        dx, dg = _norm_bwd(err * (1.0 / D_MODEL), xhat, rstd, gv)
        dx_ref[...] = dx
        dg_ref[...] += dg

    return pl.pallas_call(
        body,
        name="loss_head",
        grid=(T // ROW_TILE,),
        in_specs=[_rows(D_MODEL), _vmem(), _rows(D_MODEL)],
        out_specs=[_rows(D_MODEL), _vmem(), _vmem()],
        out_shape=[
            jax.ShapeDtypeStruct((T, D_MODEL), F32),
            jax.ShapeDtypeStruct((1, D_MODEL), F32),
            jax.ShapeDtypeStruct((1, D_MODEL), F32),
        ],
        compiler_params=_params("arbitrary"),
    )(x, g, target)


def pool_bwd_mix(dy, p, z, w_mix, scale, w_out):
    T = dy.shape[0]

    def body(dy_ref, p_ref, z_ref, wmix_ref, scale_ref, wout_ref, dp_ref, dz_ref, dwout_ref, dwmix_ref, dscale_ref):
        @pl.when(pl.program_id(0) == 0)
        def _():
            dwout_ref[...] = jnp.zeros_like(dwout_ref)
            dwmix_ref[...] = jnp.zeros_like(dwmix_ref)
            dscale_ref[...] = jnp.zeros_like(dscale_ref)

        dyb = dy_ref[...].astype(BF16)
        da = _dot_nt(dyb, wout_ref[...])
        m_pre = jnp.concatenate(
            [_dot(p_ref[:, gi * POOL_GC:(gi + 1) * POOL_GC], wmix_ref[gi]) for gi in range(len(POOL_WINDOWS))], axis=1)
        sc = scale_ref[...]
        m = m_pre * sc
        zf = z_ref[...].astype(F32)
        silu, dsilu = _silu_parts(zf)
        dwout_ref[...] += _dot_tn((m * silu).astype(BF16), dyb)
        dm = da * silu
        dz_ref[...] = (da * m * dsilu).astype(BF16)
        dscale_ref[...] += jnp.sum(dm * m_pre, axis=0, keepdims=True)
        dmp = (dm * sc).astype(BF16)
        for gi in range(len(POOL_WINDOWS)):
            cols = slice(gi * POOL_GC, (gi + 1) * POOL_GC)
            dwmix_ref[gi] += _dot_tn(p_ref[:, cols], dmp[:, cols])
            dp_ref[:, cols] = _dot_nt(dmp[:, cols], wmix_ref[gi]).astype(BF16)

    return pl.pallas_call(
        body,
        name="pool_bwd_mix",
        grid=(T // ROW_TILE,),
        in_specs=[_rows(D_MODEL), _rows(D_MODEL), _rows(D_MODEL), _vmem(), _vmem(), _vmem()],
        out_specs=[_rows(D_MODEL), _rows(D_MODEL), _vmem(), _vmem(), _vmem()],
        out_shape=[
            jax.ShapeDtypeStruct((T, D_MODEL), BF16),
            jax.ShapeDtypeStruct((T, D_MODEL), BF16),
            jax.ShapeDtypeStruct((D_MODEL, D_MODEL), F32),
            jax.ShapeDtypeStruct((len(POOL_WINDOWS), POOL_GC, POOL_GC), F32),
            jax.ShapeDtypeStruct((1, D_MODEL), F32),
        ],
        compiler_params=_params("arbitrary"),
    )(dy, p, z, w_mix, scale, w_out)


def pool_bwd_in(dp, dz, x, dy, g, w_in):
    T = x.shape[0]
    halo_blocks = ROW_TILE // POOL_HALO
    last_halo = T // POOL_HALO - 1

    def body(dp_ref, dph_ref, dz_ref, x_ref, dy_ref, g_ref, win_ref, dx_ref, dwin_ref, dg_ref):
        i = pl.program_id(0)

        @pl.when(i == 0)
        def _():
            dwin_ref[...] = jnp.zeros_like(dwin_ref)
            dg_ref[...] = jnp.zeros_like(dg_ref)

        rows = ROW_TILE + POOL_HALO
        ext = jnp.concatenate([dp_ref[...], dph_ref[...]], axis=0).astype(F32)
        t = i * ROW_TILE + lax.broadcasted_iota(jnp.int32, (rows, 1), 0)
        inside = (t < T).astype(F32)
        du = []
        for gi, window in enumerate(POOL_WINDOWS):
            cols = slice(gi * POOL_GC, (gi + 1) * POOL_GC)
            s = ext[:, cols] * (_inv_count(i * ROW_TILE, rows, window) * inside)
            shift = 1
            while shift < window:
                s = s + pltpu.roll(s, rows - shift, 0)
                shift *= 2
            du.append(s[:ROW_TILE] - ext[:ROW_TILE, cols])
        du = jnp.concatenate(du, axis=1).astype(BF16)
        chunks = [du[:, :POOL_CHUNK], du[:, POOL_CHUNK:], dz_ref[:, :POOL_CHUNK], dz_ref[:, POOL_CHUNK:]]
        gv = g_ref[...]
        xhat, rstd, h = _norm(x_ref[...], gv)
        h = h.astype(BF16)
        dh = jnp.zeros((ROW_TILE, D_MODEL), F32)
        for c in range(N_CHIPS):
            dwin_ref[c] += _dot_tn(h, chunks[c])
            dh = dh + _dot_nt(chunks[c], win_ref[c])
        dx, dg = _norm_bwd(dh, xhat, rstd, gv)
        dx_ref[...] = dx + dy_ref[...]
        dg_ref[...] += dg

    return pl.pallas_call(
        body,
        name="pool_bwd_in",
        grid=(T // ROW_TILE,),
        in_specs=[_rows(D_MODEL),
                  pl.BlockSpec((POOL_HALO, D_MODEL), lambda i: (jnp.minimum((i + 1) * halo_blocks, last_halo), 0)),
                  _rows(D_MODEL), _rows(D_MODEL), _rows(D_MODEL), _vmem(), _vmem()],
        out_specs=[_rows(D_MODEL), _vmem(), _vmem()],
        out_shape=[
            jax.ShapeDtypeStruct((T, D_MODEL), F32),
            jax.ShapeDtypeStruct((N_CHIPS, D_MODEL, POOL_CHUNK), F32),
            jax.ShapeDtypeStruct((1, D_MODEL), F32),
        ],
        compiler_params=_params("arbitrary"),
    )(dp, dp, dz, x, dy, g, w_in)


def attn_bwd_core(dy, q_t, kv, kv_t, z_t, o_t, lse, w_out, sinks):
    T = dy.shape[0]
    tiles = T // ROW_TILE
    blocks = ROW_TILE // ATTN_BLOCK

    def body(dy_ref, qt_ref, kv_ref, kvh_ref, kvt_ref, kvth_ref, zt_ref, ot_ref, lse_ref, w_ref, sink_ref,
             dqt_ref, dkv_ref, dkvh_ref, dzt_ref, dwout_ref, dsink_ref, do_s, dkv_s):
        @pl.when(pl.program_id(0) == 0)
        def _():
            dwout_ref[...] = jnp.zeros_like(dwout_ref)
            dsink_ref[...] = jnp.zeros_like(dsink_ref)

        dyb = dy_ref[...].astype(BF16)
        da = _dot_nt(w_ref[...], dyb)
        of = ot_ref[...].astype(F32)
        silu, dsilu = _silu_parts(zt_ref[...].astype(F32))
        dwout_ref[...] += _dot((of * silu).astype(BF16), dyb)
        do = da * silu
        dzt_ref[...] = (da * of * dsilu).astype(BF16)
        do_s[...] = do.astype(BF16)
        dof = do * of
        dkv_s[...] = jnp.zeros_like(dkv_s)
        tri = _causal_triangle()
        first_penalty = jnp.where(pl.program_id(0) == 0, NEG_BIG, 0.0)
        for hk in range(N_KV_HEADS):
            heads = [hk * GROUP + gi for gi in range(GROUP)]
            sink = _group_row([jnp.full((1, ATTN_BLOCK), sink_ref[h], F32) for h in heads])
            deltas = [jnp.sum(dof[h * HEAD_DIM:(h + 1) * HEAD_DIM], axis=0, keepdims=True) for h in heads]
            k_lanes = slice(hk * HEAD_DIM, (hk + 1) * HEAD_DIM)
            v_lanes = slice(KV_WIDTH + hk * HEAD_DIM, KV_WIDTH + (hk + 1) * HEAD_DIM)
            for b in range(blocks):
                cols = slice(b * ATTN_BLOCK, (b + 1) * ATTN_BLOCK)
                cur_rows = slice((b + 1) * ATTN_BLOCK, (b + 2) * ATTN_BLOCK)
                prev_rows = slice(b * ATTN_BLOCK, (b + 1) * ATTN_BLOCK)
                qt = _group_cols(qt_ref, hk, cols)
                dot = _group_cols(do_s, hk, cols)
                lse_row = _group_row([lse_ref[h:h + 1, cols] for h in heads])
                delta = _group_row([d[:, cols] for d in deltas])
                kc, kp = _block_pair(kv_ref, kvh_ref, b, k_lanes)
                vc, vp = _block_pair(kv_ref, kvh_ref, b, v_lanes)
                ktc, ktp = _block_pair_t(kvt_ref, kvth_ref, b, k_lanes)
                p = jnp.exp(_window_scores(kc, kp, qt, tri, b, first_penalty) - lse_row)
                dp = jnp.where(tri, _dot(vc, dot), _dot(vp, dot))
                ds = (p * (dp - delta)).astype(BF16)
                pb = p.astype(BF16)
                zero = jnp.zeros_like(pb)
                ds_c, ds_p = jnp.where(tri, ds, zero), jnp.where(tri, zero, ds)
                dq = (_dot(ktc, ds_c) + _dot(ktp, ds_p)) * (1.0 / math.sqrt(HEAD_DIM))
                dkv_s[cur_rows, k_lanes] += _dot_nt(ds_c, qt)
                dkv_s[prev_rows, k_lanes] += _dot_nt(ds_p, qt)
                dkv_s[cur_rows, v_lanes] += _dot_nt(jnp.where(tri, pb, zero), dot)
                dkv_s[prev_rows, v_lanes] += _dot_nt(jnp.where(tri, zero, pb), dot)
                dsink = -jnp.exp(sink - lse_row) * delta
                for gi, h in enumerate(heads):
                    part = slice(gi * ATTN_BLOCK, (gi + 1) * ATTN_BLOCK)
                    dqt_ref[h * HEAD_DIM:(h + 1) * HEAD_DIM, cols] = dq[:, part].astype(BF16)
                    dsink_ref[0:1, h:h + 1] += jnp.sum(dsink[:, part], axis=1, keepdims=True)
        dkv_ref[...] = dkv_s[ATTN_BLOCK:]
        dkvh_ref[0] = dkv_s[:ATTN_BLOCK]

    return pl.pallas_call(
        body,
        name="attn_bwd_core",
        grid=(tiles,),
        in_specs=[_rows(D_MODEL), _cols(D_MODEL), _rows(2 * KV_WIDTH), _halo_prev_rows(2 * KV_WIDTH),
                  _cols(2 * KV_WIDTH), _halo_prev_cols(2 * KV_WIDTH), _cols(D_MODEL), _cols(D_MODEL), _cols(N_HEADS),
                  _vmem(), pl.BlockSpec(memory_space=pltpu.SMEM)],
        out_specs=[_cols(D_MODEL), _rows(2 * KV_WIDTH), pl.BlockSpec((1, ATTN_BLOCK, 2 * KV_WIDTH), lambda i: (i, 0, 0)),
                   _cols(D_MODEL), _vmem(), _vmem()],
        out_shape=[
            jax.ShapeDtypeStruct((D_MODEL, T), BF16),
            jax.ShapeDtypeStruct((T, 2 * KV_WIDTH), F32),
            jax.ShapeDtypeStruct((tiles, ATTN_BLOCK, 2 * KV_WIDTH), F32),
            jax.ShapeDtypeStruct((D_MODEL, T), BF16),
            jax.ShapeDtypeStruct((D_MODEL, D_MODEL), F32),
            jax.ShapeDtypeStruct((8, 128), F32),
        ],
        scratch_shapes=[
            pltpu.VMEM((D_MODEL, ROW_TILE), BF16),
            pltpu.VMEM((ROW_TILE + ATTN_BLOCK, 2 * KV_WIDTH), F32),
        ],
        compiler_params=_params("arbitrary"),
    )(dy, q_t, kv, kv, kv_t, kv_t, z_t, o_t, lse, w_out, sinks)


def attn_bwd_in(dq_t, dkv, dkv_halo, dz_t, x, dy, g, wq_t, wz_t, wkv):
    T = x.shape[0]
    tiles = T // ROW_TILE

    def body(dqt_ref, dkv_ref, dkvh_ref, dzt_ref, x_ref, dy_ref, g_ref, wq_ref, wz_ref, wkv_ref,
             dx_ref, dwq_ref, dwz_ref, dwkv_ref, dg_ref):
        i = pl.program_id(0)

        @pl.when(i == 0)
        def _():
            dwq_ref[...] = jnp.zeros_like(dwq_ref)
            dwz_ref[...] = jnp.zeros_like(dwz_ref)
            dwkv_ref[...] = jnp.zeros_like(dwkv_ref)
            dg_ref[...] = jnp.zeros_like(dg_ref)

        more = (i < tiles - 1).astype(F32)
        tail = jnp.concatenate([jnp.zeros((ROW_TILE - ATTN_BLOCK, 2 * KV_WIDTH), F32), dkvh_ref[0] * more], axis=0)
        dkvb = (dkv_ref[...] + tail).astype(BF16)
        gv = g_ref[...]
        xhat, rstd, h = _norm(x_ref[...], gv)
        h = h.astype(BF16)
        dqt = dqt_ref[...]
        dzt = dzt_ref[...]
        dwq_ref[...] += _dot(dqt, h)
        dwz_ref[...] += _dot(dzt, h)
        dwkv_ref[...] += _dot_tn(h, dkvb)
        dh = _dot_tn(dqt, wq_ref[...]) + _dot_tn(dzt, wz_ref[...]) + _dot_nt(dkvb, wkv_ref[...])
        dx, dg = _norm_bwd(dh, xhat, rstd, gv)
        dx_ref[...] = dx + dy_ref[...]
        dg_ref[...] += dg

    halo_next = pl.BlockSpec((1, ATTN_BLOCK, 2 * KV_WIDTH), lambda i: (jnp.minimum(i + 1, tiles - 1), 0, 0))
    return pl.pallas_call(
        body,
        name="attn_bwd_in",
        grid=(tiles,),
        in_specs=[_cols(D_MODEL), _rows(2 * KV_WIDTH), halo_next, _cols(D_MODEL), _rows(D_MODEL), _rows(D_MODEL),
                  _vmem(), _vmem(), _vmem(), _vmem()],
        out_specs=[_rows(D_MODEL), _vmem(), _vmem(), _vmem(), _vmem()],
        out_shape=[
            jax.ShapeDtypeStruct((T, D_MODEL), F32),
            jax.ShapeDtypeStruct((D_MODEL, D_MODEL), F32),
            jax.ShapeDtypeStruct((D_MODEL, D_MODEL), F32),
            jax.ShapeDtypeStruct((D_MODEL, 2 * KV_WIDTH), F32),
            jax.ShapeDtypeStruct((1, D_MODEL), F32),
        ],
        compiler_params=_params("arbitrary"),
    )(dq_t, dkv, dkv_halo, dz_t, x, dy, g, wq_t, wz_t, wkv)


def local_step(x, target, norm_g, sinks, final_g, attn_in_w, attn_out_w, pool_in_w, pool_mix_w, pool_scale, pool_out_w):
    saved = []
    for layer in range(4):
        j = layer // 2
        g = norm_g[layer][None, :]
        if layer % 2 == 0:
            wq_t, wkv_t, wz_t, wkv = attn_in_w[j]
            q_t, kv, kv_t, z_t = attn_in_fwd(x, g, wq_t, wkv_t, wz_t, wkv)
            x_new, o_t, lse = attn_core_fwd(q_t, kv, kv_t, z_t, x, attn_out_w[j], sinks[j])
            saved.append((x, g, q_t, kv, kv_t, z_t, o_t, lse))
        else:
            x_new, p, z = pool_fwd(x, g, pool_in_w[j], pool_mix_w[j], pool_scale[j], pool_out_w[j])
            saved.append((x, g, p, z))
        x = x_new
    dx, sq, d_final_g = loss_head(x, final_g[None, :], target)
    grads = {"norm_g": [None] * 4, "attn_w_in": [None] * 2, "attn_sinks": [None] * 2, "attn_w_out": [None] * 2,
             "pool_w_in": [None] * 2, "pool_w_mix": [None] * 2, "pool_scale": [None] * 2, "pool_w_out": [None] * 2,
             "final_g": d_final_g}
    for layer in reversed(range(4)):
        j = layer // 2
        if layer % 2 == 0:
            x_in, g, q_t, kv, kv_t, z_t, o_t, lse = saved[layer]
            wq_t, _, wz_t, wkv = attn_in_w[j]
            dq_t, dkv, dkv_halo, dz_t, dwout, dsink = attn_bwd_core(
                dx, q_t, kv, kv_t, z_t, o_t, lse, attn_out_w[j], sinks[j])
            dx, dwq_t, dwz_t, dwkv, dg = attn_bwd_in(dq_t, dkv, dkv_halo, dz_t, x_in, dx, g, wq_t, wz_t, wkv)
            grads["attn_w_in"][j] = (dwq_t, dwkv, dwz_t)
            grads["attn_w_out"][j], grads["attn_sinks"][j] = dwout, dsink[0, :N_HEADS]
        else:
            x_in, g, p, z = saved[layer]
            dp, dz, dwout, dwmix, dscale = pool_bwd_mix(dx, p, z, pool_mix_w[j], pool_scale[j], pool_out_w[j])
            dx, dwin, dg = pool_bwd_in(dp, dz, x_in, dx, g, pool_in_w[j])
            grads["pool_w_in"][j], grads["pool_w_out"][j] = dwin, dwout
            grads["pool_w_mix"][j], grads["pool_scale"][j] = dwmix, dscale
        grads["norm_g"][layer] = dg
    return sq, dx, grads


SHARD_PIECES = (("attn_w_in", 640), ("attn_w_out", 256), ("pool_w_in", 512), ("pool_w_mix", 64), ("pool_w_out", 256))
SHARD_ROWS = sum(rows for _, rows in SHARD_PIECES)
GATHER_ROWS = SHARD_ROWS + 16
SMALL_ROWS = 8


def _piece_offsets():
    out, at = {}, 0
    for name, rows in SHARD_PIECES:
        out[name] = (at, at + rows)
        at += rows
    return out


PIECE_AT = _piece_offsets()


def _pack_shard(attn_w_in, attn_w_out, pool_w_in, pool_w_mix, pool_w_out):
    given = {"attn_w_in": attn_w_in, "attn_w_out": attn_w_out, "pool_w_in": pool_w_in, "pool_w_mix": pool_w_mix,
             "pool_w_out": pool_w_out}
    return jnp.concatenate([given[name].reshape(2, rows, D_MODEL) for name, rows in SHARD_PIECES], axis=1)


def _unpack_shard(flat):
    shapes = {"attn_w_in": (2, D_MODEL, ATTN_CHUNK), "attn_w_out": (2, 256, D_MODEL), "pool_w_in": (2, D_MODEL, POOL_CHUNK),
              "pool_w_mix": (2, 4, 64, POOL_GC), "pool_w_out": (2, 256, D_MODEL)}
    return {name: flat[:, PIECE_AT[name][0]:PIECE_AT[name][1]].reshape(shapes[name]) for name, _ in SHARD_PIECES}


def _mesh_position():
    return lax.axis_index("x"), lax.axis_index("y"), lax.axis_index("c")


def _other_chips(x, y):
    return [(1 - x, y), (x, 1 - y), (1 - x, 1 - y)]


def _any():
    return pl.BlockSpec(memory_space=pl.ANY)


def gather_weights(shard):
    def body(w_ref, out_ref, send_sems, recv_sems, local_sem):
        x, y, c = _mesh_position()
        me = 2 * x + y
        sibling = (x, y, 1 - c)
        chips = _other_chips(x, y)

        def copy(sem, chip_index, half, to, src=None):
            dst = out_ref.at[chip_index, half]
            return pltpu.make_async_remote_copy(
                src_ref=dst if src is None else src, dst_ref=dst, send_sem=send_sems.at[sem], recv_sem=recv_sems.at[sem],
                device_id=to, device_id_type=MESH_ID)

        mine = pltpu.make_async_copy(w_ref, out_ref.at[me], local_sem)
        mine.start()
        first = [copy(j, me, c, (cx, cy, c), src=w_ref.at[c]) for j, (cx, cy) in enumerate(chips)]
        for cp in first:
            cp.start()
        passed = [copy(3 + j, 2 * cx + cy, c, sibling) for j, (cx, cy) in enumerate(chips)]
        for j, (cx, cy) in enumerate(chips):
            copy(j, 2 * cx + cy, c, (x, y, c)).wait_recv()
            passed[j].start()
        for j, (cx, cy) in enumerate(chips):
            copy(3 + j, 2 * cx + cy, 1 - c, (x, y, c)).wait_recv()
        for cp in first + passed:
            cp.wait_send()
        mine.wait()

    return pl.pallas_call(
        body,
        name="gather_weights",
        in_specs=[_any()],
        out_specs=_any(),
        out_shape=jax.ShapeDtypeStruct((N_CHIPS,) + shard.shape, shard.dtype),
        scratch_shapes=[pltpu.SemaphoreType.DMA((6,)), pltpu.SemaphoreType.DMA((6,)), pltpu.SemaphoreType.DMA],
    )(shard)


def sibling_swap_halves(grads):
    n = grads.shape[0]

    def body(g_ref, out_ref, send_sems, recv_sems):
        x, y, c = _mesh_position()
        copies = [
            pltpu.make_async_remote_copy(
                src_ref=g_ref.at[k, 1 - c], dst_ref=out_ref.at[k], send_sem=send_sems.at[k], recv_sem=recv_sems.at[k],
                device_id=(x, y, 1 - c), device_id_type=MESH_ID)
            for k in range(n)
        ]
        for cp in copies:
            cp.start()
        for cp in copies:
            cp.wait()

    return pl.pallas_call(
        body,
        name="sibling_swap_halves",
        in_specs=[_any()],
        out_specs=_any(),
        out_shape=jax.ShapeDtypeStruct((n,) + grads.shape[2:], grads.dtype),
        scratch_shapes=[pltpu.SemaphoreType.DMA((n,)), pltpu.SemaphoreType.DMA((n,))],
    )(grads)


SUM_TILE = 432


def chip_sum(grads, from_sibling, core):
    n, _, rows, width = grads.shape

    def body(core_ref, a_ref, b_ref, o_ref):
        o_ref[0] = (a_ref[0, 0] + b_ref[0]).astype(BF16)

    return pl.pallas_call(
        body,
        name="chip_sum",
        grid_spec=pltpu.PrefetchScalarGridSpec(
            num_scalar_prefetch=1,
            grid=(n, rows // SUM_TILE),
            in_specs=[pl.BlockSpec((1, 1, SUM_TILE, width), lambda k, r, core_ref: (k, core_ref[0], r, 0)),
                      pl.BlockSpec((1, SUM_TILE, width), lambda k, r, core_ref: (k, r, 0))],
            out_specs=pl.BlockSpec((1, SUM_TILE, width), lambda k, r, core_ref: (k, r, 0)),
        ),
        out_shape=jax.ShapeDtypeStruct((n, rows, width), BF16),
    )(core, grads, from_sibling)


def scatter_to_chips(parts):
    def body(p_ref, out_ref, send_sems, recv_sems, local_sem):
        x, y, c = _mesh_position()
        me = 2 * x + y
        chips = _other_chips(x, y)
        mine = pltpu.make_async_copy(p_ref.at[me], out_ref.at[me], local_sem)
        mine.start()
        sends = [
            pltpu.make_async_remote_copy(
                src_ref=p_ref.at[2 * cx + cy], dst_ref=out_ref.at[me], send_sem=send_sems.at[j], recv_sem=recv_sems.at[j],
                device_id=(cx, cy, c), device_id_type=MESH_ID)
            for j, (cx, cy) in enumerate(chips)
        ]
        for cp in sends:
            cp.start()
        for j, (cx, cy) in enumerate(chips):
            pltpu.make_async_remote_copy(
                src_ref=p_ref.at[me], dst_ref=out_ref.at[2 * cx + cy], send_sem=send_sems.at[j], recv_sem=recv_sems.at[j],
                device_id=(cx, cy, c), device_id_type=MESH_ID).wait_recv()
        for cp in sends:
            cp.wait_send()
        mine.wait()

    return pl.pallas_call(
        body,
        name="scatter_to_chips",
        in_specs=[_any()],
        out_specs=_any(),
        out_shape=jax.ShapeDtypeStruct(parts.shape, parts.dtype),
        scratch_shapes=[pltpu.SemaphoreType.DMA((3,)), pltpu.SemaphoreType.DMA((3,)), pltpu.SemaphoreType.DMA],
    )(parts)


def sum_chips(parts):
    n, rows, width = parts.shape

    def body(p_ref, o_ref):
        total = p_ref[0].astype(F32)
        for k in range(1, n):
            total = total + p_ref[k].astype(F32)
        o_ref[...] = total

    return pl.pallas_call(
        body,
        name="sum_chips",
        grid=(rows // SUM_TILE,),
        in_specs=[pl.BlockSpec((n, SUM_TILE, width), lambda r: (0, r, 0))],
        out_specs=pl.BlockSpec((SUM_TILE, width), lambda r: (r, 0)),
        out_shape=jax.ShapeDtypeStruct((rows, width), F32),
    )(parts)


def sibling_join_halves(half):
    def body(h_ref, out_ref, send_sem, recv_sem, local_sem):
        x, y, c = _mesh_position()
        mine = pltpu.make_async_copy(h_ref, out_ref.at[c], local_sem)
        mine.start()
        cp = pltpu.make_async_remote_copy(
            src_ref=h_ref, dst_ref=out_ref.at[c], send_sem=send_sem, recv_sem=recv_sem,
            device_id=(x, y, 1 - c), device_id_type=MESH_ID)
        cp.start()
        pltpu.make_async_remote_copy(
            src_ref=h_ref, dst_ref=out_ref.at[1 - c], send_sem=send_sem, recv_sem=recv_sem,
            device_id=(x, y, 1 - c), device_id_type=MESH_ID).wait_recv()
        cp.wait_send()
        mine.wait()

    return pl.pallas_call(
        body,
        name="sibling_join_halves",
        in_specs=[_any()],
        out_specs=_any(),
        out_shape=jax.ShapeDtypeStruct((2,) + half.shape, half.dtype),
        scratch_shapes=[pltpu.SemaphoreType.DMA, pltpu.SemaphoreType.DMA, pltpu.SemaphoreType.DMA],
    )(half)


def all_reduce_small(v):
    n_dev = 8

    def body(v_ref, out_ref, slots, send_sems, recv_sems):
        x, y, c = _mesh_position()
        me = 4 * x + 2 * y + c
        slots[me] = v_ref[...]
        peers = []
        for r in range(1, n_dev):
            fx, fy, fc = (r >> 2) & 1, (r >> 1) & 1, r & 1
            peers.append((1 - x if fx else x, 1 - y if fy else y, 1 - c if fc else c))
        sends = [
            pltpu.make_async_remote_copy(
                src_ref=v_ref, dst_ref=slots.at[me], send_sem=send_sems.at[r], recv_sem=recv_sems.at[r],
                device_id=peer, device_id_type=MESH_ID)
            for r, peer in enumerate(peers)
        ]
        for cp in sends:
            cp.start()
        for r, (px, py, pc) in enumerate(peers):
            pltpu.make_async_remote_copy(
                src_ref=v_ref, dst_ref=slots.at[4 * px + 2 * py + pc], send_sem=send_sems.at[r], recv_sem=recv_sems.at[r],
                device_id=(px, py, pc), device_id_type=MESH_ID).wait_recv()
        for cp in sends:
            cp.wait_send()
        total = slots[0]
        for d in range(1, n_dev):
            total = total + slots[d]
        out_ref[...] = total

    return pl.pallas_call(
        body,
        name="all_reduce_small",
        in_specs=[_vmem()],
        out_specs=_vmem(),
        out_shape=jax.ShapeDtypeStruct(v.shape, v.dtype),
        scratch_shapes=[pltpu.VMEM((n_dev,) + v.shape, v.dtype), pltpu.SemaphoreType.DMA((n_dev - 1,)),
                        pltpu.SemaphoreType.DMA((n_dev - 1,))],
    )(v)


def adamw(w, g, m, v, tile):
    rows, width = w.shape

    def body(w_ref, g_ref, m_ref, v_ref, d_ref, mo_ref, vo_ref):
        gv = g_ref[...]
        m2 = ADAM_B1 * m_ref[...] + (1.0 - ADAM_B1) * gv
        v2 = ADAM_B2 * v_ref[...] + (1.0 - ADAM_B2) * (gv * gv)
        m_hat = m2 / (1.0 - ADAM_B1 ** ADAM_STEP)
        v_hat = v2 / (1.0 - ADAM_B2 ** ADAM_STEP)
        d_ref[...] = -ADAM_LR * (m_hat / (jnp.sqrt(v_hat) + ADAM_EPS) + ADAM_WD * w_ref[...])
        mo_ref[...] = m2
        vo_ref[...] = v2

    spec = pl.BlockSpec((tile, width), lambda r: (r, 0))
    shape = jax.ShapeDtypeStruct((rows, width), F32)
    return pl.pallas_call(
        body, name="adamw", grid=(rows // tile,), in_specs=[spec] * 4, out_specs=[spec] * 3, out_shape=[shape] * 3,
    )(w, g, m, v)


def _chip_major(w, chunk):
    return w.reshape(N_CHIPS, D_MODEL, chunk)


def kernel(x, norm_g, attn_w_in, attn_sinks, attn_w_out, pool_w_in, pool_w_mix, pool_scale, pool_w_out, final_g, loss_target, m_norm_g, m_attn_w_in, m_attn_sinks, m_attn_w_out, m_pool_w_in, m_pool_w_mix, m_pool_scale, m_pool_w_out, m_final_g, v_norm_g, v_attn_w_in, v_attn_sinks, v_attn_w_out, v_pool_w_in, v_pool_w_mix, v_pool_scale, v_pool_w_out, v_final_g):
    cx, cy, cc = _mesh_position()
    chip = 2 * cx + cy

    w_flat = _pack_shard(attn_w_in, attn_w_out, pool_w_in, pool_w_mix, pool_w_out)
    scale_bits = lax.bitcast_convert_type(pool_scale, BF16).reshape(2, 1, 2 * 256)
    extra = jnp.pad(scale_bits, ((0, 0), (0, GATHER_ROWS - SHARD_ROWS - 1), (0, D_MODEL - 512)))
    gathered = gather_weights(jnp.concatenate([w_flat.astype(BF16), extra], axis=1))

    def piece(name):
        lo, hi = PIECE_AT[name]
        return gathered[:, :, lo:hi]

    attn_in_w = []
    for j in range(2):
        full = _chip_major(piece("attn_w_in")[:, j], ATTN_CHUNK).transpose(1, 0, 2).reshape(D_MODEL, N_CHIPS * ATTN_CHUNK)
        wkv = full[:, D_MODEL:D_MODEL + 2 * KV_WIDTH]
        attn_in_w.append((full[:, :D_MODEL].T, wkv.T, full[:, D_MODEL + 2 * KV_WIDTH:].T, wkv))
    attn_out_w = jnp.stack([piece("attn_w_out")[:, j].reshape(D_MODEL, D_MODEL) for j in range(2)])
    pool_in_w = jnp.stack([_chip_major(piece("pool_w_in")[:, j], POOL_CHUNK) for j in range(2)])
    pool_mix_w = jnp.stack([
        piece("pool_w_mix")[:, j].reshape(N_CHIPS, 4, 64, POOL_GC).transpose(1, 0, 2, 3).reshape(4, POOL_GC, POOL_GC)
        for j in range(2)])
    pool_out_w = jnp.stack([piece("pool_w_out")[:, j].reshape(D_MODEL, D_MODEL) for j in range(2)])
    scale_full = lax.bitcast_convert_type(gathered[:, :, SHARD_ROWS, :512].reshape(N_CHIPS, 2, 256, 2), F32)
    scale_full = scale_full.transpose(1, 0, 2).reshape(2, 1, D_MODEL)

    sq, grad_x, g = local_step(x[0], loss_target[0], norm_g, attn_sinks, final_g, attn_in_w, attn_out_w, pool_in_w,
                               pool_mix_w, scale_full, pool_out_w)

    last = jnp.concatenate([g["attn_sinks"][0], g["attn_sinks"][1], jnp.sum(sq).reshape(1),
                            jnp.zeros((D_MODEL - 2 * N_HEADS - 1,), F32)])[None, :]
    small = jnp.concatenate(g["norm_g"] + [g["final_g"]] + g["pool_scale"] + [last], axis=0)
    small = all_reduce_small(small)
    loss = 0.5 * small[7, 2 * N_HEADS] / D_MODEL
    g_norm = small[0:4]
    g_final = small[4]
    g_scale = lax.dynamic_slice(small[5:7], (0, chip * 256), (2, 256))
    g_sinks = small[7, :2 * N_HEADS].reshape(2, N_HEADS)

    per_layer = []
    for j in range(2):
        mix = g["pool_w_mix"][j].reshape(4, N_CHIPS, 64, POOL_GC).transpose(1, 0, 2, 3)
        dwq_t, dwkv, dwz_t = g["attn_w_in"][j]
        d_in = jnp.concatenate([dwq_t.T, dwkv, dwz_t.T], axis=1)
        d_in = d_in.reshape(D_MODEL, N_CHIPS, ATTN_CHUNK).transpose(1, 0, 2)
        per_layer.append(jnp.concatenate([
            d_in.reshape(N_CHIPS, 640, D_MODEL), g["attn_w_out"][j].reshape(N_CHIPS, 256, D_MODEL),
            g["pool_w_in"][j].reshape(N_CHIPS, 512, D_MODEL), mix.reshape(N_CHIPS, 64, D_MODEL),
            g["pool_w_out"][j].reshape(N_CHIPS, 256, D_MODEL)], axis=1))
    g_all = jnp.stack(per_layer, axis=1)
    from_sibling = sibling_swap_halves(g_all)
    parts = chip_sum(g_all, from_sibling, cc.astype(jnp.int32).reshape(1))
    landed = scatter_to_chips(parts)
    g_shard = sibling_join_halves(sum_chips(landed))

    flat_rows = 2 * SHARD_ROWS
    big = [t.reshape(flat_rows, D_MODEL) for t in (
        w_flat, g_shard, _pack_shard(m_attn_w_in, m_attn_w_out, m_pool_w_in, m_pool_w_mix, m_pool_w_out),
        _pack_shard(v_attn_w_in, v_attn_w_out, v_pool_w_in, v_pool_w_mix, v_pool_w_out))]
    d_big, m_big, v_big = (_unpack_shard(t.reshape(2, SHARD_ROWS, D_MODEL)) for t in adamw(*big, tile=SUM_TILE))
    g_big = _unpack_shard(g_shard)

    def small_pack(ng, fg, sc, sk):
        row7 = jnp.concatenate([sk.reshape(2 * N_HEADS), jnp.zeros((D_MODEL - 2 * N_HEADS,), F32)])[None, :]
        sc = jnp.concatenate([sc, jnp.zeros((2, D_MODEL - 256), F32)], axis=1)
        return jnp.concatenate([ng, fg[None, :], sc, row7], axis=0)

    packs = [small_pack(norm_g, final_g, pool_scale, attn_sinks), small_pack(g_norm, g_final, g_scale, g_sinks),
             small_pack(m_norm_g, m_final_g, m_pool_scale, m_attn_sinks), small_pack(v_norm_g, v_final_g, v_pool_scale, v_attn_sinks)]
    small_out = adamw(*packs, tile=SMALL_ROWS)

    def small_unpack(t):
        return {"norm_g": t[0:4], "final_g": t[4], "pool_scale": t[5:7, :256], "attn_sinks": t[7, :2 * N_HEADS].reshape(2, N_HEADS)}

    d_small, m_small, v_small = (small_unpack(t) for t in small_out)
    g_small = {"norm_g": g_norm, "final_g": g_final, "pool_scale": g_scale, "attn_sinks": g_sinks}

    names = ["norm_g", "attn_w_in", "attn_sinks", "attn_w_out", "pool_w_in", "pool_w_mix", "pool_scale", "pool_w_out", "final_g"]

    def ordered(bigs, smalls):
        return [bigs[n] if n in bigs else smalls[n] for n in names]

    return (loss, grad_x[None], *ordered(g_big, g_small), *ordered(d_big, d_small), *ordered(m_big, m_small),
            *ordered(v_big, v_small))
```

```python
import functools
import math

import jax
import jax.numpy as jnp
from jax import lax
from jax.experimental import pallas as pl
from jax.experimental.pallas import tpu as pltpu

F32 = jnp.float32
BF16 = jnp.bfloat16

D_MODEL = 1024
N_HEADS = 16
N_KV_HEADS = 4
GROUP = N_HEADS // N_KV_HEADS
HEAD_DIM = 64
KV_WIDTH = N_KV_HEADS * HEAD_DIM
ATTN_BLOCK = 128
POOL_WINDOWS = (2, 4, 8, 16)
POOL_GC = 256
POOL_HALO = 16
EPS = 1e-6
N_CHIPS = 4
ATTN_CHUNK = 640
POOL_CHUNK = 512
ROW_TILE = 512
NEG_BIG = -1e30
VMEM_LIMIT_BYTES = 60 * 1024 * 1024

ADAM_LR = 0.001
ADAM_B1 = 0.9
ADAM_B2 = 0.999
ADAM_EPS = 1e-08
ADAM_WD = 0.01
ADAM_STEP = 10

MESH_ID = pl.DeviceIdType.MESH


def _dot(a, b):
    return jnp.dot(a, b, preferred_element_type=F32)


def _dot_nt(a, b):
    return lax.dot_general(a, b, (((1,), (1,)), ((), ())), preferred_element_type=F32)


def _dot_tn(a, b):
    return lax.dot_general(a, b, (((0,), (0,)), ((), ())), preferred_element_type=F32)


def _vmem():
    return pl.BlockSpec(memory_space=pltpu.VMEM)


def _rows(width, tile=ROW_TILE):
    return pl.BlockSpec((tile, width), lambda i: (i, 0))


def _params(semantics):
    return pltpu.CompilerParams(dimension_semantics=(semantics,), vmem_limit_bytes=VMEM_LIMIT_BYTES)


def _norm(xf, g):
    rstd = lax.rsqrt(jnp.mean(xf * xf, axis=-1, keepdims=True) + EPS)
    xhat = xf * rstd
    return xhat, rstd, xhat * g


def _norm_bwd(dh, xhat, rstd, g):
    dg = jnp.sum(dh * xhat, axis=0, keepdims=True)
    dxhat = dh * g
    dx = rstd * (dxhat - xhat * jnp.mean(dxhat * xhat, axis=-1, keepdims=True))
    return dx, dg


def _silu_parts(zf):
    sig = jax.nn.sigmoid(zf)
    return zf * sig, sig * (1.0 + zf * (1.0 - sig))


def _cols(height, tile=ROW_TILE):
    return pl.BlockSpec((height, tile), lambda i: (0, i))


def _halo_prev_rows(width):
    per_tile = ROW_TILE // ATTN_BLOCK
    return pl.BlockSpec((ATTN_BLOCK, width), lambda i: (jnp.maximum(i * per_tile - 1, 0), 0))


def _halo_prev_cols(height):
    per_tile = ROW_TILE // ATTN_BLOCK
    return pl.BlockSpec((height, ATTN_BLOCK), lambda i: (0, jnp.maximum(i * per_tile - 1, 0)))


def attn_in_fwd(x, g, wq_t, wkv_t, wz_t, wkv):
    T = x.shape[0]

    def body(x_ref, g_ref, wq_ref, wkvt_ref, wz_ref, wkv_ref, qt_ref, kv_ref, kvt_ref, zt_ref):
        _, _, h = _norm(x_ref[...], g_ref[...])
        h = h.astype(BF16)
        scale = 1.0 / math.sqrt(HEAD_DIM)
        qt_ref[...] = (_dot_nt(wq_ref[...], h) * scale).astype(BF16)
        zt_ref[...] = _dot_nt(wz_ref[...], h).astype(BF16)
        kvt_ref[...] = _dot_nt(wkvt_ref[...], h).astype(BF16)
        kv_ref[...] = _dot(h, wkv_ref[...]).astype(BF16)

    return pl.pallas_call(
        body,
        name="attn_in_fwd",
        grid=(T // ROW_TILE,),
        in_specs=[_rows(D_MODEL), _vmem(), _vmem(), _vmem(), _vmem(), _vmem()],
        out_specs=[_cols(D_MODEL), _rows(2 * KV_WIDTH), _cols(2 * KV_WIDTH), _cols(D_MODEL)],
        out_shape=[
            jax.ShapeDtypeStruct((D_MODEL, T), BF16),
            jax.ShapeDtypeStruct((T, 2 * KV_WIDTH), BF16),
            jax.ShapeDtypeStruct((2 * KV_WIDTH, T), BF16),
            jax.ShapeDtypeStruct((D_MODEL, T), BF16),
        ],
        compiler_params=_params("parallel"),
    )(x, g, wq_t, wkv_t, wz_t, wkv)


def _causal_triangle():
    shape = (ATTN_BLOCK, GROUP * ATTN_BLOCK)
    kj = lax.broadcasted_iota(jnp.int32, shape, 0)
    qi = lax.broadcasted_iota(jnp.int32, shape, 1) & (ATTN_BLOCK - 1)
    return kj <= qi


def _group_cols(ref, hk, cols):
    return jnp.concatenate(
        [ref[(hk * GROUP + gi) * HEAD_DIM:(hk * GROUP + gi + 1) * HEAD_DIM, cols] for gi in range(GROUP)], axis=1)


def _group_row(values):
    return jnp.concatenate(values, axis=1)


def _block_pair(ref, halo_ref, b, lanes):
    cur = ref[b * ATTN_BLOCK:(b + 1) * ATTN_BLOCK, lanes]
    prev = halo_ref[:, lanes] if b == 0 else ref[(b - 1) * ATTN_BLOCK:b * ATTN_BLOCK, lanes]
    return cur, prev


def _block_pair_t(ref, halo_ref, b, rows):
    cur = ref[rows, b * ATTN_BLOCK:(b + 1) * ATTN_BLOCK]
    prev = halo_ref[rows, :] if b == 0 else ref[rows, (b - 1) * ATTN_BLOCK:b * ATTN_BLOCK]
    return cur, prev


def _window_scores(kc, kp, qt, tri, b, first_penalty):
    sp = _dot(kp, qt)
    if b == 0:
        sp = sp + first_penalty
    return jnp.where(tri, _dot(kc, qt), sp)


def attn_core_fwd(q_t, kv, kv_t, z_t, x, w_out, sinks):
    T = x.shape[0]
    blocks = ROW_TILE // ATTN_BLOCK

    def body(qt_ref, kv_ref, kvh_ref, kvt_ref, kvth_ref, zt_ref, x_ref, w_ref, sink_ref, xo_ref, ot_ref, lse_ref, oacc):
        tri = _causal_triangle()
        first_penalty = jnp.where(pl.program_id(0) == 0, NEG_BIG, 0.0)
        for hk in range(N_KV_HEADS):
            heads = [hk * GROUP + gi for gi in range(GROUP)]
            sink = _group_row([jnp.full((1, ATTN_BLOCK), sink_ref[h], F32) for h in heads])
            k_lanes = slice(hk * HEAD_DIM, (hk + 1) * HEAD_DIM)
            v_rows = slice(KV_WIDTH + hk * HEAD_DIM, KV_WIDTH + (hk + 1) * HEAD_DIM)
            for b in range(blocks):
                cols = slice(b * ATTN_BLOCK, (b + 1) * ATTN_BLOCK)
                qt = _group_cols(qt_ref, hk, cols)
                kc, kp = _block_pair(kv_ref, kvh_ref, b, k_lanes)
                s = _window_scores(kc, kp, qt, tri, b, first_penalty)
                m = jnp.maximum(jnp.max(s, axis=0, keepdims=True), sink)
                p = jnp.exp(s - m)
                denom = jnp.sum(p, axis=0, keepdims=True) + jnp.exp(sink - m)
                pb = p.astype(BF16)
                zero = jnp.zeros_like(pb)
                vtc, vtp = _block_pair_t(kvt_ref, kvth_ref, b, v_rows)
                o = (_dot(vtc, jnp.where(tri, pb, zero)) + _dot(vtp, jnp.where(tri, zero, pb))) * (1.0 / denom)
                lse = m + jnp.log(denom)
                for gi, h in enumerate(heads):
                    part = slice(gi * ATTN_BLOCK, (gi + 1) * ATTN_BLOCK)
                    oacc[h * HEAD_DIM:(h + 1) * HEAD_DIM, cols] = o[:, part]
                    lse_ref[h:h + 1, cols] = lse[:, part]
        of = oacc[...]
        silu, _ = _silu_parts(zt_ref[...].astype(F32))
        y = _dot_tn((of * silu).astype(BF16), w_ref[...])
        xo_ref[...] = x_ref[...] + y
        ot_ref[...] = of.astype(BF16)

    return pl.pallas_call(
        body,
        name="attn_core_fwd",
        grid=(T // ROW_TILE,),
        in_specs=[_cols(D_MODEL), _rows(2 * KV_WIDTH), _halo_prev_rows(2 * KV_WIDTH), _cols(2 * KV_WIDTH),
                  _halo_prev_cols(2 * KV_WIDTH), _cols(D_MODEL), _rows(D_MODEL), _vmem(),
                  pl.BlockSpec(memory_space=pltpu.SMEM)],
        out_specs=[_rows(D_MODEL), _cols(D_MODEL), _cols(N_HEADS)],
        out_shape=[
            jax.ShapeDtypeStruct((T, D_MODEL), F32),
            jax.ShapeDtypeStruct((D_MODEL, T), BF16),
            jax.ShapeDtypeStruct((N_HEADS, T), F32),
        ],
        scratch_shapes=[pltpu.VMEM((D_MODEL, ROW_TILE), F32)],
        compiler_params=_params("parallel"),
    )(q_t, kv, kv, kv_t, kv_t, z_t, x, w_out, sinks)


def _inv_count(first_row, rows, window):
    t = first_row + lax.broadcasted_iota(jnp.int32, (rows, 1), 0)
    return 1.0 / jnp.minimum(t + 1, window).astype(F32)


MIX_ROWS = POOL_GC // N_CHIPS


def _mix_groups(wmix_ref):
    return [jnp.concatenate([wmix_ref[k, gi * MIX_ROWS:(gi + 1) * MIX_ROWS, :] for k in range(N_CHIPS)], axis=0)
            for gi in range(len(POOL_WINDOWS))]


def pool_fwd(x, g, w_in, w_mix, scale, w_out):
    T = x.shape[0]

    def body(x_ref, g_ref, win_ref, wmix_ref, scale_ref, wout_ref, xo_ref, p_ref, z_ref, carry):
        i = pl.program_id(0)
        wmix = _mix_groups(wmix_ref)

        @pl.when(i == 0)
        def _():
            carry[...] = jnp.zeros_like(carry)

        _, _, h = _norm(x_ref[...], g_ref[...])
        h = h.astype(BF16)
        u = jnp.concatenate([_dot(h, win_ref[0]), _dot(h, win_ref[1])], axis=1)
        z = jnp.concatenate([_dot(h, win_ref[2]), _dot(h, win_ref[3])], axis=1)
        ext = jnp.concatenate([carry[...], u], axis=0)
        carry[...] = u[ROW_TILE - POOL_HALO:]
        mixed = []
        for gi, window in enumerate(POOL_WINDOWS):
            cols = slice(gi * POOL_GC, (gi + 1) * POOL_GC)
            s = ext[:, cols]
            shift = 1
            while shift < window:
                s = s + pltpu.roll(s, shift, 0)
                shift *= 2
            p = s[POOL_HALO:] * _inv_count(i * ROW_TILE, ROW_TILE, window) - u[:, cols]
            p = p.astype(BF16)
            p_ref[:, cols] = p
            mixed.append(_dot(p, wmix[gi]))
        m = jnp.concatenate(mixed, axis=1) * scale_ref[...]
        silu, _ = _silu_parts(z)
        y = _dot((m * silu).astype(BF16), wout_ref[...])
        xo_ref[...] = x_ref[...] + y
        z_ref[...] = z.astype(BF16)

    return pl.pallas_call(
        body,
        name="pool_fwd",
        grid=(T // ROW_TILE,),
        in_specs=[_rows(D_MODEL), _vmem(), _vmem(), _vmem(), _vmem(), _vmem()],
        out_specs=[_rows(D_MODEL), _rows(D_MODEL), _rows(D_MODEL)],
        out_shape=[
            jax.ShapeDtypeStruct((T, D_MODEL), F32),
            jax.ShapeDtypeStruct((T, D_MODEL), BF16),
            jax.ShapeDtypeStruct((T, D_MODEL), BF16),
        ],
        scratch_shapes=[pltpu.VMEM((POOL_HALO, D_MODEL), F32)],
        compiler_params=_params("arbitrary"),
    )(x, g, w_in, w_mix, scale, w_out)


def loss_head(x, g, target):
    T = x.shape[0]

    def body(x_ref, g_ref, t_ref, dx_ref, sq_ref, dg_ref):
        @pl.when(pl.program_id(0) == 0)
        def _():
            sq_ref[...] = jnp.zeros_like(sq_ref)
            dg_ref[...] = jnp.zeros_like(dg_ref)

        gv = g_ref[...]
        xhat, rstd, out = _norm(x_ref[...], gv)
        err = out - t_ref[...]
        sq_ref[...] += jnp.sum(err * err, axis=0, keepdims=True)
        dx, dg = _norm_bwd(err * (1.0 / D_MODEL), xhat, rstd, gv)
        dx_ref[...] = dx
        dg_ref[...] += dg

    return pl.pallas_call(
        body,
        name="loss_head",
        grid=(T // ROW_TILE,),
        in_specs=[_rows(D_MODEL), _vmem(), _rows(D_MODEL)],
        out_specs=[_rows(D_MODEL), _vmem(), _vmem()],
        out_shape=[
            jax.ShapeDtypeStruct((T, D_MODEL), F32),
            jax.ShapeDtypeStruct((1, D_MODEL), F32),
            jax.ShapeDtypeStruct((1, D_MODEL), F32),
        ],
        compiler_params=_params("arbitrary"),
    )(x, g, target)


def pool_bwd_mix(dy, p, z, w_mix, scale, w_out):
    T = dy.shape[0]

    def body(dy_ref, p_ref, z_ref, wmix_ref, scale_ref, wout_ref, dp_ref, dz_ref, dwout_ref, dwmix_ref, dscale_ref):
        @pl.when(pl.program_id(0) == 0)
        def _():
            dwout_ref[...] = jnp.zeros_like(dwout_ref)
            dwmix_ref[...] = jnp.zeros_like(dwmix_ref)
            dscale_ref[...] = jnp.zeros_like(dscale_ref)

        wmix = _mix_groups(wmix_ref)
        dyb = dy_ref[...].astype(BF16)
        da = _dot_nt(dyb, wout_ref[...])
        m_pre = jnp.concatenate(
            [_dot(p_ref[:, gi * POOL_GC:(gi + 1) * POOL_GC], wmix[gi]) for gi in range(len(POOL_WINDOWS))], axis=1)
        sc = scale_ref[...]
        m = m_pre * sc
        zf = z_ref[...].astype(F32)
        silu, dsilu = _silu_parts(zf)
        dwout_ref[...] += _dot_tn((m * silu).astype(BF16), dyb)
        dm = da * silu
        dz_ref[...] = (da * m * dsilu).astype(BF16)
        dscale_ref[...] += jnp.sum(dm * m_pre, axis=0, keepdims=True)
        dmp = (dm * sc).astype(BF16)
        for gi in range(len(POOL_WINDOWS)):
            cols = slice(gi * POOL_GC, (gi + 1) * POOL_GC)
            dw = _dot_tn(p_ref[:, cols], dmp[:, cols])
            for k in range(N_CHIPS):
                dwmix_ref[k, gi * MIX_ROWS:(gi + 1) * MIX_ROWS, :] += dw[k * MIX_ROWS:(k + 1) * MIX_ROWS]
            dp_ref[:, cols] = _dot_nt(dmp[:, cols], wmix[gi]).astype(BF16)

    return pl.pallas_call(
        body,
        name="pool_bwd_mix",
        grid=(T // ROW_TILE,),
        in_specs=[_rows(D_MODEL), _rows(D_MODEL), _rows(D_MODEL), _vmem(), _vmem(), _vmem()],
        out_specs=[_rows(D_MODEL), _rows(D_MODEL), _vmem(), _vmem(), _vmem()],
        out_shape=[
            jax.ShapeDtypeStruct((T, D_MODEL), BF16),
            jax.ShapeDtypeStruct((T, D_MODEL), BF16),
            jax.ShapeDtypeStruct((D_MODEL, D_MODEL), F32),
            jax.ShapeDtypeStruct((len(POOL_WINDOWS), POOL_GC, POOL_GC), F32),
            jax.ShapeDtypeStruct((1, D_MODEL), F32),
        ],
        compiler_params=_params("arbitrary"),
    )(dy, p, z, w_mix, scale, w_out)


def pool_bwd_in(dp, dz, x, dy, g, w_in):
    T = x.shape[0]
    halo_blocks = ROW_TILE // POOL_HALO
    last_halo = T // POOL_HALO - 1

    def body(dp_ref, dph_ref, dz_ref, x_ref, dy_ref, g_ref, win_ref, dx_ref, dwin_ref, dg_ref):
        i = pl.program_id(0)

        @pl.when(i == 0)
        def _():
            dwin_ref[...] = jnp.zeros_like(dwin_ref)
            dg_ref[...] = jnp.zeros_like(dg_ref)

        rows = ROW_TILE + POOL_HALO
        ext = jnp.concatenate([dp_ref[...], dph_ref[...]], axis=0).astype(F32)
        t = i * ROW_TILE + lax.broadcasted_iota(jnp.int32, (rows, 1), 0)
        inside = (t < T).astype(F32)
        du = []
        for gi, window in enumerate(POOL_WINDOWS):
            cols = slice(gi * POOL_GC, (gi + 1) * POOL_GC)
            s = ext[:, cols] * (_inv_count(i * ROW_TILE, rows, window) * inside)
            shift = 1
            while shift < window:
                s = s + pltpu.roll(s, rows - shift, 0)
                shift *= 2
            du.append(s[:ROW_TILE] - ext[:ROW_TILE, cols])
        du = jnp.concatenate(du, axis=1).astype(BF16)
        chunks = [du[:, :POOL_CHUNK], du[:, POOL_CHUNK:], dz_ref[:, :POOL_CHUNK], dz_ref[:, POOL_CHUNK:]]
        gv = g_ref[...]
        xhat, rstd, h = _norm(x_ref[...], gv)
        h = h.astype(BF16)
        dh = jnp.zeros((ROW_TILE, D_MODEL), F32)
        for c in range(N_CHIPS):
            dwin_ref[c] += _dot_tn(h, chunks[c])
            dh = dh + _dot_nt(chunks[c], win_ref[c])
        dx, dg = _norm_bwd(dh, xhat, rstd, gv)
        dx_ref[...] = dx + dy_ref[...]
        dg_ref[...] += dg

    return pl.pallas_call(
        body,
        name="pool_bwd_in",
        grid=(T // ROW_TILE,),
        in_specs=[_rows(D_MODEL),
                  pl.BlockSpec((POOL_HALO, D_MODEL), lambda i: (jnp.minimum((i + 1) * halo_blocks, last_halo), 0)),
                  _rows(D_MODEL), _rows(D_MODEL), _rows(D_MODEL), _vmem(), _vmem()],
        out_specs=[_rows(D_MODEL), _vmem(), _vmem()],
        out_shape=[
            jax.ShapeDtypeStruct((T, D_MODEL), F32),
            jax.ShapeDtypeStruct((N_CHIPS, D_MODEL, POOL_CHUNK), F32),
            jax.ShapeDtypeStruct((1, D_MODEL), F32),
        ],
        compiler_params=_params("arbitrary"),
    )(dp, dp, dz, x, dy, g, w_in)


def attn_bwd_core(dy, q_t, kv, kv_t, z_t, o_t, lse, w_out, sinks):
    T = dy.shape[0]
    tiles = T // ROW_TILE
    blocks = ROW_TILE // ATTN_BLOCK

    def body(dy_ref, qt_ref, kv_ref, kvh_ref, kvt_ref, kvth_ref, zt_ref, ot_ref, lse_ref, w_ref, sink_ref,
             dqt_ref, dkv_ref, dkvh_ref, dzt_ref, dwout_ref, dsink_ref, do_s, dkv_s):
        @pl.when(pl.program_id(0) == 0)
        def _():
            dwout_ref[...] = jnp.zeros_like(dwout_ref)
            dsink_ref[...] = jnp.zeros_like(dsink_ref)

        dyb = dy_ref[...].astype(BF16)
        da = _dot_nt(w_ref[...], dyb)
        of = ot_ref[...].astype(F32)
        silu, dsilu = _silu_parts(zt_ref[...].astype(F32))
        dwout_ref[...] += _dot((of * silu).astype(BF16), dyb)
        do = da * silu
        dzt_ref[...] = (da * of * dsilu).astype(BF16)
        do_s[...] = do.astype(BF16)
        dof = do * of
        dkv_s[...] = jnp.zeros_like(dkv_s)
        tri = _causal_triangle()
        first_penalty = jnp.where(pl.program_id(0) == 0, NEG_BIG, 0.0)
        for hk in range(N_KV_HEADS):
            heads = [hk * GROUP + gi for gi in range(GROUP)]
            sink = _group_row([jnp.full((1, ATTN_BLOCK), sink_ref[h], F32) for h in heads])
            deltas = [jnp.sum(dof[h * HEAD_DIM:(h + 1) * HEAD_DIM], axis=0, keepdims=True) for h in heads]
            k_lanes = slice(hk * HEAD_DIM, (hk + 1) * HEAD_DIM)
            v_lanes = slice(KV_WIDTH + hk * HEAD_DIM, KV_WIDTH + (hk + 1) * HEAD_DIM)
            for b in range(blocks):
                cols = slice(b * ATTN_BLOCK, (b + 1) * ATTN_BLOCK)
                cur_rows = slice((b + 1) * ATTN_BLOCK, (b + 2) * ATTN_BLOCK)
                prev_rows = slice(b * ATTN_BLOCK, (b + 1) * ATTN_BLOCK)
                qt = _group_cols(qt_ref, hk, cols)
                dot = _group_cols(do_s, hk, cols)
                lse_row = _group_row([lse_ref[h:h + 1, cols] for h in heads])
                delta = _group_row([d[:, cols] for d in deltas])
                kc, kp = _block_pair(kv_ref, kvh_ref, b, k_lanes)
                vc, vp = _block_pair(kv_ref, kvh_ref, b, v_lanes)
                ktc, ktp = _block_pair_t(kvt_ref, kvth_ref, b, k_lanes)
                p = jnp.exp(_window_scores(kc, kp, qt, tri, b, first_penalty) - lse_row)
                dp = jnp.where(tri, _dot(vc, dot), _dot(vp, dot))
                ds = (p * (dp - delta)).astype(BF16)
                pb = p.astype(BF16)
                zero = jnp.zeros_like(pb)
                ds_c, ds_p = jnp.where(tri, ds, zero), jnp.where(tri, zero, ds)
                dq = (_dot(ktc, ds_c) + _dot(ktp, ds_p)) * (1.0 / math.sqrt(HEAD_DIM))
                dkv_s[cur_rows, k_lanes] += _dot_nt(ds_c, qt)
                dkv_s[prev_rows, k_lanes] += _dot_nt(ds_p, qt)
                dkv_s[cur_rows, v_lanes] += _dot_nt(jnp.where(tri, pb, zero), dot)
                dkv_s[prev_rows, v_lanes] += _dot_nt(jnp.where(tri, zero, pb), dot)
                dsink = -jnp.exp(sink - lse_row) * delta
                for gi, h in enumerate(heads):
                    part = slice(gi * ATTN_BLOCK, (gi + 1) * ATTN_BLOCK)
                    dqt_ref[h * HEAD_DIM:(h + 1) * HEAD_DIM, cols] = dq[:, part].astype(BF16)
                    dsink_ref[0:1, h:h + 1] += jnp.sum(dsink[:, part], axis=1, keepdims=True)
        dkv_ref[...] = dkv_s[ATTN_BLOCK:]
        dkvh_ref[0] = dkv_s[:ATTN_BLOCK]

    return pl.pallas_call(
        body,
        name="attn_bwd_core",
        grid=(tiles,),
        in_specs=[_rows(D_MODEL), _cols(D_MODEL), _rows(2 * KV_WIDTH), _halo_prev_rows(2 * KV_WIDTH),
                  _cols(2 * KV_WIDTH), _halo_prev_cols(2 * KV_WIDTH), _cols(D_MODEL), _cols(D_MODEL), _cols(N_HEADS),
                  _vmem(), pl.BlockSpec(memory_space=pltpu.SMEM)],
        out_specs=[_cols(D_MODEL), _rows(2 * KV_WIDTH), pl.BlockSpec((1, ATTN_BLOCK, 2 * KV_WIDTH), lambda i: (i, 0, 0)),
                   _cols(D_MODEL), _vmem(), _vmem()],
        out_shape=[
            jax.ShapeDtypeStruct((D_MODEL, T), BF16),
            jax.ShapeDtypeStruct((T, 2 * KV_WIDTH), F32),
            jax.ShapeDtypeStruct((tiles, ATTN_BLOCK, 2 * KV_WIDTH), F32),
            jax.ShapeDtypeStruct((D_MODEL, T), BF16),
            jax.ShapeDtypeStruct((D_MODEL, D_MODEL), F32),
            jax.ShapeDtypeStruct((8, 128), F32),
        ],
        scratch_shapes=[
            pltpu.VMEM((D_MODEL, ROW_TILE), BF16),
            pltpu.VMEM((ROW_TILE + ATTN_BLOCK, 2 * KV_WIDTH), F32),
        ],
        compiler_params=_params("arbitrary"),
    )(dy, q_t, kv, kv, kv_t, kv_t, z_t, o_t, lse, w_out, sinks)


def attn_bwd_in(dq_t, dkv, dkv_halo, dz_t, x, dy, g, wq_t, wz_t, wkv):
    T = x.shape[0]
    tiles = T // ROW_TILE
    kv_at = D_MODEL
    z_at = D_MODEL + 2 * KV_WIDTH

    def body(dqt_ref, dkv_ref, dkvh_ref, dzt_ref, x_ref, dy_ref, g_ref, wq_ref, wz_ref, wkv_ref, dx_ref, dwt_ref, dg_ref):
        i = pl.program_id(0)

        @pl.when(i == 0)
        def _():
            dwt_ref[...] = jnp.zeros_like(dwt_ref)
            dg_ref[...] = jnp.zeros_like(dg_ref)

        more = (i < tiles - 1).astype(F32)
        tail = jnp.concatenate([jnp.zeros((ROW_TILE - ATTN_BLOCK, 2 * KV_WIDTH), F32), dkvh_ref[0] * more], axis=0)
        dkvb = (dkv_ref[...] + tail).astype(BF16)
        gv = g_ref[...]
        xhat, rstd, h = _norm(x_ref[...], gv)
        h = h.astype(BF16)
        dqt = dqt_ref[...]
        dzt = dzt_ref[...]
        dwt_ref[:kv_at] += _dot(dqt, h)
        dwt_ref[kv_at:z_at] += _dot_tn(dkvb, h)
        dwt_ref[z_at:] += _dot(dzt, h)
        dh =_dot_tn(dqt, wq_ref[...]) + _dot_tn(dzt, wz_ref[...]) + _dot_nt(dkvb, wkv_ref[...])
        dx, dg = _norm_bwd(dh, xhat, rstd, gv)
        dx_ref[...] = dx + dy_ref[...]
        dg_ref[...] += dg

    halo_next = pl.BlockSpec((1, ATTN_BLOCK, 2 * KV_WIDTH), lambda i: (jnp.minimum(i + 1, tiles - 1), 0, 0))
    return pl.pallas_call(
        body,
        name="attn_bwd_in",
        grid=(tiles,),
        in_specs=[_cols(D_MODEL), _rows(2 * KV_WIDTH), halo_next, _cols(D_MODEL), _rows(D_MODEL), _rows(D_MODEL),
                  _vmem(), _vmem(), _vmem(), _vmem()],
        out_specs=[_rows(D_MODEL), _vmem(), _vmem()],
        out_shape=[
            jax.ShapeDtypeStruct((T, D_MODEL), F32),
            jax.ShapeDtypeStruct((N_CHIPS * ATTN_CHUNK, D_MODEL), F32),
            jax.ShapeDtypeStruct((1, D_MODEL), F32),
        ],
        compiler_params=_params("arbitrary"),
    )(dq_t, dkv, dkv_halo, dz_t, x, dy, g, wq_t, wz_t, wkv)


def local_step(x, target, norm_g, sinks, final_g, attn_in_w, attn_out_w, pool_in_w, pool_mix_w, pool_scale, pool_out_w):
    saved = []
    for layer in range(4):
        j = layer // 2
        g = norm_g[layer][None, :]
        if layer % 2 == 0:
            wq_t, wkv_t, wz_t, wkv = attn_in_w[j]
            q_t, kv, kv_t, z_t = attn_in_fwd(x, g, wq_t, wkv_t, wz_t, wkv)
            x_new, o_t, lse = attn_core_fwd(q_t, kv, kv_t, z_t, x, attn_out_w[j], sinks[j])
            saved.append((x, g, q_t, kv, kv_t, z_t, o_t, lse))
        else:
            x_new, p, z = pool_fwd(x, g, pool_in_w[j], pool_mix_w[j], pool_scale[j], pool_out_w[j])
            saved.append((x, g, p, z))
        x = x_new
    dx, sq, d_final_g = loss_head(x, final_g[None, :], target)
    grads = {"norm_g": [None] * 4, "attn_w_in": [None] * 2, "attn_sinks": [None] * 2, "attn_w_out": [None] * 2,
             "pool_w_in": [None] * 2, "pool_w_mix": [None] * 2, "pool_scale": [None] * 2, "pool_w_out": [None] * 2,
             "final_g": d_final_g}
    for layer in reversed(range(4)):
        j = layer // 2
        if layer % 2 == 0:
            x_in, g, q_t, kv, kv_t, z_t, o_t, lse = saved[layer]
            wq_t, _, wz_t, wkv = attn_in_w[j]
            dq_t, dkv, dkv_halo, dz_t, dwout, dsink = attn_bwd_core(
                dx, q_t, kv, kv_t, z_t, o_t, lse, attn_out_w[j], sinks[j])
            dx, dwin_t, dg = attn_bwd_in(dq_t, dkv, dkv_halo, dz_t, x_in, dx, g, wq_t, wz_t, wkv)
            grads["attn_w_in"][j] = dwin_t.reshape(N_CHIPS, ATTN_CHUNK, D_MODEL)
            grads["attn_w_out"][j], grads["attn_sinks"][j] = dwout.reshape(N_CHIPS, -1, D_MODEL), dsink[0, :N_HEADS]
        else:
            x_in, g, p, z = saved[layer]
            dp, dz, dwout, dwmix, dscale = pool_bwd_mix(dx, p, z, pool_mix_w[j], pool_scale[j], pool_out_w[j])
            dx, dwin, dg = pool_bwd_in(dp, dz, x_in, dx, g, pool_in_w[j])
            grads["pool_w_in"][j], grads["pool_w_out"][j] = dwin, dwout.reshape(N_CHIPS, -1, D_MODEL)
            grads["pool_w_mix"][j], grads["pool_scale"][j] = dwmix, dscale
        grads["norm_g"][layer] = dg
    return sq, dx, grads


BIG = ("attn_w_in", "attn_w_out", "pool_w_in", "pool_w_mix", "pool_w_out")
SMALL_ROWS = 8


def _half(ref, h, axis=0):
    rows = ref.shape[axis] // 2
    index = (slice(None),) * axis + (pl.ds(h * rows, rows),)
    return ref.at[index]


def _mesh_position():
    return lax.axis_index("x"), lax.axis_index("y"), lax.axis_index("c")


def _other_chips(x, y):
    return [(1 - x, y), (x, 1 - y), (1 - x, 1 - y)]


def _any():
    return pl.BlockSpec(memory_space=pl.ANY)


def _remote(src, dst, sems, index, to):
    send_sems, recv_sems = sems
    return pltpu.make_async_remote_copy(src_ref=src, dst_ref=dst, send_sem=send_sems.at[index], recv_sem=recv_sems.at[index],
                                        device_id=to, device_id_type=MESH_ID)


def _run_exchange(name, body, inputs, out_shapes, n_remote, n_local):
    def kernel_body(*refs):
        ins, outs = refs[:len(inputs)], refs[len(inputs):len(inputs) + len(out_shapes)]
        send_sems, recv_sems, local_sems = refs[len(inputs) + len(out_shapes):]
        body(ins, outs, (send_sems, recv_sems), local_sems)

    return pl.pallas_call(
        kernel_body,
        name=name,
        in_specs=[_any()] * len(inputs),
        out_specs=[_any()] * len(out_shapes),
        out_shape=out_shapes,
        scratch_shapes=[pltpu.SemaphoreType.DMA((n_remote,)), pltpu.SemaphoreType.DMA((n_remote,)),
                        pltpu.SemaphoreType.DMA((max(n_local, 1),))],
    )(*inputs)


def gather_weights(name, blocks, whole):
    def body(ins, outs, sems, local_sems):
        x, y, c = _mesh_position()
        me = 2 * x + y
        sibling = (x, y, 1 - c)
        chips = _other_chips(x, y)
        waits = []
        forwards = []
        for t, (w_ref, out_ref) in enumerate(zip(ins, outs)):
            mine = pltpu.make_async_copy(w_ref, out_ref.at[me], local_sems.at[t])
            mine.start()
            waits.append(mine.wait)
            for j, (px, py) in enumerate(chips):
                peer = 2 * px + py
                if t in whole:
                    cp = _remote(w_ref, out_ref.at[me], sems, 6 * t + j, (px, py, c))
                    cp.start()
                    waits += [_remote(w_ref, out_ref.at[peer], sems, 6 * t + j, (px, py, c)).wait_recv, cp.wait_send]
                    continue
                cp = _remote(_half(w_ref, c), _half(out_ref.at[me], c), sems, 6 * t + j, (px, py, c))
                cp.start()
                landed = _half(out_ref.at[peer], c)
                forwards.append((_remote(landed, landed, sems, 6 * t + j, (px, py, c)).wait_recv,
                                 _remote(landed, landed, sems, 6 * t + 3 + j, sibling)))
                from_sibling = _half(out_ref.at[peer], 1 - c)
                waits += [cp.wait_send, _remote(from_sibling, from_sibling, sems, 6 * t + 3 + j, sibling).wait_recv]
        for arrived, forward in forwards:
            arrived()
            forward.start()
            waits.append(forward.wait_send)
        for wait in waits:
            wait()

    shapes = [jax.ShapeDtypeStruct((N_CHIPS,) + b.shape, b.dtype) for b in blocks]
    return _run_exchange(name, body, blocks, shapes, 6 * len(blocks), len(blocks))


def sibling_swap_halves(name, grads):
    def body(ins, outs, sems, local_sems):
        x, y, c = _mesh_position()
        copies = [_remote(_half(g_ref, 1 - c, axis=1), out_ref, sems, t, (x, y, 1 - c))
                  for t, (g_ref, out_ref) in enumerate(zip(ins, outs))]
        for cp in copies:
            cp.start()
        for cp in copies:
            cp.wait()

    shapes = [jax.ShapeDtypeStruct((g.shape[0], g.shape[1] // 2, g.shape[2]), g.dtype) for g in grads]
    return _run_exchange(name, body, grads, shapes, len(grads), 0)


def chip_sum(name, grads, from_sibling, core):
    n = len(grads)

    def body(core_ref, *refs):
        for a_ref, b_ref, o_ref in zip(refs[:n], refs[n:2 * n], refs[2 * n:]):
            o_ref[0] = (a_ref[0, 0] + b_ref[0]).astype(BF16)

    halves = [g.reshape(N_CHIPS, 2, g.shape[1] // 2, g.shape[2]) for g in grads]
    a_specs = [pl.BlockSpec((1, 1) + h.shape[2:], lambda k, core_ref: (k, core_ref[0], 0, 0)) for h in halves]
    b_specs = [pl.BlockSpec((1,) + b.shape[1:], lambda k, core_ref: (k, 0, 0)) for b in from_sibling]
    return pl.pallas_call(
        body,
        name=name,
        grid_spec=pltpu.PrefetchScalarGridSpec(
            num_scalar_prefetch=1, grid=(N_CHIPS,), in_specs=a_specs + b_specs, out_specs=list(b_specs)),
        out_shape=[jax.ShapeDtypeStruct(b.shape, BF16) for b in from_sibling],
        compiler_params=_params("parallel"),
    )(core, *halves, *from_sibling)


def scatter_to_chips(name, parts):
    def body(ins, outs, sems, local_sems):
        x, y, c = _mesh_position()
        me = 2 * x + y
        waits = []
        for t, (p_ref, out_ref) in enumerate(zip(ins, outs)):
            mine = pltpu.make_async_copy(p_ref.at[me], out_ref.at[me], local_sems.at[t])
            mine.start()
            waits.append(mine.wait)
            for j, (px, py) in enumerate(_other_chips(x, y)):
                peer = 2 * px + py
                cp = _remote(p_ref.at[peer], out_ref.at[me], sems, 3 * t + j, (px, py, c))
                cp.start()
                waits += [_remote(p_ref.at[me], out_ref.at[peer], sems, 3 * t + j, (px, py, c)).wait_recv, cp.wait_send]
        for wait in waits:
            wait()

    shapes = [jax.ShapeDtypeStruct(p.shape, p.dtype) for p in parts]
    return _run_exchange(name, body, parts, shapes, 3 * len(parts), len(parts))


def sum_chips(name, parts):
    n = len(parts)
    steps = 2

    def body(*refs):
        for p_ref, o_ref in zip(refs[:n], refs[n:]):
            total = p_ref[0].astype(F32)
            for k in range(1, N_CHIPS):
                total = total + p_ref[k].astype(F32)
            o_ref[...] = total

    return pl.pallas_call(
        body,
        name=name,
        grid=(steps,),
        in_specs=[pl.BlockSpec((N_CHIPS, p.shape[1] // steps, p.shape[2]), lambda r: (0, r, 0)) for p in parts],
        out_specs=[pl.BlockSpec((p.shape[1] // steps, p.shape[2]), lambda r: (r, 0)) for p in parts],
        out_shape=[jax.ShapeDtypeStruct(p.shape[1:], F32) for p in parts],
        compiler_params=_params("parallel"),
    )(*parts)


def sibling_join_halves(name, halves):
    n = len(halves[0])
    flat = [h for per_layer in halves for h in per_layer]

    def body(ins, outs, sems, local_sems):
        x, y, c = _mesh_position()
        waits = []
        for i, h_ref in enumerate(ins):
            j, t = divmod(i, n)
            mine = pltpu.make_async_copy(h_ref, _half(outs[t].at[j], c), local_sems.at[i])
            mine.start()
            cp = _remote(h_ref, _half(outs[t].at[j], c), sems, i, (x, y, 1 - c))
            cp.start()
            waits += [mine.wait, cp.wait_send, _remote(h_ref, _half(outs[t].at[j], 1 - c), sems, i, (x, y, 1 - c)).wait_recv]
        for wait in waits:
            wait()

    shapes = [jax.ShapeDtypeStruct((2, 2 * h.shape[0], h.shape[1]), h.dtype) for h in halves[0]]
    return _run_exchange(name, body, flat, shapes, len(flat), len(flat))


def all_reduce_small(v):
    n_dev = 8

    def body(v_ref, out_ref, slots, send_sems, recv_sems):
        x, y, c = _mesh_position()
        me = 4 * x + 2 * y + c
        slots[me] = v_ref[...]
        peers = []
        for r in range(1, n_dev):
            fx, fy, fc = (r >> 2) & 1, (r >> 1) & 1, r & 1
            peers.append((1 - x if fx else x, 1 - y if fy else y, 1 - c if fc else c))
        sends = [
            pltpu.make_async_remote_copy(
                src_ref=v_ref, dst_ref=slots.at[me], send_sem=send_sems.at[r], recv_sem=recv_sems.at[r],
                device_id=peer, device_id_type=MESH_ID)
            for r, peer in enumerate(peers)
        ]
        for cp in sends:
            cp.start()
        for r, (px, py, pc) in enumerate(peers):
            pltpu.make_async_remote_copy(
                src_ref=v_ref, dst_ref=slots.at[4 * px + 2 * py + pc], send_sem=send_sems.at[r], recv_sem=recv_sems.at[r],
                device_id=(px, py, pc), device_id_type=MESH_ID).wait_recv()
        for cp in sends:
            cp.wait_send()
        total = slots[0]
        for d in range(1, n_dev):
            total = total + slots[d]
        out_ref[...] = total

    return pl.pallas_call(
        body,
        name="all_reduce_small",
        in_specs=[_vmem()],
        out_specs=_vmem(),
        out_shape=jax.ShapeDtypeStruct(v.shape, v.dtype),
        scratch_shapes=[pltpu.VMEM((n_dev,) + v.shape, v.dtype), pltpu.SemaphoreType.DMA((n_dev - 1,)),
                        pltpu.SemaphoreType.DMA((n_dev - 1,))],
    )(v)


def adamw(name, ws, gs, ms, vs, steps):
    n = len(ws)

    def body(*refs):
        ins, outs = refs[:4 * n], refs[4 * n:]
        for t in range(n):
            w_ref, g_ref, m_ref, v_ref = (ins[q * n + t] for q in range(4))
            d_ref, mo_ref, vo_ref = (outs[q * n + t] for q in range(3))
            gv = g_ref[...]
            m2 = ADAM_B1 * m_ref[...] + (1.0 - ADAM_B1) * gv
            v2 = ADAM_B2 * v_ref[...] + (1.0 - ADAM_B2) * (gv * gv)
            m_hat = m2 / (1.0 - ADAM_B1 ** ADAM_STEP)
            v_hat = v2 / (1.0 - ADAM_B2 ** ADAM_STEP)
            d_ref[...] = -ADAM_LR * (m_hat / (jnp.sqrt(v_hat) + ADAM_EPS) + ADAM_WD * w_ref[...])
            mo_ref[...] = m2
            vo_ref[...] = v2

    specs = [pl.BlockSpec((1, w.shape[1] // steps, w.shape[2]), lambda j, r: (j, r, 0)) for w in ws]
    shapes = [jax.ShapeDtypeStruct(w.shape, F32) for w in ws]
    out = pl.pallas_call(
        body, name=name, grid=(ws[0].shape[0], steps), in_specs=specs * 4, out_specs=specs * 3, out_shape=shapes * 3,
        compiler_params=pltpu.CompilerParams(dimension_semantics=("parallel", "parallel"), vmem_limit_bytes=VMEM_LIMIT_BYTES),
    )(*ws, *gs, *ms, *vs)
    return out[:n], out[n:2 * n], out[2 * n:]


def kernel(x, norm_g, attn_w_in, attn_sinks, attn_w_out, pool_w_in, pool_w_mix, pool_scale, pool_w_out, final_g, loss_target, m_norm_g, m_attn_w_in, m_attn_sinks, m_attn_w_out, m_pool_w_in, m_pool_w_mix, m_pool_scale, m_pool_w_out, m_final_g, v_norm_g, v_attn_w_in, v_attn_sinks, v_attn_w_out, v_pool_w_in, v_pool_w_mix, v_pool_scale, v_pool_w_out, v_final_g):
    cx, cy, cc = _mesh_position()
    chip = 2 * cx + cy

    core = cc.astype(jnp.int32).reshape(1)

    def blocks_2d(a_in, a_out, p_in, p_mix, p_out):
        return [a_in, a_out, p_in, p_mix.reshape(2, POOL_GC, POOL_GC), p_out]

    w_blocks = blocks_2d(attn_w_in, attn_w_out, pool_w_in, pool_w_mix, pool_w_out)
    attn_in_w, attn_out_w, pool_in_w, pool_mix_w, pool_out_w, scale_full = [], [], [], [], [], []
    for j in range(2):
        a_in, a_out, p_in, p_mix, p_out, scale = gather_weights(
            f"gather_weights_{j}", [w[j].astype(BF16) for w in w_blocks] + [pool_scale[j][None, :]], whole=(len(BIG),))
        full = a_in.transpose(1, 0, 2).reshape(D_MODEL, N_CHIPS * ATTN_CHUNK)
        wkv = full[:, D_MODEL:D_MODEL + 2 * KV_WIDTH]
        attn_in_w.append((full[:, :D_MODEL].T, wkv.T, full[:, D_MODEL + 2 * KV_WIDTH:].T, wkv))
        attn_out_w.append(a_out.reshape(D_MODEL, D_MODEL))
        pool_in_w.append(p_in)
        pool_mix_w.append(p_mix)
        pool_out_w.append(p_out.reshape(D_MODEL, D_MODEL))
        scale_full.append(scale.reshape(1, D_MODEL))

    sq, grad_x, g = local_step(x[0], loss_target[0], norm_g, attn_sinks, final_g, attn_in_w, attn_out_w, pool_in_w,
                               pool_mix_w, scale_full, pool_out_w)

    last = jnp.concatenate([g["attn_sinks"][0], g["attn_sinks"][1], jnp.sum(sq).reshape(1),
                            jnp.zeros((D_MODEL - 2 * N_HEADS - 1,), F32)])[None, :]
    small = jnp.concatenate(g["norm_g"] + [g["final_g"]] + g["pool_scale"] + [last], axis=0)
    small = all_reduce_small(small)
    loss = 0.5 * small[7, 2 * N_HEADS] / D_MODEL
    g_norm = small[0:4]
    g_final = small[4]
    g_scale = lax.dynamic_slice(small[5:7], (0, chip * 256), (2, 256))
    g_sinks = small[7, :2 * N_HEADS].reshape(2, N_HEADS)

    reduced = []
    for j in range(2):
        mine = [g[name][j] for name in BIG]
        from_sibling = sibling_swap_halves(f"sibling_swap_halves_{j}", mine)
        parts = chip_sum(f"chip_sum_{j}", mine, from_sibling, core)
        reduced.append(sum_chips(f"sum_chips_{j}", scatter_to_chips(f"scatter_to_chips_{j}", parts)))
    g_blocks = list(sibling_join_halves("sibling_join_halves", reduced))
    g_blocks[0] = g_blocks[0].transpose(0, 2, 1)

    shapes = [w.shape for w in (attn_w_in, attn_w_out, pool_w_in, pool_w_mix, pool_w_out)]
    d_blocks, m_blocks, v_blocks = adamw(
        "adamw", w_blocks, g_blocks, blocks_2d(m_attn_w_in, m_attn_w_out, m_pool_w_in, m_pool_w_mix, m_pool_w_out),
        blocks_2d(v_attn_w_in, v_attn_w_out, v_pool_w_in, v_pool_w_mix, v_pool_w_out), steps=4)
    g_big, d_big, m_big, v_big = (
        {name: b.reshape(s) for name, b, s in zip(BIG, blocks, shapes)} for blocks in (g_blocks, d_blocks, m_blocks, v_blocks))

    def small_pack(ng, fg, sc, sk):
        row7 = jnp.concatenate([sk.reshape(2 * N_HEADS), jnp.zeros((D_MODEL - 2 * N_HEADS,), F32)])[None, :]
        sc = jnp.concatenate([sc, jnp.zeros((2, D_MODEL - 256), F32)], axis=1)
        return jnp.concatenate([ng, fg[None, :], sc, row7], axis=0)

    packs = [small_pack(norm_g, final_g, pool_scale, attn_sinks), small_pack(g_norm, g_final, g_scale, g_sinks),
             small_pack(m_norm_g, m_final_g, m_pool_scale, m_attn_sinks), small_pack(v_norm_g, v_final_g, v_pool_scale, v_attn_sinks)]
    small_out = [out[0] for out in adamw("adamw_small", *[[p[None]] for p in packs], steps=1)]

    def small_unpack(t):
        t = t[0]
        return {"norm_g": t[0:4], "final_g": t[4], "pool_scale": t[5:7, :256], "attn_sinks": t[7, :2 * N_HEADS].reshape(2, N_HEADS)}

    d_small, m_small, v_small = (small_unpack(t) for t in small_out)
    g_small = {"norm_g": g_norm, "final_g": g_final, "pool_scale": g_scale, "attn_sinks": g_sinks}

    names = ["norm_g", "attn_w_in", "attn_sinks", "attn_w_out", "pool_w_in", "pool_w_mix", "pool_scale", "pool_w_out", "final_g"]

    def ordered(bigs, smalls):
        return [bigs[n] if n in bigs else smalls[n] for n in names]

    return (loss, grad_x[None], *ordered(g_big, g_small), *ordered(d_big, d_small), *ordered(m_big, m_small),
            *ordered(v_big, v_small))
```

```python
import functools
import math

import jax
import jax.numpy as jnp
from jax import lax
from jax.experimental import pallas as pl
from jax.experimental.pallas import tpu as pltpu

F32 = jnp.float32
BF16 = jnp.bfloat16

D_MODEL = 1024
N_HEADS = 16
N_KV_HEADS = 4
GROUP = N_HEADS // N_KV_HEADS
HEAD_DIM = 64
KV_WIDTH = N_KV_HEADS * HEAD_DIM
ATTN_BLOCK = 128
POOL_WINDOWS = (2, 4, 8, 16)
POOL_GC = 256
POOL_HALO = 16
EPS = 1e-6
N_CHIPS = 4
ATTN_CHUNK = 640
POOL_CHUNK = 512
ROW_TILE = 512
NEG_BIG = -1e30
VMEM_LIMIT_BYTES = 60 * 1024 * 1024

ADAM_LR = 0.001
ADAM_B1 = 0.9
ADAM_B2 = 0.999
ADAM_EPS = 1e-08
ADAM_WD = 0.01
ADAM_STEP = 10

MESH_ID = pl.DeviceIdType.MESH


def _dot(a, b):
    return jnp.dot(a, b, preferred_element_type=F32)


def _dot_nt(a, b):
    return lax.dot_general(a, b, (((1,), (1,)), ((), ())), preferred_element_type=F32)


def _dot_tn(a, b):
    return lax.dot_general(a, b, (((0,), (0,)), ((), ())), preferred_element_type=F32)


def _vmem():
    return pl.BlockSpec(memory_space=pltpu.VMEM)


def _rows(width, tile=ROW_TILE):
    return pl.BlockSpec((tile, width), lambda i: (i, 0))


def _params(semantics):
    return pltpu.CompilerParams(dimension_semantics=(semantics,), vmem_limit_bytes=VMEM_LIMIT_BYTES)


def _norm(xf, g):
    rstd = lax.rsqrt(jnp.mean(xf * xf, axis=-1, keepdims=True) + EPS)
    xhat = xf * rstd
    return xhat, rstd, xhat * g


def _norm_bwd(dh, xhat, rstd, g):
    dg = jnp.sum(dh * xhat, axis=0, keepdims=True)
    dxhat = dh * g
    dx = rstd * (dxhat - xhat * jnp.mean(dxhat * xhat, axis=-1, keepdims=True))
    return dx, dg


def _silu_parts(zf):
    sig = jax.nn.sigmoid(zf)
    return zf * sig, sig * (1.0 + zf * (1.0 - sig))


def _cols(height, tile=ROW_TILE):
    return pl.BlockSpec((height, tile), lambda i: (0, i))


def _halo_prev_rows(width):
    per_tile = ROW_TILE // ATTN_BLOCK
    return pl.BlockSpec((ATTN_BLOCK, width), lambda i: (jnp.maximum(i * per_tile - 1, 0), 0))


def _halo_prev_cols(height):
    per_tile = ROW_TILE // ATTN_BLOCK
    return pl.BlockSpec((height, ATTN_BLOCK), lambda i: (0, jnp.maximum(i * per_tile - 1, 0)))


def attn_in_fwd(x, g, wq_t, wkv_t, wz_t, wkv):
    T = x.shape[0]

    def body(x_ref, g_ref, wq_ref, wkvt_ref, wz_ref, wkv_ref, qt_ref, kv_ref, kvt_ref, zt_ref):
        _, _, h = _norm(x_ref[...], g_ref[...])
        h = h.astype(BF16)
        scale = 1.0 / math.sqrt(HEAD_DIM)
        qt_ref[...] = (_dot_nt(wq_ref[...], h) * scale).astype(BF16)
        zt_ref[...] = _dot_nt(wz_ref[...], h).astype(BF16)
        kvt_ref[...] = _dot_nt(wkvt_ref[...], h).astype(BF16)
        kv_ref[...] = _dot(h, wkv_ref[...]).astype(BF16)

    return pl.pallas_call(
        body,
        name="attn_in_fwd",
        grid=(T // ROW_TILE,),
        in_specs=[_rows(D_MODEL), _vmem(), _vmem(), _vmem(), _vmem(), _vmem()],
        out_specs=[_cols(D_MODEL), _rows(2 * KV_WIDTH), _cols(2 * KV_WIDTH), _cols(D_MODEL)],
        out_shape=[
            jax.ShapeDtypeStruct((D_MODEL, T), BF16),
            jax.ShapeDtypeStruct((T, 2 * KV_WIDTH), BF16),
            jax.ShapeDtypeStruct((2 * KV_WIDTH, T), BF16),
            jax.ShapeDtypeStruct((D_MODEL, T), BF16),
        ],
        compiler_params=_params("parallel"),
    )(x, g, wq_t, wkv_t, wz_t, wkv)


def _causal_triangle():
    shape = (ATTN_BLOCK, GROUP * ATTN_BLOCK)
    kj = lax.broadcasted_iota(jnp.int32, shape, 0)
    qi = lax.broadcasted_iota(jnp.int32, shape, 1) & (ATTN_BLOCK - 1)
    return kj <= qi


def _group_cols(ref, hk, cols):
    return jnp.concatenate(
        [ref[(hk * GROUP + gi) * HEAD_DIM:(hk * GROUP + gi + 1) * HEAD_DIM, cols] for gi in range(GROUP)], axis=1)


def _group_row(values):
    return jnp.concatenate(values, axis=1)


def _block_pair(ref, halo_ref, b, lanes):
    cur = ref[b * ATTN_BLOCK:(b + 1) * ATTN_BLOCK, lanes]
    prev = halo_ref[:, lanes] if b == 0 else ref[(b - 1) * ATTN_BLOCK:b * ATTN_BLOCK, lanes]
    return cur, prev


def _block_pair_t(ref, halo_ref, b, rows):
    cur = ref[rows, b * ATTN_BLOCK:(b + 1) * ATTN_BLOCK]
    prev = halo_ref[rows, :] if b == 0 else ref[rows, (b - 1) * ATTN_BLOCK:b * ATTN_BLOCK]
    return cur, prev


def _window_scores(kc, kp, qt, tri, b, first_penalty):
    sp = _dot(kp, qt)
    if b == 0:
        sp = sp + first_penalty
    return jnp.where(tri, _dot(kc, qt), sp)


def attn_core_fwd(q_t, kv, kv_t, z_t, x, w_out, sinks):
    T = x.shape[0]
    blocks = ROW_TILE // ATTN_BLOCK

    def body(qt_ref, kv_ref, kvh_ref, kvt_ref, kvth_ref, zt_ref, x_ref, w_ref, sink_ref, xo_ref, ot_ref, lse_ref, oacc):
        tri = _causal_triangle()
        first_penalty = jnp.where(pl.program_id(0) == 0, NEG_BIG, 0.0)
        for hk in range(N_KV_HEADS):
            heads = [hk * GROUP + gi for gi in range(GROUP)]
            sink = _group_row([jnp.full((1, ATTN_BLOCK), sink_ref[h], F32) for h in heads])
            k_lanes = slice(hk * HEAD_DIM, (hk + 1) * HEAD_DIM)
            v_rows = slice(KV_WIDTH + hk * HEAD_DIM, KV_WIDTH + (hk + 1) * HEAD_DIM)
            for b in range(blocks):
                cols = slice(b * ATTN_BLOCK, (b + 1) * ATTN_BLOCK)
                qt = _group_cols(qt_ref, hk, cols)
                kc, kp = _block_pair(kv_ref, kvh_ref, b, k_lanes)
                s = _window_scores(kc, kp, qt, tri, b, first_penalty)
                m = jnp.maximum(jnp.max(s, axis=0, keepdims=True), sink)
                p = jnp.exp(s - m)
                denom = jnp.sum(p, axis=0, keepdims=True) + jnp.exp(sink - m)
                pb = p.astype(BF16)
                zero = jnp.zeros_like(pb)
                vtc, vtp = _block_pair_t(kvt_ref, kvth_ref, b, v_rows)
                o = (_dot(vtc, jnp.where(tri, pb, zero)) + _dot(vtp, jnp.where(tri, zero, pb))) * (1.0 / denom)
                lse = m + jnp.log(denom)
                for gi, h in enumerate(heads):
                    part = slice(gi * ATTN_BLOCK, (gi + 1) * ATTN_BLOCK)
                    oacc[h * HEAD_DIM:(h + 1) * HEAD_DIM, cols] = o[:, part]
                    lse_ref[h:h + 1, cols] = lse[:, part]
        of = oacc[...]
        silu, _ = _silu_parts(zt_ref[...].astype(F32))
        y = _dot_tn((of * silu).astype(BF16), w_ref[...])
        xo_ref[...] = x_ref[...] + y
        ot_ref[...] = of.astype(BF16)

    return pl.pallas_call(
        body,
        name="attn_core_fwd",
        grid=(T // ROW_TILE,),
        in_specs=[_cols(D_MODEL), _rows(2 * KV_WIDTH), _halo_prev_rows(2 * KV_WIDTH), _cols(2 * KV_WIDTH),
                  _halo_prev_cols(2 * KV_WIDTH), _cols(D_MODEL), _rows(D_MODEL), _vmem(),
                  pl.BlockSpec(memory_space=pltpu.SMEM)],
        out_specs=[_rows(D_MODEL), _cols(D_MODEL), _cols(N_HEADS)],
        out_shape=[
            jax.ShapeDtypeStruct((T, D_MODEL), F32),
            jax.ShapeDtypeStruct((D_MODEL, T), BF16),
            jax.ShapeDtypeStruct((N_HEADS, T), F32),
        ],
        scratch_shapes=[pltpu.VMEM((D_MODEL, ROW_TILE), F32)],
        compiler_params=_params("parallel"),
    )(q_t, kv, kv, kv_t, kv_t, z_t, x, w_out, sinks)


def _inv_count(first_row, rows, window):
    t = first_row + lax.broadcasted_iota(jnp.int32, (rows, 1), 0)
    return 1.0 / jnp.minimum(t + 1, window).astype(F32)


MIX_ROWS = POOL_GC // N_CHIPS


def _mix_groups(wmix_ref):
    return [jnp.concatenate([wmix_ref[k, gi * MIX_ROWS:(gi + 1) * MIX_ROWS, :] for k in range(N_CHIPS)], axis=0)
            for gi in range(len(POOL_WINDOWS))]


def pool_fwd(x, g, w_in, w_mix, scale, w_out):
    T = x.shape[0]

    def body(x_ref, g_ref, win_ref, wmix_ref, scale_ref, wout_ref, xo_ref, p_ref, z_ref, carry):
        i = pl.program_id(0)
        wmix = _mix_groups(wmix_ref)

        @pl.when(i == 0)
        def _():
            carry[...] = jnp.zeros_like(carry)

        _, _, h = _norm(x_ref[...], g_ref[...])
        h = h.astype(BF16)
        u = jnp.concatenate([_dot(h, win_ref[0]), _dot(h, win_ref[1])], axis=1)
        z = jnp.concatenate([_dot(h, win_ref[2]), _dot(h, win_ref[3])], axis=1)
        ext = jnp.concatenate([carry[...], u], axis=0)
        carry[...] = u[ROW_TILE - POOL_HALO:]
        mixed = []
        for gi, window in enumerate(POOL_WINDOWS):
            cols = slice(gi * POOL_GC, (gi + 1) * POOL_GC)
            s = ext[:, cols]
            shift = 1
            while shift < window:
                s = s + pltpu.roll(s, shift, 0)
                shift *= 2
            p = s[POOL_HALO:] * _inv_count(i * ROW_TILE, ROW_TILE, window) - u[:, cols]
            p = p.astype(BF16)
            p_ref[:, cols] = p
            mixed.append(_dot(p, wmix[gi]))
        m = jnp.concatenate(mixed, axis=1) * scale_ref[...]
        silu, _ = _silu_parts(z)
        y = _dot((m * silu).astype(BF16), wout_ref[...])
        xo_ref[...] = x_ref[...] + y
        z_ref[...] = z.astype(BF16)

    return pl.pallas_call(
        body,
        name="pool_fwd",
        grid=(T // ROW_TILE,),
        in_specs=[_rows(D_MODEL), _vmem(), _vmem(), _vmem(), _vmem(), _vmem()],
        out_specs=[_rows(D_MODEL), _rows(D_MODEL), _rows(D_MODEL)],
        out_shape=[
            jax.ShapeDtypeStruct((T, D_MODEL), F32),
            jax.ShapeDtypeStruct((T, D_MODEL), BF16),
            jax.ShapeDtypeStruct((T, D_MODEL), BF16),
        ],
        scratch_shapes=[pltpu.VMEM((POOL_HALO, D_MODEL), F32)],
        compiler_params=_params("arbitrary"),
    )(x, g, w_in, w_mix, scale, w_out)


def loss_head(x, g, target):
    T = x.shape[0]

    def body(x_ref, g_ref, t_ref, dx_ref, sq_ref, dg_ref):
        @pl.when(pl.program_id(0) == 0)
        def _():
            sq_ref[...] = jnp.zeros_like(sq_ref)
            dg_ref[...] = jnp.zeros_like(dg_ref)

        gv = g_ref[...]
        xhat, rstd, out = _norm(x_ref[...], gv)
        err = out - t_ref[...]
        sq_ref[...] += jnp.sum(err * err, axis=0, keepdims=True)
        dx, dg = _norm_bwd(err * (1.0 / D_MODEL), xhat, rstd, gv)
        dx_ref[...] = dx
        dg_ref[...] += dg

    return pl.pallas_call(
        body,
        name="loss_head",
        grid=(T // ROW_TILE,),
        in_specs=[_rows(D_MODEL), _vmem(), _rows(D_MODEL)],
        out_specs=[_rows(D_MODEL), _vmem(), _vmem()],
        out_shape=[
            jax.ShapeDtypeStruct((T, D_MODEL), F32),
            jax.ShapeDtypeStruct((1, D_MODEL), F32),
            jax.ShapeDtypeStruct((1, D_MODEL), F32),
        ],
        compiler_params=_params("arbitrary"),
    )(x, g, target)


def pool_bwd_mix(dy, p, z, w_mix, scale, w_out):
    T = dy.shape[0]

    def body(dy_ref, p_ref, z_ref, wmix_ref, scale_ref, wout_ref, dp_ref, dz_ref, dwout_ref, dwmix_ref, dscale_ref):
        @pl.when(pl.program_id(0) == 0)
        def _():
            dwout_ref[...] = jnp.zeros_like(dwout_ref)
            dwmix_ref[...] = jnp.zeros_like(dwmix_ref)
            dscale_ref[...] = jnp.zeros_like(dscale_ref)

        wmix = _mix_groups(wmix_ref)
        dyb = dy_ref[...].astype(BF16)
        da = _dot_nt(dyb, wout_ref[...])
        m_pre = jnp.concatenate(
            [_dot(p_ref[:, gi * POOL_GC:(gi + 1) * POOL_GC], wmix[gi]) for gi in range(len(POOL_WINDOWS))], axis=1)
        sc = scale_ref[...]
        m = m_pre * sc
        zf = z_ref[...].astype(F32)
        silu, dsilu = _silu_parts(zf)
        dwout_ref[...] += _dot_tn((m * silu).astype(BF16), dyb)
        dm = da * silu
        dz_ref[...] = (da * m * dsilu).astype(BF16)
        dscale_ref[...] += jnp.sum(dm * m_pre, axis=0, keepdims=True)
        dmp = (dm * sc).astype(BF16)
        for gi in range(len(POOL_WINDOWS)):
            cols = slice(gi * POOL_GC, (gi + 1) * POOL_GC)
            dw = _dot_tn(p_ref[:, cols], dmp[:, cols])
            for k in range(N_CHIPS):
                dwmix_ref[k, gi * MIX_ROWS:(gi + 1) * MIX_ROWS, :] += dw[k * MIX_ROWS:(k + 1) * MIX_ROWS]
            dp_ref[:, cols] = _dot_nt(dmp[:, cols], wmix[gi]).astype(BF16)

    return pl.pallas_call(
        body,
        name="pool_bwd_mix",
        grid=(T // ROW_TILE,),
        in_specs=[_rows(D_MODEL), _rows(D_MODEL), _rows(D_MODEL), _vmem(), _vmem(), _vmem()],
        out_specs=[_rows(D_MODEL), _rows(D_MODEL), _vmem(), _vmem(), _vmem()],
        out_shape=[
            jax.ShapeDtypeStruct((T, D_MODEL), BF16),
            jax.ShapeDtypeStruct((T, D_MODEL), BF16),
            jax.ShapeDtypeStruct((D_MODEL, D_MODEL), F32),
            jax.ShapeDtypeStruct((len(POOL_WINDOWS), POOL_GC, POOL_GC), F32),
            jax.ShapeDtypeStruct((1, D_MODEL), F32),
        ],
        compiler_params=_params("arbitrary"),
    )(dy, p, z, w_mix, scale, w_out)


def pool_bwd_in(dp, dz, x, dy, g, w_in):
    T = x.shape[0]
    halo_blocks = ROW_TILE // POOL_HALO
    last_halo = T // POOL_HALO - 1

    def body(dp_ref, dph_ref, dz_ref, x_ref, dy_ref, g_ref, win_ref, dx_ref, dwin_ref, dg_ref):
        i = pl.program_id(0)

        @pl.when(i == 0)
        def _():
            dwin_ref[...] = jnp.zeros_like(dwin_ref)
            dg_ref[...] = jnp.zeros_like(dg_ref)

        rows = ROW_TILE + POOL_HALO
        ext = jnp.concatenate([dp_ref[...], dph_ref[...]], axis=0).astype(F32)
        t = i * ROW_TILE + lax.broadcasted_iota(jnp.int32, (rows, 1), 0)
        inside = (t < T).astype(F32)
        du = []
        for gi, window in enumerate(POOL_WINDOWS):
            cols = slice(gi * POOL_GC, (gi + 1) * POOL_GC)
            s = ext[:, cols] * (_inv_count(i * ROW_TILE, rows, window) * inside)
            shift = 1
            while shift < window:
                s = s + pltpu.roll(s, rows - shift, 0)
                shift *= 2
            du.append(s[:ROW_TILE] - ext[:ROW_TILE, cols])
        du = jnp.concatenate(du, axis=1).astype(BF16)
        chunks = [du[:, :POOL_CHUNK], du[:, POOL_CHUNK:], dz_ref[:, :POOL_CHUNK], dz_ref[:, POOL_CHUNK:]]
        gv = g_ref[...]
        xhat, rstd, h = _norm(x_ref[...], gv)
        h = h.astype(BF16)
        dh = jnp.zeros((ROW_TILE, D_MODEL), F32)
        for c in range(N_CHIPS):
            dwin_ref[c] += _dot_tn(h, chunks[c])
            dh = dh + _dot_nt(chunks[c], win_ref[c])
        dx, dg = _norm_bwd(dh, xhat, rstd, gv)
        dx_ref[...] = dx + dy_ref[...]
        dg_ref[...] += dg

    return pl.pallas_call(
        body,
        name="pool_bwd_in",
        grid=(T // ROW_TILE,),
        in_specs=[_rows(D_MODEL),
                  pl.BlockSpec((POOL_HALO, D_MODEL), lambda i: (jnp.minimum((i + 1) * halo_blocks, last_halo), 0)),
                  _rows(D_MODEL), _rows(D_MODEL), _rows(D_MODEL), _vmem(), _vmem()],
        out_specs=[_rows(D_MODEL), _vmem(), _vmem()],
        out_shape=[
            jax.ShapeDtypeStruct((T, D_MODEL), F32),
            jax.ShapeDtypeStruct((N_CHIPS, D_MODEL, POOL_CHUNK), F32),
            jax.ShapeDtypeStruct((1, D_MODEL), F32),
        ],
        compiler_params=_params("arbitrary"),
    )(dp, dp, dz, x, dy, g, w_in)


def attn_bwd_core(dy, q_t, kv, kv_t, z_t, o_t, lse, w_out, sinks):
    T = dy.shape[0]
    tiles = T // ROW_TILE
    blocks = ROW_TILE // ATTN_BLOCK

    def body(dy_ref, qt_ref, kv_ref, kvh_ref, kvt_ref, kvth_ref, zt_ref, ot_ref, lse_ref, w_ref, sink_ref,
             dqt_ref, dkv_ref, dkvh_ref, dzt_ref, dwout_ref, dsink_ref, do_s, dkv_s):
        @pl.when(pl.program_id(0) == 0)
        def _():
            dwout_ref[...] = jnp.zeros_like(dwout_ref)
            dsink_ref[...] = jnp.zeros_like(dsink_ref)

        dyb = dy_ref[...].astype(BF16)
        da = _dot_nt(w_ref[...], dyb)
        of = ot_ref[...].astype(F32)
        silu, dsilu = _silu_parts(zt_ref[...].astype(F32))
        dwout_ref[...] += _dot((of * silu).astype(BF16), dyb)
        do = da * silu
        dzt_ref[...] = (da * of * dsilu).astype(BF16)
        do_s[...] = do.astype(BF16)
        dof = do * of
        dkv_s[...] = jnp.zeros_like(dkv_s)
        tri = _causal_triangle()
        first_penalty = jnp.where(pl.program_id(0) == 0, NEG_BIG, 0.0)
        for hk in range(N_KV_HEADS):
            heads = [hk * GROUP + gi for gi in range(GROUP)]
            sink = _group_row([jnp.full((1, ATTN_BLOCK), sink_ref[h], F32) for h in heads])
            deltas = [jnp.sum(dof[h * HEAD_DIM:(h + 1) * HEAD_DIM], axis=0, keepdims=True) for h in heads]
            k_lanes = slice(hk * HEAD_DIM, (hk + 1) * HEAD_DIM)
            v_lanes = slice(KV_WIDTH + hk * HEAD_DIM, KV_WIDTH + (hk + 1) * HEAD_DIM)
            for b in range(blocks):
                cols = slice(b * ATTN_BLOCK, (b + 1) * ATTN_BLOCK)
                cur_rows = slice((b + 1) * ATTN_BLOCK, (b + 2) * ATTN_BLOCK)
                prev_rows = slice(b * ATTN_BLOCK, (b + 1) * ATTN_BLOCK)
                qt = _group_cols(qt_ref, hk, cols)
                dot = _group_cols(do_s, hk, cols)
                lse_row = _group_row([lse_ref[h:h + 1, cols] for h in heads])
                delta = _group_row([d[:, cols] for d in deltas])
                kc, kp = _block_pair(kv_ref, kvh_ref, b, k_lanes)
                vc, vp = _block_pair(kv_ref, kvh_ref, b, v_lanes)
                ktc, ktp = _block_pair_t(kvt_ref, kvth_ref, b, k_lanes)
                p = jnp.exp(_window_scores(kc, kp, qt, tri, b, first_penalty) - lse_row)
                dp = jnp.where(tri, _dot(vc, dot), _dot(vp, dot))
                ds = (p * (dp - delta)).astype(BF16)
                pb = p.astype(BF16)
                zero = jnp.zeros_like(pb)
                ds_c, ds_p = jnp.where(tri, ds, zero), jnp.where(tri, zero, ds)
                dq = (_dot(ktc, ds_c) + _dot(ktp, ds_p)) * (1.0 / math.sqrt(HEAD_DIM))
                dkv_s[cur_rows, k_lanes] += _dot_nt(ds_c, qt)
                dkv_s[prev_rows, k_lanes] += _dot_nt(ds_p, qt)
                dkv_s[cur_rows, v_lanes] += _dot_nt(jnp.where(tri, pb, zero), dot)
                dkv_s[prev_rows, v_lanes] += _dot_nt(jnp.where(tri, zero, pb), dot)
                dsink = -jnp.exp(sink - lse_row) * delta
                for gi, h in enumerate(heads):
                    part = slice(gi * ATTN_BLOCK, (gi + 1) * ATTN_BLOCK)
                    dqt_ref[h * HEAD_DIM:(h + 1) * HEAD_DIM, cols] = dq[:, part].astype(BF16)
                    dsink_ref[0:1, h:h + 1] += jnp.sum(dsink[:, part], axis=1, keepdims=True)
        dkv_ref[...] = dkv_s[ATTN_BLOCK:]
        dkvh_ref[0] = dkv_s[:ATTN_BLOCK]

    return pl.pallas_call(
        body,
        name="attn_bwd_core",
        grid=(tiles,),
        in_specs=[_rows(D_MODEL), _cols(D_MODEL), _rows(2 * KV_WIDTH), _halo_prev_rows(2 * KV_WIDTH),
                  _cols(2 * KV_WIDTH), _halo_prev_cols(2 * KV_WIDTH), _cols(D_MODEL), _cols(D_MODEL), _cols(N_HEADS),
                  _vmem(), pl.BlockSpec(memory_space=pltpu.SMEM)],
        out_specs=[_cols(D_MODEL), _rows(2 * KV_WIDTH), pl.BlockSpec((1, ATTN_BLOCK, 2 * KV_WIDTH), lambda i: (i, 0, 0)),
                   _cols(D_MODEL), _vmem(), _vmem()],
        out_shape=[
            jax.ShapeDtypeStruct((D_MODEL, T), BF16),
            jax.ShapeDtypeStruct((T, 2 * KV_WIDTH), F32),
            jax.ShapeDtypeStruct((tiles, ATTN_BLOCK, 2 * KV_WIDTH), F32),
            jax.ShapeDtypeStruct((D_MODEL, T), BF16),
            jax.ShapeDtypeStruct((D_MODEL, D_MODEL), F32),
            jax.ShapeDtypeStruct((8, 128), F32),
        ],
        scratch_shapes=[
            pltpu.VMEM((D_MODEL, ROW_TILE), BF16),
            pltpu.VMEM((ROW_TILE + ATTN_BLOCK, 2 * KV_WIDTH), F32),
        ],
        compiler_params=_params("arbitrary"),
    )(dy, q_t, kv, kv, kv_t, kv_t, z_t, o_t, lse, w_out, sinks)


def attn_bwd_in(dq_t, dkv, dkv_halo, dz_t, x, dy, g, wq_t, wz_t, wkv):
    T = x.shape[0]
    tiles = T // ROW_TILE
    kv_at = D_MODEL
    z_at = D_MODEL + 2 * KV_WIDTH

    def body(dqt_ref, dkv_ref, dkvh_ref, dzt_ref, x_ref, dy_ref, g_ref, wq_ref, wz_ref, wkv_ref, dx_ref, dwt_ref, dg_ref):
        i = pl.program_id(0)

        @pl.when(i == 0)
        def _():
            dwt_ref[...] = jnp.zeros_like(dwt_ref)
            dg_ref[...] = jnp.zeros_like(dg_ref)

        more = (i < tiles - 1).astype(F32)
        tail = jnp.concatenate([jnp.zeros((ROW_TILE - ATTN_BLOCK, 2 * KV_WIDTH), F32), dkvh_ref[0] * more], axis=0)
        dkvb = (dkv_ref[...] + tail).astype(BF16)
        gv = g_ref[...]
        xhat, rstd, h = _norm(x_ref[...], gv)
        h = h.astype(BF16)
        dqt = dqt_ref[...]
        dzt = dzt_ref[...]
        dwt_ref[:kv_at] += _dot(dqt, h)
        dwt_ref[kv_at:z_at] += _dot_tn(dkvb, h)
        dwt_ref[z_at:] += _dot(dzt, h)
        dh =_dot_tn(dqt, wq_ref[...]) + _dot_tn(dzt, wz_ref[...]) + _dot_nt(dkvb, wkv_ref[...])
        dx, dg = _norm_bwd(dh, xhat, rstd, gv)
        dx_ref[...] = dx + dy_ref[...]
        dg_ref[...] += dg

    halo_next = pl.BlockSpec((1, ATTN_BLOCK, 2 * KV_WIDTH), lambda i: (jnp.minimum(i + 1, tiles - 1), 0, 0))
    return pl.pallas_call(
        body,
        name="attn_bwd_in",
        grid=(tiles,),
        in_specs=[_cols(D_MODEL), _rows(2 * KV_WIDTH), halo_next, _cols(D_MODEL), _rows(D_MODEL), _rows(D_MODEL),
                  _vmem(), _vmem(), _vmem(), _vmem()],
        out_specs=[_rows(D_MODEL), _vmem(), _vmem()],
        out_shape=[
            jax.ShapeDtypeStruct((T, D_MODEL), F32),
            jax.ShapeDtypeStruct((N_CHIPS * ATTN_CHUNK, D_MODEL), F32),
            jax.ShapeDtypeStruct((1, D_MODEL), F32),
        ],
        compiler_params=_params("arbitrary"),
    )(dq_t, dkv, dkv_halo, dz_t, x, dy, g, wq_t, wz_t, wkv)


def local_step(x, target, norm_g, sinks, final_g, attn_in_w, attn_out_w, pool_in_w, pool_mix_w, pool_scale, pool_out_w):
    saved = []
    for layer in range(4):
        j = layer // 2
        g = norm_g[layer][None, :]
        if layer % 2 == 0:
            wq_t, wkv_t, wz_t, wkv = attn_in_w[j]
            q_t, kv, kv_t, z_t = attn_in_fwd(x, g, wq_t, wkv_t, wz_t, wkv)
            x_new, o_t, lse = attn_core_fwd(q_t, kv, kv_t, z_t, x, attn_out_w[j], sinks[j])
            saved.append((x, g, q_t, kv, kv_t, z_t, o_t, lse))
        else:
            x_new, p, z = pool_fwd(x, g, pool_in_w[j], pool_mix_w[j], pool_scale[j], pool_out_w[j])
            saved.append((x, g, p, z))
        x = x_new
    dx, sq, d_final_g = loss_head(x, final_g[None, :], target)
    grads = {"norm_g": [None] * 4, "attn_w_in": [None] * 2, "attn_sinks": [None] * 2, "attn_w_out": [None] * 2,
             "pool_w_in": [None] * 2, "pool_w_mix": [None] * 2, "pool_scale": [None] * 2, "pool_w_out": [None] * 2,
             "final_g": d_final_g}
    for layer in reversed(range(4)):
        j = layer // 2
        if layer % 2 == 0:
            x_in, g, q_t, kv, kv_t, z_t, o_t, lse = saved[layer]
            wq_t, _, wz_t, wkv = attn_in_w[j]
            dq_t, dkv, dkv_halo, dz_t, dwout, dsink = attn_bwd_core(
                dx, q_t, kv, kv_t, z_t, o_t, lse, attn_out_w[j], sinks[j])
            dx, dwin_t, dg = attn_bwd_in(dq_t, dkv, dkv_halo, dz_t, x_in, dx, g, wq_t, wz_t, wkv)
            grads["attn_w_in"][j] = dwin_t.reshape(N_CHIPS, ATTN_CHUNK, D_MODEL)
            grads["attn_w_out"][j], grads["attn_sinks"][j] = dwout.reshape(N_CHIPS, -1, D_MODEL), dsink[0, :N_HEADS]
        else:
            x_in, g, p, z = saved[layer]
            dp, dz, dwout, dwmix, dscale = pool_bwd_mix(dx, p, z, pool_mix_w[j], pool_scale[j], pool_out_w[j])
            dx, dwin, dg = pool_bwd_in(dp, dz, x_in, dx, g, pool_in_w[j])
            grads["pool_w_in"][j], grads["pool_w_out"][j] = dwin, dwout.reshape(N_CHIPS, -1, D_MODEL)
            grads["pool_w_mix"][j], grads["pool_scale"][j] = dwmix, dscale
        grads["norm_g"][layer] = dg
    return sq, dx, grads


BIG = ("attn_w_in", "attn_w_out", "pool_w_in", "pool_w_mix", "pool_w_out")
SMALL_ROWS = 8


def _half(ref, h, axis=0):
    rows = ref.shape[axis] // 2
    index = (slice(None),) * axis + (pl.ds(h * rows, rows),)
    return ref.at[index]


def _mesh_position():
    return lax.axis_index("x"), lax.axis_index("y"), lax.axis_index("c")


def _other_chips(x, y):
    return [(1 - x, y), (x, 1 - y), (1 - x, 1 - y)]


def _any():
    return pl.BlockSpec(memory_space=pl.ANY)


def _remote(src, dst, sems, index, to):
    send_sems, recv_sems = sems
    return pltpu.make_async_remote_copy(src_ref=src, dst_ref=dst, send_sem=send_sems.at[index], recv_sem=recv_sems.at[index],
                                        device_id=to, device_id_type=MESH_ID)


def _run_exchange(name, body, inputs, out_shapes, n_remote, n_local):
    def kernel_body(*refs):
        ins, outs = refs[:len(inputs)], refs[len(inputs):len(inputs) + len(out_shapes)]
        send_sems, recv_sems, local_sems = refs[len(inputs) + len(out_shapes):]
        body(ins, outs, (send_sems, recv_sems), local_sems)

    return pl.pallas_call(
        kernel_body,
        name=name,
        in_specs=[_any()] * len(inputs),
        out_specs=[_any()] * len(out_shapes),
        out_shape=out_shapes,
        scratch_shapes=[pltpu.SemaphoreType.DMA((n_remote,)), pltpu.SemaphoreType.DMA((n_remote,)),
                        pltpu.SemaphoreType.DMA((max(n_local, 1),))],
    )(*inputs)


def gather_weights(name, blocks, whole):
    def body(ins, outs, sems, local_sems):
        x, y, c = _mesh_position()
        me = 2 * x + y
        sibling = (x, y, 1 - c)
        chips = _other_chips(x, y)
        waits = []
        forwards = []
        for t, (w_ref, out_ref) in enumerate(zip(ins, outs)):
            for j, (px, py) in enumerate(chips):
                peer = 2 * px + py
                if t in whole:
                    cp = _remote(w_ref, out_ref.at[me], sems, 6 * t + j, (px, py, c))
                    cp.start()
                    waits += [_remote(w_ref, out_ref.at[peer], sems, 6 * t + j, (px, py, c)).wait_recv, cp.wait_send]
                    continue
                cp = _remote(_half(w_ref, c), _half(out_ref.at[me], c), sems, 6 * t + j, (px, py, c))
                cp.start()
                landed = _half(out_ref.at[peer], c)
                forwards.append((_remote(landed, landed, sems, 6 * t + j, (px, py, c)).wait_recv,
                                 _remote(landed, landed, sems, 6 * t + 3 + j, sibling)))
                from_sibling = _half(out_ref.at[peer], 1 - c)
                waits += [cp.wait_send, _remote(from_sibling, from_sibling, sems, 6 * t + 3 + j, sibling).wait_recv]
        for arrived, forward in forwards:
            arrived()
            forward.start()
            waits.append(forward.wait_send)
        for wait in waits:
            wait()

    shapes = [jax.ShapeDtypeStruct((N_CHIPS,) + b.shape, b.dtype) for b in blocks]
    return _run_exchange(name, body, blocks, shapes, 6 * len(blocks), 0)


def sibling_swap_halves(name, grads):
    def body(ins, outs, sems, local_sems):
        x, y, c = _mesh_position()
        copies = [_remote(_half(g_ref, 1 - c, axis=1), out_ref, sems, t, (x, y, 1 - c))
                  for t, (g_ref, out_ref) in enumerate(zip(ins, outs))]
        for cp in copies:
            cp.start()
        for cp in copies:
            cp.wait()

    shapes = [jax.ShapeDtypeStruct((g.shape[0], g.shape[1] // 2, g.shape[2]), g.dtype) for g in grads]
    return _run_exchange(name, body, grads, shapes, len(grads), 0)


def chip_sum(name, grads, from_sibling, core):
    n = len(grads)

    def body(core_ref, *refs):
        for a_ref, b_ref, o_ref in zip(refs[:n], refs[n:2 * n], refs[2 * n:]):
            o_ref[0] = (a_ref[0, 0] + b_ref[0]).astype(BF16)

    halves = [g.reshape(N_CHIPS, 2, g.shape[1] // 2, g.shape[2]) for g in grads]
    a_specs = [pl.BlockSpec((1, 1) + h.shape[2:], lambda k, core_ref: (k, core_ref[0], 0, 0)) for h in halves]
    b_specs = [pl.BlockSpec((1,) + b.shape[1:], lambda k, core_ref: (k, 0, 0)) for b in from_sibling]
    return pl.pallas_call(
        body,
        name=name,
        grid_spec=pltpu.PrefetchScalarGridSpec(
            num_scalar_prefetch=1, grid=(N_CHIPS,), in_specs=a_specs + b_specs, out_specs=list(b_specs)),
        out_shape=[jax.ShapeDtypeStruct(b.shape, BF16) for b in from_sibling],
        compiler_params=_params("parallel"),
    )(core, *halves, *from_sibling)


def scatter_to_chips(name, parts):
    def body(ins, outs, sems, local_sems):
        x, y, c = _mesh_position()
        waits = []
        for t, (p_ref, out_ref) in enumerate(zip(ins, outs)):
            for j, (px, py) in enumerate(_other_chips(x, y)):
                cp = _remote(p_ref.at[2 * px + py], out_ref.at[j], sems, 3 * t + j, (px, py, c))
                cp.start()
                waits += [cp.wait_recv, cp.wait_send]
        for wait in waits:
            wait()

    shapes = [jax.ShapeDtypeStruct((3,) + p.shape[1:], p.dtype) for p in parts]
    return _run_exchange(name, body, parts, shapes, 3 * len(parts), 0)


def sum_chips(name, parts, landed, chip):
    n = len(parts)
    steps = 2

    def body(chip_ref, *refs):
        for p_ref, q_ref, o_ref in zip(refs[:n], refs[n:2 * n], refs[2 * n:]):
            total = p_ref[0].astype(F32)
            for j in range(3):
                total = total + q_ref[j].astype(F32)
            o_ref[...] = total

    def rows(p):
        return p.shape[1] // steps

    return pl.pallas_call(
        body,
        name=name,
        grid_spec=pltpu.PrefetchScalarGridSpec(
            num_scalar_prefetch=1, grid=(steps,),
            in_specs=[pl.BlockSpec((1, rows(p), p.shape[2]), lambda r, chip_ref: (chip_ref[0], r, 0)) for p in parts]
            + [pl.BlockSpec((3, rows(p), p.shape[2]), lambda r, chip_ref: (0, r, 0)) for p in parts],
            out_specs=[pl.BlockSpec((rows(p), p.shape[2]), lambda r, chip_ref: (r, 0)) for p in parts]),
        out_shape=[jax.ShapeDtypeStruct(p.shape[1:], F32) for p in parts],
        compiler_params=_params("parallel"),
    )(chip, *parts, *landed)


def sibling_swap_reduced(name, halves):
    def body(ins, outs, sems, local_sems):
        x, y, c = _mesh_position()
        copies = [_remote(h_ref, out_ref, sems, i, (x, y, 1 - c)) for i, (h_ref, out_ref) in enumerate(zip(ins, outs))]
        for cp in copies:
            cp.start()
        for cp in copies:
            cp.wait()

    return _run_exchange(name, body, halves, [jax.ShapeDtypeStruct(h.shape, h.dtype) for h in halves], len(halves), 0)


def all_reduce_small(v):
    n_dev = 8

    def body(v_ref, out_ref, slots, send_sems, recv_sems):
        x, y, c = _mesh_position()
        me = 4 * x + 2 * y + c
        slots[me] = v_ref[...]
        peers = []
        for r in range(1, n_dev):
            fx, fy, fc = (r >> 2) & 1, (r >> 1) & 1, r & 1
            peers.append((1 - x if fx else x, 1 - y if fy else y, 1 - c if fc else c))
        sends = [
            pltpu.make_async_remote_copy(
                src_ref=v_ref, dst_ref=slots.at[me], send_sem=send_sems.at[r], recv_sem=recv_sems.at[r],
                device_id=peer, device_id_type=MESH_ID)
            for r, peer in enumerate(peers)
        ]
        for cp in sends:
            cp.start()
        for r, (px, py, pc) in enumerate(peers):
            pltpu.make_async_remote_copy(
                src_ref=v_ref, dst_ref=slots.at[4 * px + 2 * py + pc], send_sem=send_sems.at[r], recv_sem=recv_sems.at[r],
                device_id=(px, py, pc), device_id_type=MESH_ID).wait_recv()
        for cp in sends:
            cp.wait_send()
        total = slots[0]
        for d in range(1, n_dev):
            total = total + slots[d]
        out_ref[...] = total

    return pl.pallas_call(
        body,
        name="all_reduce_small",
        in_specs=[_vmem()],
        out_specs=_vmem(),
        out_shape=jax.ShapeDtypeStruct(v.shape, v.dtype),
        scratch_shapes=[pltpu.VMEM((n_dev,) + v.shape, v.dtype), pltpu.SemaphoreType.DMA((n_dev - 1,)),
                        pltpu.SemaphoreType.DMA((n_dev - 1,))],
    )(v)


def adamw(name, ws, gs, ms, vs, steps):
    n = len(ws)

    def body(*refs):
        ins, outs = refs[:4 * n], refs[4 * n:]
        for t in range(n):
            w_ref, g_ref, m_ref, v_ref = (ins[q * n + t] for q in range(4))
            d_ref, mo_ref, vo_ref = (outs[q * n + t] for q in range(3))
            gv = g_ref[...]
            m2 = ADAM_B1 * m_ref[...] + (1.0 - ADAM_B1) * gv
            v2 = ADAM_B2 * v_ref[...] + (1.0 - ADAM_B2) * (gv * gv)
            m_hat = m2 / (1.0 - ADAM_B1 ** ADAM_STEP)
            v_hat = v2 / (1.0 - ADAM_B2 ** ADAM_STEP)
            d_ref[...] = -ADAM_LR * (m_hat / (jnp.sqrt(v_hat) + ADAM_EPS) + ADAM_WD * w_ref[...])
            mo_ref[...] = m2
            vo_ref[...] = v2

    specs = [pl.BlockSpec((1, w.shape[1] // steps, w.shape[2]), lambda j, r: (j, r, 0)) for w in ws]
    shapes = [jax.ShapeDtypeStruct(w.shape, F32) for w in ws]
    out = pl.pallas_call(
        body, name=name, grid=(ws[0].shape[0], steps), in_specs=specs * 4, out_specs=specs * 3, out_shape=shapes * 3,
        compiler_params=pltpu.CompilerParams(dimension_semantics=("parallel", "parallel"), vmem_limit_bytes=VMEM_LIMIT_BYTES),
    )(*ws, *gs, *ms, *vs)
    return out[:n], out[n:2 * n], out[2 * n:]


def kernel(x, norm_g, attn_w_in, attn_sinks, attn_w_out, pool_w_in, pool_w_mix, pool_scale, pool_w_out, final_g, loss_target, m_norm_g, m_attn_w_in, m_attn_sinks, m_attn_w_out, m_pool_w_in, m_pool_w_mix, m_pool_scale, m_pool_w_out, m_final_g, v_norm_g, v_attn_w_in, v_attn_sinks, v_attn_w_out, v_pool_w_in, v_pool_w_mix, v_pool_scale, v_pool_w_out, v_final_g):
    cx, cy, cc = _mesh_position()
    chip = 2 * cx + cy

    core = cc.astype(jnp.int32).reshape(1)

    def blocks_2d(a_in, a_out, p_in, p_mix, p_out):
        return [a_in, a_out, p_in, p_mix.reshape(2, POOL_GC, POOL_GC), p_out]

    w_blocks = blocks_2d(attn_w_in, attn_w_out, pool_w_in, pool_w_mix, pool_w_out)
    attn_in_w, attn_out_w, pool_in_w, pool_mix_w, pool_out_w, scale_full = [], [], [], [], [], []
    for j in range(2):
        own = [w[j].astype(BF16) for w in w_blocks] + [pool_scale[j][None, :]]
        a_in, a_out, p_in, p_mix, p_out, scale = (
            lax.dynamic_update_slice(others, mine[None], (chip, 0, 0))
            for others, mine in zip(gather_weights(f"gather_weights_{j}", own, whole=(len(BIG),)), own))
        full = a_in.transpose(1, 0, 2).reshape(D_MODEL, N_CHIPS * ATTN_CHUNK)
        wkv = full[:, D_MODEL:D_MODEL + 2 * KV_WIDTH]
        attn_in_w.append((full[:, :D_MODEL].T, wkv.T, full[:, D_MODEL + 2 * KV_WIDTH:].T, wkv))
        attn_out_w.append(a_out.reshape(D_MODEL, D_MODEL))
        pool_in_w.append(p_in)
        pool_mix_w.append(p_mix)
        pool_out_w.append(p_out.reshape(D_MODEL, D_MODEL))
        scale_full.append(scale.reshape(1, D_MODEL))

    sq, grad_x, g = local_step(x[0], loss_target[0], norm_g, attn_sinks, final_g, attn_in_w, attn_out_w, pool_in_w,
                               pool_mix_w, scale_full, pool_out_w)

    last = jnp.concatenate([g["attn_sinks"][0], g["attn_sinks"][1], jnp.sum(sq).reshape(1),
                            jnp.zeros((D_MODEL - 2 * N_HEADS - 1,), F32)])[None, :]
    small = jnp.concatenate(g["norm_g"] + [g["final_g"]] + g["pool_scale"] + [last], axis=0)
    small = all_reduce_small(small)
    loss = 0.5 * small[7, 2 * N_HEADS] / D_MODEL
    g_norm = small[0:4]
    g_final = small[4]
    g_scale = lax.dynamic_slice(small[5:7], (0, chip * 256), (2, 256))
    g_sinks = small[7, :2 * N_HEADS].reshape(2, N_HEADS)

    reduced = []
    for j in range(2):
        mine = [g[name][j] for name in BIG]
        from_sibling = sibling_swap_halves(f"sibling_swap_halves_{j}", mine)
        parts = chip_sum(f"chip_sum_{j}", mine, from_sibling, core)
        landed = scatter_to_chips(f"scatter_to_chips_{j}", parts)
        reduced += sum_chips(f"sum_chips_{j}", parts, landed, chip.astype(jnp.int32).reshape(1))
    from_sibling = sibling_swap_reduced("sibling_swap_reduced", reduced)
    whole = [jnp.where(cc == 0, jnp.concatenate([mine, theirs]), jnp.concatenate([theirs, mine]))
             for mine, theirs in zip(reduced, from_sibling)]
    g_blocks = [jnp.stack([whole[t], whole[len(BIG) + t]]) for t in range(len(BIG))]
    g_blocks[0] = g_blocks[0].transpose(0, 2, 1)

    shapes = [w.shape for w in (attn_w_in, attn_w_out, pool_w_in, pool_w_mix, pool_w_out)]
    d_blocks, m_blocks, v_blocks = adamw(
        "adamw", w_blocks, g_blocks, blocks_2d(m_attn_w_in, m_attn_w_out, m_pool_w_in, m_pool_w_mix, m_pool_w_out),
        blocks_2d(v_attn_w_in, v_attn_w_out, v_pool_w_in, v_pool_w_mix, v_pool_w_out), steps=4)
    g_big, d_big, m_big, v_big = (
        {name: b.reshape(s) for name, b, s in zip(BIG, blocks, shapes)} for blocks in (g_blocks, d_blocks, m_blocks, v_blocks))

    def small_pack(ng, fg, sc, sk):
        row7 = jnp.concatenate([sk.reshape(2 * N_HEADS), jnp.zeros((D_MODEL - 2 * N_HEADS,), F32)])[None, :]
        sc = jnp.concatenate([sc, jnp.zeros((2, D_MODEL - 256), F32)], axis=1)
        return jnp.concatenate([ng, fg[None, :], sc, row7], axis=0)

    packs = [small_pack(norm_g, final_g, pool_scale, attn_sinks), small_pack(g_norm, g_final, g_scale, g_sinks),
             small_pack(m_norm_g, m_final_g, m_pool_scale, m_attn_sinks), small_pack(v_norm_g, v_final_g, v_pool_scale, v_attn_sinks)]
    small_out = [out[0] for out in adamw("adamw_small", *[[p[None]] for p in packs], steps=1)]

    def small_unpack(t):
        t = t[0]
        return {"norm_g": t[0:4], "final_g": t[4], "pool_scale": t[5:7, :256], "attn_sinks": t[7, :2 * N_HEADS].reshape(2, N_HEADS)}

    d_small, m_small, v_small = (small_unpack(t) for t in small_out)
    g_small = {"norm_g": g_norm, "final_g": g_final, "pool_scale": g_scale, "attn_sinks": g_sinks}

    names = ["norm_g", "attn_w_in", "attn_sinks", "attn_w_out", "pool_w_in", "pool_w_mix", "pool_scale", "pool_w_out", "final_g"]

    def ordered(bigs, smalls):
        return [bigs[n] if n in bigs else smalls[n] for n in names]

    return (loss, grad_x[None], *ordered(g_big, g_small), *ordered(d_big, d_small), *ordered(m_big, m_small),
            *ordered(v_big, v_small))
```

```python
import functools
import math

import jax
import jax.numpy as jnp
from jax import lax
from jax.experimental import pallas as pl
from jax.experimental.pallas import tpu as pltpu

F32 = jnp.float32
BF16 = jnp.bfloat16

D_MODEL = 1024
N_HEADS = 16
N_KV_HEADS = 4
GROUP = N_HEADS // N_KV_HEADS
HEAD_DIM = 64
KV_WIDTH = N_KV_HEADS * HEAD_DIM
ATTN_BLOCK = 128
POOL_WINDOWS = (2, 4, 8, 16)
POOL_GC = 256
POOL_HALO = 16
EPS = 1e-6
N_CHIPS = 4
ATTN_CHUNK = 640
POOL_CHUNK = 512
ROW_TILE = 512
NEG_BIG = -1e30
VMEM_LIMIT_BYTES = 60 * 1024 * 1024

ADAM_LR = 0.001
ADAM_B1 = 0.9
ADAM_B2 = 0.999
ADAM_EPS = 1e-08
ADAM_WD = 0.01
ADAM_STEP = 10

MESH_ID = pl.DeviceIdType.MESH


def _dot(a, b):
    return jnp.dot(a, b, preferred_element_type=F32)


def _dot_nt(a, b):
    return lax.dot_general(a, b, (((1,), (1,)), ((), ())), preferred_element_type=F32)


def _dot_tn(a, b):
    return lax.dot_general(a, b, (((0,), (0,)), ((), ())), preferred_element_type=F32)


def _vmem():
    return pl.BlockSpec(memory_space=pltpu.VMEM)


def _rows(width, tile=ROW_TILE):
    return pl.BlockSpec((tile, width), lambda i: (i, 0))


def _params(semantics):
    return pltpu.CompilerParams(dimension_semantics=(semantics,), vmem_limit_bytes=VMEM_LIMIT_BYTES)


def _grid_call(name, body, tiles, in_specs, out_specs, out_shape, scratch_shapes, args, semantics, carry=None):
    if carry is None:
        return pl.pallas_call(body, name=name, grid=(tiles,), in_specs=in_specs, out_specs=out_specs, out_shape=out_shape,
                              scratch_shapes=scratch_shapes, compiler_params=_params(semantics))(*args)
    counts = [len(args), len(carry.inputs), len(out_shape), len(carry.out_shapes), len(scratch_shapes), 2]

    def wrapped(*refs):
        groups, at = [], 0
        for n in counts:
            groups.append(refs[at:at + n])
            at += n
        ins, c_in, outs, c_out, scratch, sems = groups
        step = pl.program_id(0)
        pl.when(step == 0)(lambda: carry.start(c_in, c_out, sems))
        if carry.mid is not None:
            pl.when(step == tiles // 2)(lambda: carry.mid(c_in, c_out, sems))
        body(*ins, *outs, *scratch)
        pl.when(step == tiles - 1)(lambda: carry.finish(c_in, c_out, sems))

    results = pl.pallas_call(
        wrapped, name=name, grid=(tiles,),
        in_specs=list(in_specs) + [_any()] * len(carry.inputs),
        out_specs=list(out_specs) + [_any()] * len(carry.out_shapes),
        out_shape=list(out_shape) + carry.out_shapes,
        scratch_shapes=list(scratch_shapes) + [pltpu.SemaphoreType.DMA((carry.n_remote,))] * 2,
        compiler_params=_params("arbitrary"))(*args, *carry.inputs)
    return results[:len(out_shape)], results[len(out_shape):]


def _norm(xf, g):
    rstd = lax.rsqrt(jnp.mean(xf * xf, axis=-1, keepdims=True) + EPS)
    xhat = xf * rstd
    return xhat, rstd, xhat * g


def _norm_bwd(dh, xhat, rstd, g):
    dg = jnp.sum(dh * xhat, axis=0, keepdims=True)
    dxhat = dh * g
    dx = rstd * (dxhat - xhat * jnp.mean(dxhat * xhat, axis=-1, keepdims=True))
    return dx, dg


def _silu_parts(zf):
    sig = jax.nn.sigmoid(zf)
    return zf * sig, sig * (1.0 + zf * (1.0 - sig))


def _cols(height, tile=ROW_TILE):
    return pl.BlockSpec((height, tile), lambda i: (0, i))


def _halo_prev_rows(width):
    per_tile = ROW_TILE // ATTN_BLOCK
    return pl.BlockSpec((ATTN_BLOCK, width), lambda i: (jnp.maximum(i * per_tile - 1, 0), 0))


def _halo_prev_cols(height):
    per_tile = ROW_TILE // ATTN_BLOCK
    return pl.BlockSpec((height, ATTN_BLOCK), lambda i: (0, jnp.maximum(i * per_tile - 1, 0)))


def attn_in_fwd(x, g, wq_t, wkv_t, wz_t, wkv):
    T = x.shape[0]

    def body(x_ref, g_ref, wq_ref, wkvt_ref, wz_ref, wkv_ref, qt_ref, kv_ref, kvt_ref, zt_ref):
        _, _, h = _norm(x_ref[...], g_ref[...])
        h = h.astype(BF16)
        scale = 1.0 / math.sqrt(HEAD_DIM)
        qt_ref[...] = (_dot_nt(wq_ref[...], h) * scale).astype(BF16)
        zt_ref[...] = _dot_nt(wz_ref[...], h).astype(BF16)
        kvt_ref[...] = _dot_nt(wkvt_ref[...], h).astype(BF16)
        kv_ref[...] = _dot(h, wkv_ref[...]).astype(BF16)

    return pl.pallas_call(
        body,
        name="attn_in_fwd",
        grid=(T // ROW_TILE,),
        in_specs=[_rows(D_MODEL), _vmem(), _vmem(), _vmem(), _vmem(), _vmem()],
        out_specs=[_cols(D_MODEL), _rows(2 * KV_WIDTH), _cols(2 * KV_WIDTH), _cols(D_MODEL)],
        out_shape=[
            jax.ShapeDtypeStruct((D_MODEL, T), BF16),
            jax.ShapeDtypeStruct((T, 2 * KV_WIDTH), BF16),
            jax.ShapeDtypeStruct((2 * KV_WIDTH, T), BF16),
            jax.ShapeDtypeStruct((D_MODEL, T), BF16),
        ],
        compiler_params=_params("parallel"),
    )(x, g, wq_t, wkv_t, wz_t, wkv)


def _causal_triangle():
    shape = (ATTN_BLOCK, GROUP * ATTN_BLOCK)
    kj = lax.broadcasted_iota(jnp.int32, shape, 0)
    qi = lax.broadcasted_iota(jnp.int32, shape, 1) & (ATTN_BLOCK - 1)
    return kj <= qi


def _group_cols(ref, hk, cols):
    return jnp.concatenate(
        [ref[(hk * GROUP + gi) * HEAD_DIM:(hk * GROUP + gi + 1) * HEAD_DIM, cols] for gi in range(GROUP)], axis=1)


def _group_row(values):
    return jnp.concatenate(values, axis=1)


def _block_pair(ref, halo_ref, b, lanes):
    cur = ref[b * ATTN_BLOCK:(b + 1) * ATTN_BLOCK, lanes]
    prev = halo_ref[:, lanes] if b == 0 else ref[(b - 1) * ATTN_BLOCK:b * ATTN_BLOCK, lanes]
    return cur, prev


def _block_pair_t(ref, halo_ref, b, rows):
    cur = ref[rows, b * ATTN_BLOCK:(b + 1) * ATTN_BLOCK]
    prev = halo_ref[rows, :] if b == 0 else ref[rows, (b - 1) * ATTN_BLOCK:b * ATTN_BLOCK]
    return cur, prev


def _window_scores(kc, kp, qt, tri, b, first_penalty):
    sp = _dot(kp, qt)
    if b == 0:
        sp = sp + first_penalty
    return jnp.where(tri, _dot(kc, qt), sp)


def attn_core_fwd(q_t, kv, kv_t, z_t, x, w_out, sinks, carry=None):
    T = x.shape[0]
    blocks = ROW_TILE // ATTN_BLOCK

    def body(qt_ref, kv_ref, kvh_ref, kvt_ref, kvth_ref, zt_ref, x_ref, w_ref, sink_ref, xo_ref, ot_ref, lse_ref, oacc):
        tri = _causal_triangle()
        first_penalty = jnp.where(pl.program_id(0) == 0, NEG_BIG, 0.0)
        for hk in range(N_KV_HEADS):
            heads = [hk * GROUP + gi for gi in range(GROUP)]
            sink = _group_row([jnp.full((1, ATTN_BLOCK), sink_ref[h], F32) for h in heads])
            k_lanes = slice(hk * HEAD_DIM, (hk + 1) * HEAD_DIM)
            v_rows = slice(KV_WIDTH + hk * HEAD_DIM, KV_WIDTH + (hk + 1) * HEAD_DIM)
            for b in range(blocks):
                cols = slice(b * ATTN_BLOCK, (b + 1) * ATTN_BLOCK)
                qt = _group_cols(qt_ref, hk, cols)
                kc, kp = _block_pair(kv_ref, kvh_ref, b, k_lanes)
                s = _window_scores(kc, kp, qt, tri, b, first_penalty)
                m = jnp.maximum(jnp.max(s, axis=0, keepdims=True), sink)
                p = jnp.exp(s - m)
                denom = jnp.sum(p, axis=0, keepdims=True) + jnp.exp(sink - m)
                pb = p.astype(BF16)
                zero = jnp.zeros_like(pb)
                vtc, vtp = _block_pair_t(kvt_ref, kvth_ref, b, v_rows)
                o = (_dot(vtc, jnp.where(tri, pb, zero)) + _dot(vtp, jnp.where(tri, zero, pb))) * (1.0 / denom)
                lse = m + jnp.log(denom)
                for gi, h in enumerate(heads):
                    part = slice(gi * ATTN_BLOCK, (gi + 1) * ATTN_BLOCK)
                    oacc[h * HEAD_DIM:(h + 1) * HEAD_DIM, cols] = o[:, part]
                    lse_ref[h:h + 1, cols] = lse[:, part]
        of = oacc[...]
        silu, _ = _silu_parts(zt_ref[...].astype(F32))
        y = _dot_tn((of * silu).astype(BF16), w_ref[...])
        xo_ref[...] = x_ref[...] + y
        ot_ref[...] = of.astype(BF16)

    return _grid_call(
        "attn_core_fwd", body, T // ROW_TILE,
        in_specs=[_cols(D_MODEL), _rows(2 * KV_WIDTH), _halo_prev_rows(2 * KV_WIDTH), _cols(2 * KV_WIDTH),
                  _halo_prev_cols(2 * KV_WIDTH), _cols(D_MODEL), _rows(D_MODEL), _vmem(),
                  pl.BlockSpec(memory_space=pltpu.SMEM)],
        out_specs=[_rows(D_MODEL), _cols(D_MODEL), _cols(N_HEADS)],
        out_shape=[
            jax.ShapeDtypeStruct((T, D_MODEL), F32),
            jax.ShapeDtypeStruct((D_MODEL, T), BF16),
            jax.ShapeDtypeStruct((N_HEADS, T), F32),
        ],
        scratch_shapes=[pltpu.VMEM((D_MODEL, ROW_TILE), F32)],
        args=(q_t, kv, kv, kv_t, kv_t, z_t, x, w_out, sinks), semantics="parallel", carry=carry)


def _inv_count(first_row, rows, window):
    t = first_row + lax.broadcasted_iota(jnp.int32, (rows, 1), 0)
    return 1.0 / jnp.minimum(t + 1, window).astype(F32)


MIX_ROWS = POOL_GC // N_CHIPS


def _mix_groups(wmix_ref):
    return [jnp.concatenate([wmix_ref[k, gi * MIX_ROWS:(gi + 1) * MIX_ROWS, :] for k in range(N_CHIPS)], axis=0)
            for gi in range(len(POOL_WINDOWS))]


def pool_fwd(x, g, w_in, w_mix, scale, w_out, carry=None):
    T = x.shape[0]

    def body(x_ref, g_ref, win_ref, wmix_ref, scale_ref, wout_ref, xo_ref, p_ref, z_ref, carry):
        i = pl.program_id(0)
        wmix = _mix_groups(wmix_ref)

        @pl.when(i == 0)
        def _():
            carry[...] = jnp.zeros_like(carry)

        _, _, h = _norm(x_ref[...], g_ref[...])
        h = h.astype(BF16)
        u = jnp.concatenate([_dot(h, win_ref[0]), _dot(h, win_ref[1])], axis=1)
        z = jnp.concatenate([_dot(h, win_ref[2]), _dot(h, win_ref[3])], axis=1)
        ext = jnp.concatenate([carry[...], u], axis=0)
        carry[...] = u[ROW_TILE - POOL_HALO:]
        mixed = []
        for gi, window in enumerate(POOL_WINDOWS):
            cols = slice(gi * POOL_GC, (gi + 1) * POOL_GC)
            s = ext[:, cols]
            shift = 1
            while shift < window:
                s = s + pltpu.roll(s, shift, 0)
                shift *= 2
            p = s[POOL_HALO:] * _inv_count(i * ROW_TILE, ROW_TILE, window) - u[:, cols]
            p = p.astype(BF16)
            p_ref[:, cols] = p
            mixed.append(_dot(p, wmix[gi]))
        m = jnp.concatenate(mixed, axis=1) * scale_ref[...]
        silu, _ = _silu_parts(z)
        y = _dot((m * silu).astype(BF16), wout_ref[...])
        xo_ref[...] = x_ref[...] + y
        z_ref[...] = z.astype(BF16)

    return _grid_call(
        "pool_fwd", body, T // ROW_TILE,
        in_specs=[_rows(D_MODEL), _vmem(), _vmem(), _vmem(), _vmem(), _vmem()],
        out_specs=[_rows(D_MODEL), _rows(D_MODEL), _rows(D_MODEL)],
        out_shape=[
            jax.ShapeDtypeStruct((T, D_MODEL), F32),
            jax.ShapeDtypeStruct((T, D_MODEL), BF16),
            jax.ShapeDtypeStruct((T, D_MODEL), BF16),
        ],
        scratch_shapes=[pltpu.VMEM((POOL_HALO, D_MODEL), F32)],
        args=(x, g, w_in, w_mix, scale, w_out), semantics="arbitrary", carry=carry)


def loss_head(x, g, target):
    T = x.shape[0]

    def body(x_ref, g_ref, t_ref, dx_ref, sq_ref, dg_ref):
        @pl.when(pl.program_id(0) == 0)
        def _():
            sq_ref[...] = jnp.zeros_like(sq_ref)
            dg_ref[...] = jnp.zeros_like(dg_ref)

        gv = g_ref[...]
        xhat, rstd, out = _norm(x_ref[...], gv)
        err = out - t_ref[...]
        sq_ref[...] += jnp.sum(err * err, axis=0, keepdims=True)
        dx, dg = _norm_bwd(err * (1.0 / D_MODEL), xhat, rstd, gv)
        dx_ref[...] = dx
        dg_ref[...] += dg

    return pl.pallas_call(
        body,
        name="loss_head",
        grid=(T // ROW_TILE,),
        in_specs=[_rows(D_MODEL), _vmem(), _rows(D_MODEL)],
        out_specs=[_rows(D_MODEL), _vmem(), _vmem()],
        out_shape=[
            jax.ShapeDtypeStruct((T, D_MODEL), F32),
            jax.ShapeDtypeStruct((1, D_MODEL), F32),
            jax.ShapeDtypeStruct((1, D_MODEL), F32),
        ],
        compiler_params=_params("arbitrary"),
    )(x, g, target)


def pool_bwd_mix(dy, p, z, w_mix, scale, w_out):
    T = dy.shape[0]
    tiles = T // ROW_TILE

    def body(dy_ref, p_ref, z_ref, wmix_ref, scale_ref, wout_ref, dp_ref, dz_ref, dwout_out, dwmix_out, dscale_ref,
             dwout_ref, dwmix_ref):
        @pl.when(pl.program_id(0) == 0)
        def _():
            dwout_ref[...] = jnp.zeros_like(dwout_ref)
            dwmix_ref[...] = jnp.zeros_like(dwmix_ref)
            dscale_ref[...] = jnp.zeros_like(dscale_ref)

        wmix = _mix_groups(wmix_ref)
        dyb = dy_ref[...].astype(BF16)
        da = _dot_nt(dyb, wout_ref[...])
        m_pre = jnp.concatenate(
            [_dot(p_ref[:, gi * POOL_GC:(gi + 1) * POOL_GC], wmix[gi]) for gi in range(len(POOL_WINDOWS))], axis=1)
        sc = scale_ref[...]
        m = m_pre * sc
        zf = z_ref[...].astype(F32)
        silu, dsilu = _silu_parts(zf)
        dwout_ref[...] += _dot_tn((m * silu).astype(BF16), dyb)
        dm = da * silu
        dz_ref[...] = (da * m * dsilu).astype(BF16)
        dscale_ref[...] += jnp.sum(dm * m_pre, axis=0, keepdims=True)
        dmp = (dm * sc).astype(BF16)
        for gi in range(len(POOL_WINDOWS)):
            cols = slice(gi * POOL_GC, (gi + 1) * POOL_GC)
            dw = _dot_tn(p_ref[:, cols], dmp[:, cols])
            for k in range(N_CHIPS):
                dwmix_ref[k, gi * MIX_ROWS:(gi + 1) * MIX_ROWS, :] += dw[k * MIX_ROWS:(k + 1) * MIX_ROWS]
            dp_ref[:, cols] = _dot_nt(dmp[:, cols], wmix[gi]).astype(BF16)

        @pl.when(pl.program_id(0) == tiles - 1)
        def _():
            dwout_out[...] = dwout_ref[...].astype(BF16)
            dwmix_out[...] = dwmix_ref[...].astype(BF16)

    return pl.pallas_call(
        body,
        name="pool_bwd_mix",
        grid=(tiles,),
        in_specs=[_rows(D_MODEL), _rows(D_MODEL), _rows(D_MODEL), _vmem(), _vmem(), _vmem()],
        out_specs=[_rows(D_MODEL), _rows(D_MODEL), _vmem(), _vmem(), _vmem()],
        out_shape=[
            jax.ShapeDtypeStruct((T, D_MODEL), BF16),
            jax.ShapeDtypeStruct((T, D_MODEL), BF16),
            jax.ShapeDtypeStruct((D_MODEL, D_MODEL), BF16),
            jax.ShapeDtypeStruct((N_CHIPS, POOL_GC, POOL_GC), BF16),
            jax.ShapeDtypeStruct((1, D_MODEL), F32),
        ],
        scratch_shapes=[pltpu.VMEM((D_MODEL, D_MODEL), F32), pltpu.VMEM((N_CHIPS, POOL_GC, POOL_GC), F32)],
        compiler_params=_params("arbitrary"),
    )(dy, p, z, w_mix, scale, w_out)


def pool_bwd_in(dp, dz, x, dy, g, w_in, carry=None):
    T = x.shape[0]
    tiles = T // ROW_TILE
    halo_blocks = ROW_TILE // POOL_HALO
    last_halo = T // POOL_HALO - 1

    def body(dp_ref, dph_ref, dz_ref, x_ref, dy_ref, g_ref, win_ref, dx_ref, dwin_out, dg_ref, dwin_ref):
        i = pl.program_id(0)

        @pl.when(i == 0)
        def _():
            dwin_ref[...] = jnp.zeros_like(dwin_ref)
            dg_ref[...] = jnp.zeros_like(dg_ref)

        rows = ROW_TILE + POOL_HALO
        ext = jnp.concatenate([dp_ref[...], dph_ref[...]], axis=0).astype(F32)
        t = i * ROW_TILE + lax.broadcasted_iota(jnp.int32, (rows, 1), 0)
        inside = (t < T).astype(F32)
        du = []
        for gi, window in enumerate(POOL_WINDOWS):
            cols = slice(gi * POOL_GC, (gi + 1) * POOL_GC)
            s = ext[:, cols] * (_inv_count(i * ROW_TILE, rows, window) * inside)
            shift = 1
            while shift < window:
                s = s + pltpu.roll(s, rows - shift, 0)
                shift *= 2
            du.append(s[:ROW_TILE] - ext[:ROW_TILE, cols])
        du = jnp.concatenate(du, axis=1).astype(BF16)
        chunks = [du[:, :POOL_CHUNK], du[:, POOL_CHUNK:], dz_ref[:, :POOL_CHUNK], dz_ref[:, POOL_CHUNK:]]
        gv = g_ref[...]
        xhat, rstd, h = _norm(x_ref[...], gv)
        h = h.astype(BF16)
        dh = jnp.zeros((ROW_TILE, D_MODEL), F32)
        for c in range(N_CHIPS):
            dwin_ref[c] += _dot_tn(h, chunks[c])
            dh = dh + _dot_nt(chunks[c], win_ref[c])
        dx, dg = _norm_bwd(dh, xhat, rstd, gv)
        dx_ref[...] = dx + dy_ref[...]
        dg_ref[...] += dg

        @pl.when(i == tiles - 1)
        def _():
            dwin_out[...] = dwin_ref[...].astype(BF16)

    return _grid_call(
        "pool_bwd_in", body, tiles,
        in_specs=[_rows(D_MODEL),
                  pl.BlockSpec((POOL_HALO, D_MODEL), lambda i: (jnp.minimum((i + 1) * halo_blocks, last_halo), 0)),
                  _rows(D_MODEL), _rows(D_MODEL), _rows(D_MODEL), _vmem(), _vmem()],
        out_specs=[_rows(D_MODEL), _vmem(), _vmem()],
        out_shape=[
            jax.ShapeDtypeStruct((T, D_MODEL), F32),
            jax.ShapeDtypeStruct((N_CHIPS, D_MODEL, POOL_CHUNK), BF16),
            jax.ShapeDtypeStruct((1, D_MODEL), F32),
        ],
        scratch_shapes=[pltpu.VMEM((N_CHIPS, D_MODEL, POOL_CHUNK), F32)],
        args=(dp, dp, dz, x, dy, g, w_in), semantics="arbitrary", carry=carry)


def attn_bwd_core(dy, q_t, kv, kv_t, z_t, o_t, lse, w_out, sinks, carry=None):
    T = dy.shape[0]
    tiles = T // ROW_TILE
    blocks = ROW_TILE // ATTN_BLOCK

    def body(dy_ref, qt_ref, kv_ref, kvh_ref, kvt_ref, kvth_ref, zt_ref, ot_ref, lse_ref, w_ref, sink_ref,
             dqt_ref, dkv_ref, dkvh_ref, dzt_ref, dwout_out, dsink_ref, do_s, dkv_s, dwout_ref):
        @pl.when(pl.program_id(0) == 0)
        def _():
            dwout_ref[...] = jnp.zeros_like(dwout_ref)
            dsink_ref[...] = jnp.zeros_like(dsink_ref)

        dyb = dy_ref[...].astype(BF16)
        da = _dot_nt(w_ref[...], dyb)
        of = ot_ref[...].astype(F32)
        silu, dsilu = _silu_parts(zt_ref[...].astype(F32))
        dwout_ref[...] += _dot((of * silu).astype(BF16), dyb)
        do = da * silu
        dzt_ref[...] = (da * of * dsilu).astype(BF16)
        do_s[...] = do.astype(BF16)
        dof = do * of
        dkv_s[...] = jnp.zeros_like(dkv_s)
        tri = _causal_triangle()
        first_penalty = jnp.where(pl.program_id(0) == 0, NEG_BIG, 0.0)
        for hk in range(N_KV_HEADS):
            heads = [hk * GROUP + gi for gi in range(GROUP)]
            sink = _group_row([jnp.full((1, ATTN_BLOCK), sink_ref[h], F32) for h in heads])
            deltas = [jnp.sum(dof[h * HEAD_DIM:(h + 1) * HEAD_DIM], axis=0, keepdims=True) for h in heads]
            k_lanes = slice(hk * HEAD_DIM, (hk + 1) * HEAD_DIM)
            v_lanes = slice(KV_WIDTH + hk * HEAD_DIM, KV_WIDTH + (hk + 1) * HEAD_DIM)
            for b in range(blocks):
                cols = slice(b * ATTN_BLOCK, (b + 1) * ATTN_BLOCK)
                cur_rows = slice((b + 1) * ATTN_BLOCK, (b + 2) * ATTN_BLOCK)
                prev_rows = slice(b * ATTN_BLOCK, (b + 1) * ATTN_BLOCK)
                qt = _group_cols(qt_ref, hk, cols)
                dot = _group_cols(do_s, hk, cols)
                lse_row = _group_row([lse_ref[h:h + 1, cols] for h in heads])
                delta = _group_row([d[:, cols] for d in deltas])
                kc, kp = _block_pair(kv_ref, kvh_ref, b, k_lanes)
                vc, vp = _block_pair(kv_ref, kvh_ref, b, v_lanes)
                ktc, ktp = _block_pair_t(kvt_ref, kvth_ref, b, k_lanes)
                p = jnp.exp(_window_scores(kc, kp, qt, tri, b, first_penalty) - lse_row)
                dp = jnp.where(tri, _dot(vc, dot), _dot(vp, dot))
                ds = (p * (dp - delta)).astype(BF16)
                pb = p.astype(BF16)
                zero = jnp.zeros_like(pb)
                ds_c, ds_p = jnp.where(tri, ds, zero), jnp.where(tri, zero, ds)
                dq = (_dot(ktc, ds_c) + _dot(ktp, ds_p)) * (1.0 / math.sqrt(HEAD_DIM))
                dkv_s[cur_rows, k_lanes] += _dot_nt(ds_c, qt)
                dkv_s[prev_rows, k_lanes] += _dot_nt(ds_p, qt)
                dkv_s[cur_rows, v_lanes] += _dot_nt(jnp.where(tri, pb, zero), dot)
                dkv_s[prev_rows, v_lanes] += _dot_nt(jnp.where(tri, zero, pb), dot)
                dsink = -jnp.exp(sink - lse_row) * delta
                for gi, h in enumerate(heads):
                    part = slice(gi * ATTN_BLOCK, (gi + 1) * ATTN_BLOCK)
                    dqt_ref[h * HEAD_DIM:(h + 1) * HEAD_DIM, cols] = dq[:, part].astype(BF16)
                    dsink_ref[0:1, h:h + 1] += jnp.sum(dsink[:, part], axis=1, keepdims=True)
        dkv_ref[...] = dkv_s[ATTN_BLOCK:]
        dkvh_ref[0] = dkv_s[:ATTN_BLOCK]

        @pl.when(pl.program_id(0) == tiles - 1)
        def _():
            dwout_out[...] = dwout_ref[...].astype(BF16)

    return _grid_call(
        "attn_bwd_core", body, tiles,
        in_specs=[_rows(D_MODEL), _cols(D_MODEL), _rows(2 * KV_WIDTH), _halo_prev_rows(2 * KV_WIDTH),
                  _cols(2 * KV_WIDTH), _halo_prev_cols(2 * KV_WIDTH), _cols(D_MODEL), _cols(D_MODEL), _cols(N_HEADS),
                  _vmem(), pl.BlockSpec(memory_space=pltpu.SMEM)],
        out_specs=[_cols(D_MODEL), _rows(2 * KV_WIDTH), pl.BlockSpec((1, ATTN_BLOCK, 2 * KV_WIDTH), lambda i: (i, 0, 0)),
                   _cols(D_MODEL), _vmem(), _vmem()],
        out_shape=[
            jax.ShapeDtypeStruct((D_MODEL, T), BF16),
            jax.ShapeDtypeStruct((T, 2 * KV_WIDTH), F32),
            jax.ShapeDtypeStruct((tiles, ATTN_BLOCK, 2 * KV_WIDTH), F32),
            jax.ShapeDtypeStruct((D_MODEL, T), BF16),
            jax.ShapeDtypeStruct((D_MODEL, D_MODEL), BF16),
            jax.ShapeDtypeStruct((8, 128), F32),
        ],
        scratch_shapes=[
            pltpu.VMEM((D_MODEL, ROW_TILE), BF16),
            pltpu.VMEM((ROW_TILE + ATTN_BLOCK, 2 * KV_WIDTH), F32),
            pltpu.VMEM((D_MODEL, D_MODEL), F32),
        ],
        args=(dy, q_t, kv, kv, kv_t, kv_t, z_t, o_t, lse, w_out, sinks), semantics="arbitrary", carry=carry)


def attn_bwd_in(dq_t, dkv, dkv_halo, dz_t, x, dy, g, wq_t, wz_t, wkv):
    T = x.shape[0]
    tiles = T // ROW_TILE
    kv_at = D_MODEL
    z_at = D_MODEL + 2 * KV_WIDTH

    def body(dqt_ref, dkv_ref, dkvh_ref, dzt_ref, x_ref, dy_ref, g_ref, wq_ref, wz_ref, wkv_ref, dx_ref, dwt_out, dg_ref,
             dwt_ref):
        i = pl.program_id(0)

        @pl.when(i == 0)
        def _():
            dwt_ref[...] = jnp.zeros_like(dwt_ref)
            dg_ref[...] = jnp.zeros_like(dg_ref)

        more = (i < tiles - 1).astype(F32)
        tail = jnp.concatenate([jnp.zeros((ROW_TILE - ATTN_BLOCK, 2 * KV_WIDTH), F32), dkvh_ref[0] * more], axis=0)
        dkvb = (dkv_ref[...] + tail).astype(BF16)
        gv = g_ref[...]
        xhat, rstd, h = _norm(x_ref[...], gv)
        h = h.astype(BF16)
        dqt = dqt_ref[...]
        dzt = dzt_ref[...]
        dwt_ref[:kv_at] += _dot(dqt, h)
        dwt_ref[kv_at:z_at] += _dot_tn(dkvb, h)
        dwt_ref[z_at:] += _dot(dzt, h)
        dh = _dot_tn(dqt, wq_ref[...]) + _dot_tn(dzt, wz_ref[...]) + _dot_nt(dkvb, wkv_ref[...])
        dx, dg = _norm_bwd(dh, xhat, rstd, gv)
        dx_ref[...] = dx + dy_ref[...]
        dg_ref[...] += dg

        @pl.when(i == tiles - 1)
        def _():
            dwt_out[...] = dwt_ref[...].astype(BF16)

    halo_next = pl.BlockSpec((1, ATTN_BLOCK, 2 * KV_WIDTH), lambda i: (jnp.minimum(i + 1, tiles - 1), 0, 0))
    return pl.pallas_call(
        body,
        name="attn_bwd_in",
        grid=(tiles,),
        in_specs=[_cols(D_MODEL), _rows(2 * KV_WIDTH), halo_next, _cols(D_MODEL), _rows(D_MODEL), _rows(D_MODEL),
                  _vmem(), _vmem(), _vmem(), _vmem()],
        out_specs=[_rows(D_MODEL), _vmem(), _vmem()],
        out_shape=[
            jax.ShapeDtypeStruct((T, D_MODEL), F32),
            jax.ShapeDtypeStruct((N_CHIPS * ATTN_CHUNK, D_MODEL), BF16),
            jax.ShapeDtypeStruct((1, D_MODEL), F32),
        ],
        scratch_shapes=[pltpu.VMEM((N_CHIPS * ATTN_CHUNK, D_MODEL), F32)],
        compiler_params=_params("arbitrary"),
    )(dq_t, dkv, dkv_halo, dz_t, x, dy, g, wq_t, wz_t, wkv)


BIG = ("attn_w_in", "attn_w_out", "pool_w_in", "pool_w_mix", "pool_w_out")
SMALL_ROWS = 8


def _half(ref, h, axis=0):
    rows = ref.shape[axis] // 2
    index = (slice(None),) * axis + (pl.ds(h * rows, rows),)
    return ref.at[index]


def _mesh_position():
    return lax.axis_index("x"), lax.axis_index("y"), lax.axis_index("c")


def _other_chips(x, y):
    return [(1 - x, y), (x, 1 - y), (1 - x, 1 - y)]


def _any():
    return pl.BlockSpec(memory_space=pl.ANY)


def _remote(src, dst, sems, index, to):
    send_sems, recv_sems = sems
    return pltpu.make_async_remote_copy(src_ref=src, dst_ref=dst, send_sem=send_sems.at[index], recv_sem=recv_sems.at[index],
                                        device_id=to, device_id_type=MESH_ID)


class Exchange:
    def __init__(self, inputs, out_shapes, n_remote, start, finish, mid=None):
        self.inputs, self.out_shapes, self.n_remote = list(inputs), list(out_shapes), n_remote
        self.start, self.finish, self.mid = start, finish, mid


def _run_exchange(name, ex):
    n_in, n_out = len(ex.inputs), len(ex.out_shapes)

    def kernel_body(*refs):
        ins, outs, sems = refs[:n_in], refs[n_in:n_in + n_out], refs[n_in + n_out:]
        ex.start(ins, outs, sems)
        if ex.mid is not None:
            ex.mid(ins, outs, sems)
        ex.finish(ins, outs, sems)

    return pl.pallas_call(
        kernel_body,
        name=name,
        in_specs=[_any()] * n_in,
        out_specs=[_any()] * n_out,
        out_shape=ex.out_shapes,
        scratch_shapes=[pltpu.SemaphoreType.DMA((ex.n_remote,))] * 2,
    )(*ex.inputs)


def gather_exchange(blocks, whole):
    def copies(ins, outs, sems):
        x, y, c = _mesh_position()
        me = 2 * x + y
        sibling = (x, y, 1 - c)
        table = []
        for t, (w_ref, out_ref) in enumerate(zip(ins, outs)):
            for j, (px, py) in enumerate(_other_chips(x, y)):
                peer = 2 * px + py
                if t in whole:
                    table.append((_remote(w_ref, out_ref.at[me], sems, 6 * t + j, (px, py, c)),
                                  _remote(w_ref, out_ref.at[peer], sems, 6 * t + j, (px, py, c)), None, None))
                    continue
                landed = _half(out_ref.at[peer], c)
                from_sibling = _half(out_ref.at[peer], 1 - c)
                table.append((_remote(_half(w_ref, c), _half(out_ref.at[me], c), sems, 6 * t + j, (px, py, c)),
                              _remote(landed, landed, sems, 6 * t + j, (px, py, c)),
                              _remote(landed, landed, sems, 6 * t + 3 + j, sibling),
                              _remote(from_sibling, from_sibling, sems, 6 * t + 3 + j, sibling)))
        return table

    def start(ins, outs, sems):
        for send, _, _, _ in copies(ins, outs, sems):
            send.start()

    def mid(ins, outs, sems):
        for _, arrival, forward, _ in copies(ins, outs, sems):
            if forward is not None:
                arrival.wait_recv()
                forward.start()

    def finish(ins, outs, sems):
        for send, arrival, forward, from_sibling in copies(ins, outs, sems):
            send.wait_send()
            if forward is None:
                arrival.wait_recv()
            else:
                forward.wait_send()
                from_sibling.wait_recv()

    shapes = [jax.ShapeDtypeStruct((N_CHIPS,) + b.shape, b.dtype) for b in blocks]
    return Exchange(blocks, shapes, 6 * len(blocks), start, finish, mid)


N_SENDERS = 7


def reduce_exchange(grads):
    def copies(ins, outs, sems):
        send_sems, recv_sems = sems
        x, y, c = _mesh_position()
        me = 2 * x + y
        sends, arrivals = [], []
        for t, (g_ref, out_ref) in enumerate(zip(ins, outs)):
            base = N_SENDERS * t

            def copy(src, slot, send_index, to):
                return pltpu.make_async_remote_copy(
                    src_ref=src, dst_ref=out_ref.at[slot], send_sem=send_sems.at[base + send_index],
                    recv_sem=recv_sems.at[base + slot], device_id=to, device_id_type=MESH_ID)

            mine = _half(g_ref.at[me], c)
            sends.append(copy(_half(g_ref.at[me], 1 - c), 0, 0, (x, y, 1 - c)))
            arrivals.append(copy(mine, 0, 0, (x, y, 1 - c)))
            for j, (px, py) in enumerate(_other_chips(x, y)):
                for h in range(2):
                    sends.append(copy(_half(g_ref.at[2 * px + py], h), 1 + 2 * j + c, 1 + 2 * j + h, (px, py, h)))
                    arrivals.append(copy(mine, 1 + 2 * j + h, 1 + 2 * j + h, (px, py, h)))
        return sends, arrivals

    def start(ins, outs, sems):
        for cp in copies(ins, outs, sems)[0]:
            cp.start()

    def finish(ins, outs, sems):
        sends, arrivals = copies(ins, outs, sems)
        for cp in sends:
            cp.wait_send()
        for cp in arrivals:
            cp.wait_recv()

    shapes = [jax.ShapeDtypeStruct((N_SENDERS, g.shape[1] // 2, g.shape[2]), g.dtype) for g in grads]
    return Exchange(grads, shapes, N_SENDERS * len(grads), start, finish)


def sum_landed(name, grads, landed, chip, core):
    n = len(grads)
    steps = 4

    def body(where_ref, *refs):
        for g_ref, l_ref, o_ref in zip(refs[:n], refs[n:2 * n], refs[2 * n:]):
            total = g_ref[0, 0].astype(F32)
            for s in range(N_SENDERS):
                total = total + l_ref[s].astype(F32)
            o_ref[...] = total

    halves = [g.reshape(N_CHIPS, 2, g.shape[1] // 2, g.shape[2]) for g in grads]

    def rows(h):
        return h.shape[2] // steps

    return pl.pallas_call(
        body,
        name=name,
        grid_spec=pltpu.PrefetchScalarGridSpec(
            num_scalar_prefetch=1, grid=(steps,),
            in_specs=[pl.BlockSpec((1, 1, rows(h), h.shape[3]), lambda r, where_ref: (where_ref[0], where_ref[1], r, 0))
                      for h in halves]
            + [pl.BlockSpec((N_SENDERS, rows(h), h.shape[3]), lambda r, where_ref: (0, r, 0)) for h in halves],
            out_specs=[pl.BlockSpec((rows(h), h.shape[3]), lambda r, where_ref: (r, 0)) for h in halves]),
        out_shape=[jax.ShapeDtypeStruct(h.shape[2:], F32) for h in halves],
        compiler_params=_params("parallel"),
    )(jnp.stack([chip, core]).astype(jnp.int32), *halves, *landed)


def swap_exchange(halves):
    def copies(ins, outs, sems):
        x, y, c = _mesh_position()
        return [_remote(h_ref, out_ref, sems, i, (x, y, 1 - c)) for i, (h_ref, out_ref) in enumerate(zip(ins, outs))]

    def start(ins, outs, sems):
        for cp in copies(ins, outs, sems):
            cp.start()

    def finish(ins, outs, sems):
        for cp in copies(ins, outs, sems):
            cp.wait()

    return Exchange(halves, [jax.ShapeDtypeStruct(h.shape, h.dtype) for h in halves], len(halves), start, finish)


def all_reduce_small(v):
    n_dev = 8

    def body(v_ref, out_ref, slots, send_sems, recv_sems):
        x, y, c = _mesh_position()
        me = 4 * x + 2 * y + c
        slots[me] = v_ref[...]
        peers = []
        for r in range(1, n_dev):
            fx, fy, fc = (r >> 2) & 1, (r >> 1) & 1, r & 1
            peers.append((1 - x if fx else x, 1 - y if fy else y, 1 - c if fc else c))
        sends = [
            pltpu.make_async_remote_copy(
                src_ref=v_ref, dst_ref=slots.at[me], send_sem=send_sems.at[r], recv_sem=recv_sems.at[r],
                device_id=peer, device_id_type=MESH_ID)
            for r, peer in enumerate(peers)
        ]
        for cp in sends:
            cp.start()
        for r, (px, py, pc) in enumerate(peers):
            pltpu.make_async_remote_copy(
                src_ref=v_ref, dst_ref=slots.at[4 * px + 2 * py + pc], send_sem=send_sems.at[r], recv_sem=recv_sems.at[r],
                device_id=(px, py, pc), device_id_type=MESH_ID).wait_recv()
        for cp in sends:
            cp.wait_send()
        total = slots[0]
        for d in range(1, n_dev):
            total = total + slots[d]
        out_ref[...] = total

    return pl.pallas_call(
        body,
        name="all_reduce_small",
        in_specs=[_vmem()],
        out_specs=_vmem(),
        out_shape=jax.ShapeDtypeStruct(v.shape, v.dtype),
        scratch_shapes=[pltpu.VMEM((n_dev,) + v.shape, v.dtype), pltpu.SemaphoreType.DMA((n_dev - 1,)),
                        pltpu.SemaphoreType.DMA((n_dev - 1,))],
    )(v)


def adamw(name, ws, gs, ms, vs, steps):
    n = len(ws)

    def body(*refs):
        ins, outs = refs[:4 * n], refs[4 * n:]
        for t in range(n):
            w_ref, g_ref, m_ref, v_ref = (ins[q * n + t] for q in range(4))
            d_ref, mo_ref, vo_ref = (outs[q * n + t] for q in range(3))
            gv = g_ref[...]
            m2 = ADAM_B1 * m_ref[...] + (1.0 - ADAM_B1) * gv
            v2 = ADAM_B2 * v_ref[...] + (1.0 - ADAM_B2) * (gv * gv)
            m_hat = m2 / (1.0 - ADAM_B1 ** ADAM_STEP)
            v_hat = v2 / (1.0 - ADAM_B2 ** ADAM_STEP)
            d_ref[...] = -ADAM_LR * (m_hat / (jnp.sqrt(v_hat) + ADAM_EPS) + ADAM_WD * w_ref[...])
            mo_ref[...] = m2
            vo_ref[...] = v2

    specs = [pl.BlockSpec((1, w.shape[1] // steps, w.shape[2]), lambda j, r: (j, r, 0)) for w in ws]
    shapes = [jax.ShapeDtypeStruct(w.shape, F32) for w in ws]
    out = pl.pallas_call(
        body, name=name, grid=(ws[0].shape[0], steps), in_specs=specs * 4, out_specs=specs * 3, out_shape=shapes * 3,
        compiler_params=pltpu.CompilerParams(dimension_semantics=("parallel", "parallel"), vmem_limit_bytes=VMEM_LIMIT_BYTES),
    )(*ws, *gs, *ms, *vs)
    return out[:n], out[n:2 * n], out[2 * n:]


def kernel(x, norm_g, attn_w_in, attn_sinks, attn_w_out, pool_w_in, pool_w_mix, pool_scale, pool_w_out, final_g, loss_target, m_norm_g, m_attn_w_in, m_attn_sinks, m_attn_w_out, m_pool_w_in, m_pool_w_mix, m_pool_scale, m_pool_w_out, m_final_g, v_norm_g, v_attn_w_in, v_attn_sinks, v_attn_w_out, v_pool_w_in, v_pool_w_mix, v_pool_scale, v_pool_w_out, v_final_g):
    cx, cy, cc = _mesh_position()
    chip = 2 * cx + cy

    def blocks_2d(a_in, a_out, p_in, p_mix, p_out):
        return [a_in, a_out, p_in, p_mix.reshape(2, POOL_GC, POOL_GC), p_out]

    w_blocks = blocks_2d(attn_w_in, attn_w_out, pool_w_in, pool_w_mix, pool_w_out)

    def gather(layer):
        j = layer // 2
        if layer % 2 == 0:
            own = [w_blocks[t][j].astype(BF16) for t in (0, 1)]
            return own, gather_exchange(own, whole=())
        own = [w_blocks[t][j].astype(BF16) for t in (2, 3, 4)] + [pool_scale[j][None, :]]
        return own, gather_exchange(own, whole=(3,))

    def placed(own, others):
        return [lax.dynamic_update_slice(o, mine[None], (chip, 0, 0)) for o, mine in zip(others, own)]

    own, exchange = gather(0)
    weights = placed(own, _run_exchange("gather_layer_0", exchange))
    xs = x[0]
    saved = []
    for layer in range(4):
        g_row = norm_g[layer][None, :]
        own, exchange = gather(layer + 1) if layer < 3 else (None, None)
        if layer % 2 == 0:
            a_in, a_out = weights
            full = a_in.transpose(1, 0, 2).reshape(D_MODEL, N_CHIPS * ATTN_CHUNK)
            wkv = full[:, D_MODEL:D_MODEL + 2 * KV_WIDTH]
            wq_t, wz_t, w_out = full[:, :D_MODEL].T, full[:, D_MODEL + 2 * KV_WIDTH:].T, a_out.reshape(D_MODEL, D_MODEL)
            q_t, kv, kv_t, z_t = attn_in_fwd(xs, g_row, wq_t, wkv.T, wz_t, wkv)
            (x_new, o_t, lse), landed = attn_core_fwd(q_t, kv, kv_t, z_t, xs, w_out, attn_sinks[layer // 2], carry=exchange)
            saved.append((xs, g_row, q_t, kv, kv_t, z_t, o_t, lse, wq_t, wz_t, wkv, w_out))
        else:
            p_in, p_mix, p_out, scale = weights
            scale, w_out = scale.reshape(1, D_MODEL), p_out.reshape(D_MODEL, D_MODEL)
            result = pool_fwd(xs, g_row, p_in, p_mix, scale, w_out, carry=exchange)
            (x_new, p, z), landed = result if exchange is not None else (result, None)
            saved.append((xs, g_row, p, z, p_in, p_mix, scale, w_out))
        if exchange is not None:
            weights = placed(own, landed)
        xs = x_new

    dx, sq, d_final_g = loss_head(xs, final_g[None, :], loss_target[0])
    d_norm, d_scale, d_sinks, sent, pending = [None] * 4, [None] * 2, [None] * 2, {}, None
    for layer in reversed(range(4)):
        exchange = reduce_exchange(pending[1]) if pending is not None else None
        if layer % 2 == 0:
            x_in, g_row, q_t, kv, kv_t, z_t, o_t, lse, wq_t, wz_t, wkv, w_out = saved[layer]
            (dq_t, dkv, dkv_halo, dz_t, dwout, dsink), landed = attn_bwd_core(
                dx, q_t, kv, kv_t, z_t, o_t, lse, w_out, attn_sinks[layer // 2], carry=exchange)
            dx, dwin_t, d_norm[layer] = attn_bwd_in(dq_t, dkv, dkv_halo, dz_t, x_in, dx, g_row, wq_t, wz_t, wkv)
            d_sinks[layer // 2] = dsink[0, :N_HEADS]
            mine = [dwin_t.reshape(N_CHIPS, ATTN_CHUNK, D_MODEL), dwout.reshape(N_CHIPS, -1, D_MODEL)]
        else:
            x_in, g_row, p, z, p_in, p_mix, scale, w_out = saved[layer]
            dp, dz, dwout, dwmix, d_scale[layer // 2] = pool_bwd_mix(dx, p, z, p_mix, scale, w_out)
            result = pool_bwd_in(dp, dz, x_in, dx, g_row, p_in, carry=exchange)
            (dx, dwin, d_norm[layer]), landed = result if exchange is not None else (result, None)
            mine = [dwin, dwmix, dwout.reshape(N_CHIPS, -1, D_MODEL)]
        if pending is not None:
            sent[pending[0]] = (pending[1], landed)
        pending = (layer, mine)
    sent[0] = (pending[1], _run_exchange("reduce_layer_0", reduce_exchange(pending[1])))
    grad_x = dx

    last = jnp.concatenate([d_sinks[0], d_sinks[1], jnp.sum(sq).reshape(1),
                            jnp.zeros((D_MODEL - 2 * N_HEADS - 1,), F32)])[None, :]
    small = jnp.concatenate(d_norm + [d_final_g] + d_scale + [last], axis=0)
    small = all_reduce_small(small)
    loss = 0.5 * small[7, 2 * N_HEADS] / D_MODEL
    g_norm = small[0:4]
    g_final = small[4]
    g_scale = lax.dynamic_slice(small[5:7], (0, chip * 256), (2, 256))
    g_sinks = small[7, :2 * N_HEADS].reshape(2, N_HEADS)

    shares, landed = [], []
    for layer in (0, 1, 2, 3):
        shares += sent[layer][0]
        landed += sent[layer][1]
    reduced = sum_landed("sum_landed", shares, landed, chip, cc)
    from_sibling = _run_exchange("swap_reduced", swap_exchange(reduced))
    whole = [jnp.where(cc == 0, jnp.concatenate([mine, theirs]), jnp.concatenate([theirs, mine]))
             for mine, theirs in zip(reduced, from_sibling)]
    g_blocks = [jnp.stack([whole[t], whole[len(BIG) + t]]) for t in range(len(BIG))]
    g_blocks[0] = g_blocks[0].transpose(0, 2, 1)

    shapes = [w.shape for w in (attn_w_in, attn_w_out, pool_w_in, pool_w_mix, pool_w_out)]
    d_blocks, m_blocks, v_blocks = adamw(
        "adamw", w_blocks, g_blocks, blocks_2d(m_attn_w_in, m_attn_w_out, m_pool_w_in, m_pool_w_mix, m_pool_w_out),
        blocks_2d(v_attn_w_in, v_attn_w_out, v_pool_w_in, v_pool_w_mix, v_pool_w_out), steps=4)
    g_big, d_big, m_big, v_big = (
        {name: b.reshape(s) for name, b, s in zip(BIG, blocks, shapes)} for blocks in (g_blocks, d_blocks, m_blocks, v_blocks))

    def small_pack(ng, fg, sc, sk):
        row7 = jnp.concatenate([sk.reshape(2 * N_HEADS), jnp.zeros((D_MODEL - 2 * N_HEADS,), F32)])[None, :]
        sc = jnp.concatenate([sc, jnp.zeros((2, D_MODEL - 256), F32)], axis=1)
        return jnp.concatenate([ng, fg[None, :], sc, row7], axis=0)

    packs = [small_pack(norm_g, final_g, pool_scale, attn_sinks), small_pack(g_norm, g_final, g_scale, g_sinks),
             small_pack(m_norm_g, m_final_g, m_pool_scale, m_attn_sinks), small_pack(v_norm_g, v_final_g, v_pool_scale, v_attn_sinks)]
    small_out = [out[0] for out in adamw("adamw_small", *[[p[None]] for p in packs], steps=1)]

    def small_unpack(t):
        t = t[0]
        return {"norm_g": t[0:4], "final_g": t[4], "pool_scale": t[5:7, :256], "attn_sinks": t[7, :2 * N_HEADS].reshape(2, N_HEADS)}

    d_small, m_small, v_small = (small_unpack(t) for t in small_out)
    g_small = {"norm_g": g_norm, "final_g": g_final, "pool_scale": g_scale, "attn_sinks": g_sinks}

    names = ["norm_g", "attn_w_in", "attn_sinks", "attn_w_out", "pool_w_in", "pool_w_mix", "pool_scale", "pool_w_out", "final_g"]

    def ordered(bigs, smalls):
        return [bigs[n] if n in bigs else smalls[n] for n in names]

    return (loss, grad_x[None], *ordered(g_big, g_small), *ordered(d_big, d_small), *ordered(m_big, m_small),
            *ordered(v_big, v_small))
```

```python
import functools
import math

import jax
import jax.numpy as jnp
from jax import lax
from jax.experimental import pallas as pl
from jax.experimental.pallas import tpu as pltpu

F32 = jnp.float32
BF16 = jnp.bfloat16

D_MODEL = 1024
N_HEADS = 16
N_KV_HEADS = 4
GROUP = N_HEADS // N_KV_HEADS
HEAD_DIM = 64
KV_WIDTH = N_KV_HEADS * HEAD_DIM
ATTN_BLOCK = 128
POOL_WINDOWS = (2, 4, 8, 16)
POOL_GC = 256
POOL_HALO = 16
EPS = 1e-6
N_CHIPS = 4
ATTN_CHUNK = 640
POOL_CHUNK = 512
ROW_TILE = 512
NEG_BIG = -1e30
VMEM_LIMIT_BYTES = 60 * 1024 * 1024

ADAM_LR = 0.001
ADAM_B1 = 0.9
ADAM_B2 = 0.999
ADAM_EPS = 1e-08
ADAM_WD = 0.01
ADAM_STEP = 10

MESH_ID = pl.DeviceIdType.MESH


def _dot(a, b):
    return jnp.dot(a, b, preferred_element_type=F32)


def _dot_nt(a, b):
    return lax.dot_general(a, b, (((1,), (1,)), ((), ())), preferred_element_type=F32)


def _dot_tn(a, b):
    return lax.dot_general(a, b, (((0,), (0,)), ((), ())), preferred_element_type=F32)


def _vmem():
    return pl.BlockSpec(memory_space=pltpu.VMEM)


def _rows(width, tile=ROW_TILE):
    return pl.BlockSpec((tile, width), lambda i: (i, 0))


def _params(semantics):
    return pltpu.CompilerParams(dimension_semantics=(semantics,), vmem_limit_bytes=VMEM_LIMIT_BYTES)


def _grid_call(name, body, tiles, in_specs, out_specs, out_shape, scratch_shapes, args, semantics, carry=None):
    if carry is None:
        return pl.pallas_call(body, name=name, grid=(tiles,), in_specs=in_specs, out_specs=out_specs, out_shape=out_shape,
                              scratch_shapes=scratch_shapes, compiler_params=_params(semantics))(*args)
    counts = [len(args), len(carry.inputs), len(out_shape), len(carry.out_shapes), len(scratch_shapes), 2]

    def wrapped(*refs):
        groups, at = [], 0
        for n in counts:
            groups.append(refs[at:at + n])
            at += n
        ins, c_in, outs, c_out, scratch, sems = groups
        step = pl.program_id(0)
        pl.when(step == 0)(lambda: carry.start(c_in, c_out, sems))
        if carry.mid is not None:
            pl.when(step == tiles // 2)(lambda: carry.mid(c_in, c_out, sems))
        body(*ins, *outs, *scratch)
        pl.when(step == tiles - 1)(lambda: carry.finish(c_in, c_out, sems))

    results = pl.pallas_call(
        wrapped, name=name, grid=(tiles,),
        in_specs=list(in_specs) + [_any()] * len(carry.inputs),
        out_specs=list(out_specs) + [_any()] * len(carry.out_shapes),
        out_shape=list(out_shape) + carry.out_shapes,
        scratch_shapes=list(scratch_shapes) + [pltpu.SemaphoreType.DMA((carry.n_remote,))] * 2,
        compiler_params=_params("arbitrary"))(*args, *carry.inputs)
    return results[:len(out_shape)], results[len(out_shape):]


def _norm(xf, g):
    rstd = lax.rsqrt(jnp.mean(xf * xf, axis=-1, keepdims=True) + EPS)
    xhat = xf * rstd
    return xhat, rstd, xhat * g


def _norm_bwd(dh, xhat, rstd, g):
    dg = jnp.sum(dh * xhat, axis=0, keepdims=True)
    dxhat = dh * g
    dx = rstd * (dxhat - xhat * jnp.mean(dxhat * xhat, axis=-1, keepdims=True))
    return dx, dg


def _silu_parts(zf):
    sig = jax.nn.sigmoid(zf)
    return zf * sig, sig * (1.0 + zf * (1.0 - sig))


def _cols(height, tile=ROW_TILE):
    return pl.BlockSpec((height, tile), lambda i: (0, i))


def _halo_prev_rows(width):
    per_tile = ROW_TILE // ATTN_BLOCK
    return pl.BlockSpec((ATTN_BLOCK, width), lambda i: (jnp.maximum(i * per_tile - 1, 0), 0))


def _halo_prev_cols(height):
    per_tile = ROW_TILE // ATTN_BLOCK
    return pl.BlockSpec((height, ATTN_BLOCK), lambda i: (0, jnp.maximum(i * per_tile - 1, 0)))


KV_AT = D_MODEL
Z_AT = D_MODEL + 2 * KV_WIDTH


def attn_in_fwd(x, g, w_t, carry=None):
    T = x.shape[0]

    def body(x_ref, g_ref, w_ref, qt_ref, kv_ref, kvt_ref, zt_ref):
        _, _, h = _norm(x_ref[...], g_ref[...])
        h = h.astype(BF16)
        scale = 1.0 / math.sqrt(HEAD_DIM)
        qt_ref[...] = (_dot_nt(w_ref[:KV_AT], h) * scale).astype(BF16)
        zt_ref[...] = _dot_nt(w_ref[Z_AT:], h).astype(BF16)
        kvt_ref[...] = _dot_nt(w_ref[KV_AT:Z_AT], h).astype(BF16)
        kv_ref[...] = _dot_nt(h, w_ref[KV_AT:Z_AT]).astype(BF16)

    return _grid_call(
        "attn_in_fwd", body, T // ROW_TILE,
        in_specs=[_rows(D_MODEL), _vmem(), _vmem()],
        out_specs=[_cols(D_MODEL), _rows(2 * KV_WIDTH), _cols(2 * KV_WIDTH), _cols(D_MODEL)],
        out_shape=[
            jax.ShapeDtypeStruct((D_MODEL, T), BF16),
            jax.ShapeDtypeStruct((T, 2 * KV_WIDTH), BF16),
            jax.ShapeDtypeStruct((2 * KV_WIDTH, T), BF16),
            jax.ShapeDtypeStruct((D_MODEL, T), BF16),
        ],
        scratch_shapes=[], args=(x, g, w_t), semantics="parallel", carry=carry)


def _causal_triangle():
    shape = (ATTN_BLOCK, GROUP * ATTN_BLOCK)
    kj = lax.broadcasted_iota(jnp.int32, shape, 0)
    qi = lax.broadcasted_iota(jnp.int32, shape, 1) & (ATTN_BLOCK - 1)
    return kj <= qi


def _group_cols(ref, hk, cols):
    return jnp.concatenate(
        [ref[(hk * GROUP + gi) * HEAD_DIM:(hk * GROUP + gi + 1) * HEAD_DIM, cols] for gi in range(GROUP)], axis=1)


def _group_row(values):
    return jnp.concatenate(values, axis=1)


def _window_rows(ref, halo_ref, b, lanes):
    if b == 0:
        return jnp.concatenate([halo_ref[:, lanes], ref[:ATTN_BLOCK, lanes]], axis=0)
    return ref[(b - 1) * ATTN_BLOCK:(b + 1) * ATTN_BLOCK, lanes]


def _window_cols(ref, halo_ref, b, rows):
    if b == 0:
        return jnp.concatenate([halo_ref[rows, :], ref[rows, :ATTN_BLOCK]], axis=1)
    return ref[rows, (b - 1) * ATTN_BLOCK:(b + 1) * ATTN_BLOCK]


def _select_window(both, tri, b=None, first_penalty=None):
    prev = both[:ATTN_BLOCK]
    if b == 0:
        prev = prev + first_penalty
    return jnp.where(tri, both[ATTN_BLOCK:], prev)


def _split_window(slots, tri):
    zero = jnp.zeros_like(slots)
    return jnp.concatenate([jnp.where(tri, zero, slots), jnp.where(tri, slots, zero)], axis=0)


def attn_core_fwd(q_t, kv, kv_t, z_t, x, w_out, sinks, carry=None):
    T = x.shape[0]
    blocks = ROW_TILE // ATTN_BLOCK

    def body(qt_ref, kv_ref, kvh_ref, kvt_ref, kvth_ref, zt_ref, x_ref, w_ref, sink_ref, xo_ref, ot_ref, lse_ref, oacc):
        tri = _causal_triangle()
        first_penalty = jnp.where(pl.program_id(0) == 0, NEG_BIG, 0.0)
        for hk in range(N_KV_HEADS):
            heads = [hk * GROUP + gi for gi in range(GROUP)]
            sink = _group_row([jnp.full((1, ATTN_BLOCK), sink_ref[h], F32) for h in heads])
            k_lanes = slice(hk * HEAD_DIM, (hk + 1) * HEAD_DIM)
            v_rows = slice(KV_WIDTH + hk * HEAD_DIM, KV_WIDTH + (hk + 1) * HEAD_DIM)
            for b in range(blocks):
                cols = slice(b * ATTN_BLOCK, (b + 1) * ATTN_BLOCK)
                qt = _group_cols(qt_ref, hk, cols)
                s = _select_window(_dot(_window_rows(kv_ref, kvh_ref, b, k_lanes), qt), tri, b, first_penalty)
                m = jnp.maximum(jnp.max(s, axis=0, keepdims=True), sink)
                p = jnp.exp(s - m)
                denom = jnp.sum(p, axis=0, keepdims=True) + jnp.exp(sink - m)
                o = _dot(_window_cols(kvt_ref, kvth_ref, b, v_rows), _split_window(p.astype(BF16), tri)) * (1.0 / denom)
                lse = m + jnp.log(denom)
                for gi, h in enumerate(heads):
                    part = slice(gi * ATTN_BLOCK, (gi + 1) * ATTN_BLOCK)
                    oacc[h * HEAD_DIM:(h + 1) * HEAD_DIM, cols] = o[:, part]
                    lse_ref[h:h + 1, cols] = lse[:, part]
        of = oacc[...]
        silu, _ = _silu_parts(zt_ref[...].astype(F32))
        y = _dot_tn((of * silu).astype(BF16), w_ref[...])
        xo_ref[...] = x_ref[...] + y
        ot_ref[...] = of.astype(BF16)

    return _grid_call(
        "attn_core_fwd", body, T // ROW_TILE,
        in_specs=[_cols(D_MODEL), _rows(2 * KV_WIDTH), _halo_prev_rows(2 * KV_WIDTH), _cols(2 * KV_WIDTH),
                  _halo_prev_cols(2 * KV_WIDTH), _cols(D_MODEL), _rows(D_MODEL), _vmem(),
                  pl.BlockSpec(memory_space=pltpu.SMEM)],
        out_specs=[_rows(D_MODEL), _cols(D_MODEL), _cols(N_HEADS)],
        out_shape=[
            jax.ShapeDtypeStruct((T, D_MODEL), F32),
            jax.ShapeDtypeStruct((D_MODEL, T), BF16),
            jax.ShapeDtypeStruct((N_HEADS, T), F32),
        ],
        scratch_shapes=[pltpu.VMEM((D_MODEL, ROW_TILE), F32)],
        args=(q_t, kv, kv, kv_t, kv_t, z_t, x, w_out, sinks), semantics="parallel", carry=carry)


def _inv_count(first_row, rows, window):
    t = first_row + lax.broadcasted_iota(jnp.int32, (rows, 1), 0)
    return 1.0 / jnp.minimum(t + 1, window).astype(F32)


MIX_ROWS = POOL_GC // N_CHIPS


def _mix_groups(wmix_ref):
    return [jnp.concatenate([wmix_ref[k, gi * MIX_ROWS:(gi + 1) * MIX_ROWS, :] for k in range(N_CHIPS)], axis=0)
            for gi in range(len(POOL_WINDOWS))]


def pool_fwd(x, g, w_in, w_mix, scale, w_out, carry=None):
    T = x.shape[0]

    def body(x_ref, g_ref, win_ref, wmix_ref, scale_ref, wout_ref, xo_ref, p_ref, z_ref, carry):
        i = pl.program_id(0)
        wmix = _mix_groups(wmix_ref)

        @pl.when(i == 0)
        def _():
            carry[...] = jnp.zeros_like(carry)

        _, _, h = _norm(x_ref[...], g_ref[...])
        h = h.astype(BF16)
        u = jnp.concatenate([_dot(h, win_ref[0]), _dot(h, win_ref[1])], axis=1)
        z = jnp.concatenate([_dot(h, win_ref[2]), _dot(h, win_ref[3])], axis=1)
        ext = jnp.concatenate([carry[...], u], axis=0)
        carry[...] = u[ROW_TILE - POOL_HALO:]
        mixed = []
        for gi, window in enumerate(POOL_WINDOWS):
            cols = slice(gi * POOL_GC, (gi + 1) * POOL_GC)
            s = ext[:, cols]
            shift = 1
            while shift < window:
                s = s + pltpu.roll(s, shift, 0)
                shift *= 2
            p = s[POOL_HALO:] * _inv_count(i * ROW_TILE, ROW_TILE, window) - u[:, cols]
            p = p.astype(BF16)
            p_ref[:, cols] = p
            mixed.append(_dot(p, wmix[gi]))
        m = jnp.concatenate(mixed, axis=1) * scale_ref[...]
        silu, _ = _silu_parts(z)
        y = _dot((m * silu).astype(BF16), wout_ref[...])
        xo_ref[...] = x_ref[...] + y
        z_ref[...] = z.astype(BF16)

    return _grid_call(
        "pool_fwd", body, T // ROW_TILE,
        in_specs=[_rows(D_MODEL), _vmem(), _vmem(), _vmem(), _vmem(), _vmem()],
        out_specs=[_rows(D_MODEL), _rows(D_MODEL), _rows(D_MODEL)],
        out_shape=[
            jax.ShapeDtypeStruct((T, D_MODEL), F32),
            jax.ShapeDtypeStruct((T, D_MODEL), BF16),
            jax.ShapeDtypeStruct((T, D_MODEL), BF16),
        ],
        scratch_shapes=[pltpu.VMEM((POOL_HALO, D_MODEL), F32)],
        args=(x, g, w_in, w_mix, scale, w_out), semantics="arbitrary", carry=carry)


def loss_head(x, g, target):
    T = x.shape[0]

    def body(x_ref, g_ref, t_ref, dx_ref, sq_ref, dg_ref):
        @pl.when(pl.program_id(0) == 0)
        def _():
            sq_ref[...] = jnp.zeros_like(sq_ref)
            dg_ref[...] = jnp.zeros_like(dg_ref)

        gv = g_ref[...]
        xhat, rstd, out = _norm(x_ref[...], gv)
        err = out - t_ref[...]
        sq_ref[...] += jnp.sum(err * err, axis=0, keepdims=True)
        dx, dg = _norm_bwd(err * (1.0 / D_MODEL), xhat, rstd, gv)
        dx_ref[...] = dx
        dg_ref[...] += dg

    return pl.pallas_call(
        body,
        name="loss_head",
        grid=(T // ROW_TILE,),
        in_specs=[_rows(D_MODEL), _vmem(), _rows(D_MODEL)],
        out_specs=[_rows(D_MODEL), _vmem(), _vmem()],
        out_shape=[
            jax.ShapeDtypeStruct((T, D_MODEL), F32),
            jax.ShapeDtypeStruct((1, D_MODEL), F32),
            jax.ShapeDtypeStruct((1, D_MODEL), F32),
        ],
        compiler_params=_params("arbitrary"),
    )(x, g, target)


def pool_bwd_mix(dy, p, z, w_mix, scale, w_out):
    T = dy.shape[0]
    tiles = T // ROW_TILE

    def body(dy_ref, p_ref, z_ref, wmix_ref, scale_ref, wout_ref, dp_ref, dz_ref, dwout_out, dwmix_out, dscale_ref,
             dwout_ref, dwmix_ref):
        @pl.when(pl.program_id(0) == 0)
        def _():
            dwout_ref[...] = jnp.zeros_like(dwout_ref)
            dwmix_ref[...] = jnp.zeros_like(dwmix_ref)
            dscale_ref[...] = jnp.zeros_like(dscale_ref)

        wmix = _mix_groups(wmix_ref)
        dyb = dy_ref[...].astype(BF16)
        da = _dot_nt(dyb, wout_ref[...])
        m_pre = jnp.concatenate(
            [_dot(p_ref[:, gi * POOL_GC:(gi + 1) * POOL_GC], wmix[gi]) for gi in range(len(POOL_WINDOWS))], axis=1)
        sc = scale_ref[...]
        m = m_pre * sc
        zf = z_ref[...].astype(F32)
        silu, dsilu = _silu_parts(zf)
        dwout_ref[...] += _dot_tn((m * silu).astype(BF16), dyb)
        dm = da * silu
        dz_ref[...] = (da * m * dsilu).astype(BF16)
        dscale_ref[...] += jnp.sum(dm * m_pre, axis=0, keepdims=True)
        dmp = (dm * sc).astype(BF16)
        for gi in range(len(POOL_WINDOWS)):
            cols = slice(gi * POOL_GC, (gi + 1) * POOL_GC)
            dw = _dot_tn(p_ref[:, cols], dmp[:, cols])
            for k in range(N_CHIPS):
                dwmix_ref[k, gi * MIX_ROWS:(gi + 1) * MIX_ROWS, :] += dw[k * MIX_ROWS:(k + 1) * MIX_ROWS]
            dp_ref[:, cols] = _dot_nt(dmp[:, cols], wmix[gi]).astype(BF16)

        @pl.when(pl.program_id(0) == tiles - 1)
        def _():
            dwout_out[...] = dwout_ref[...].astype(BF16)
            dwmix_out[...] = dwmix_ref[...].astype(BF16)

    return pl.pallas_call(
        body,
        name="pool_bwd_mix",
        grid=(tiles,),
        in_specs=[_rows(D_MODEL), _rows(D_MODEL), _rows(D_MODEL), _vmem(), _vmem(), _vmem()],
        out_specs=[_rows(D_MODEL), _rows(D_MODEL), _vmem(), _vmem(), _vmem()],
        out_shape=[
            jax.ShapeDtypeStruct((T, D_MODEL), BF16),
            jax.ShapeDtypeStruct((T, D_MODEL), BF16),
            jax.ShapeDtypeStruct((D_MODEL, D_MODEL), BF16),
            jax.ShapeDtypeStruct((N_CHIPS, POOL_GC, POOL_GC), BF16),
            jax.ShapeDtypeStruct((1, D_MODEL), F32),
        ],
        scratch_shapes=[pltpu.VMEM((D_MODEL, D_MODEL), F32), pltpu.VMEM((N_CHIPS, POOL_GC, POOL_GC), F32)],
        compiler_params=_params("arbitrary"),
    )(dy, p, z, w_mix, scale, w_out)


def pool_bwd_in(dp, dz, x, dy, g, w_in, carry=None):
    T = x.shape[0]
    tiles = T // ROW_TILE
    halo_blocks = ROW_TILE // POOL_HALO
    last_halo = T // POOL_HALO - 1

    def body(dp_ref, dph_ref, dz_ref, x_ref, dy_ref, g_ref, win_ref, dx_ref, dwin_out, dg_ref, dwin_ref):
        i = pl.program_id(0)

        @pl.when(i == 0)
        def _():
            dwin_ref[...] = jnp.zeros_like(dwin_ref)
            dg_ref[...] = jnp.zeros_like(dg_ref)

        rows = ROW_TILE + POOL_HALO
        ext = jnp.concatenate([dp_ref[...], dph_ref[...]], axis=0).astype(F32)
        t = i * ROW_TILE + lax.broadcasted_iota(jnp.int32, (rows, 1), 0)
        inside = (t < T).astype(F32)
        du = []
        for gi, window in enumerate(POOL_WINDOWS):
            cols = slice(gi * POOL_GC, (gi + 1) * POOL_GC)
            s = ext[:, cols] * (_inv_count(i * ROW_TILE, rows, window) * inside)
            shift = 1
            while shift < window:
                s = s + pltpu.roll(s, rows - shift, 0)
                shift *= 2
            du.append(s[:ROW_TILE] - ext[:ROW_TILE, cols])
        du = jnp.concatenate(du, axis=1).astype(BF16)
        chunks = [du[:, :POOL_CHUNK], du[:, POOL_CHUNK:], dz_ref[:, :POOL_CHUNK], dz_ref[:, POOL_CHUNK:]]
        gv = g_ref[...]
        xhat, rstd, h = _norm(x_ref[...], gv)
        h = h.astype(BF16)
        dh = jnp.zeros((ROW_TILE, D_MODEL), F32)
        for c in range(N_CHIPS):
            dwin_ref[c] += _dot_tn(h, chunks[c])
            dh = dh + _dot_nt(chunks[c], win_ref[c])
        dx, dg = _norm_bwd(dh, xhat, rstd, gv)
        dx_ref[...] = dx + dy_ref[...]
        dg_ref[...] += dg

        @pl.when(i == tiles - 1)
        def _():
            dwin_out[...] = dwin_ref[...].astype(BF16)

    return _grid_call(
        "pool_bwd_in", body, tiles,
        in_specs=[_rows(D_MODEL),
                  pl.BlockSpec((POOL_HALO, D_MODEL), lambda i: (jnp.minimum((i + 1) * halo_blocks, last_halo), 0)),
                  _rows(D_MODEL), _rows(D_MODEL), _rows(D_MODEL), _vmem(), _vmem()],
        out_specs=[_rows(D_MODEL), _vmem(), _vmem()],
        out_shape=[
            jax.ShapeDtypeStruct((T, D_MODEL), F32),
            jax.ShapeDtypeStruct((N_CHIPS, D_MODEL, POOL_CHUNK), BF16),
            jax.ShapeDtypeStruct((1, D_MODEL), F32),
        ],
        scratch_shapes=[pltpu.VMEM((N_CHIPS, D_MODEL, POOL_CHUNK), F32)],
        args=(dp, dp, dz, x, dy, g, w_in), semantics="arbitrary", carry=carry)


def attn_bwd_core(dy, q_t, kv, kv_t, z_t, o_t, lse, w_out, sinks, carry=None):
    T = dy.shape[0]
    tiles = T // ROW_TILE
    blocks = ROW_TILE // ATTN_BLOCK

    def body(dy_ref, qt_ref, kv_ref, kvh_ref, kvt_ref, kvth_ref, zt_ref, ot_ref, lse_ref, w_ref, sink_ref,
             dqt_ref, dkv_ref, dkvh_ref, dzt_ref, dwout_out, dsink_ref, do_s, dkv_s, dwout_ref):
        @pl.when(pl.program_id(0) == 0)
        def _():
            dwout_ref[...] = jnp.zeros_like(dwout_ref)
            dsink_ref[...] = jnp.zeros_like(dsink_ref)

        dyb = dy_ref[...].astype(BF16)
        da = _dot_nt(w_ref[...], dyb)
        of = ot_ref[...].astype(F32)
        silu, dsilu = _silu_parts(zt_ref[...].astype(F32))
        dwout_ref[...] += _dot((of * silu).astype(BF16), dyb)
        do = da * silu
        dzt_ref[...] = (da * of * dsilu).astype(BF16)
        do_s[...] = do.astype(BF16)
        dof = do * of
        dkv_s[...] = jnp.zeros_like(dkv_s)
        tri = _causal_triangle()
        first_penalty = jnp.where(pl.program_id(0) == 0, NEG_BIG, 0.0)
        for hk in range(N_KV_HEADS):
            heads = [hk * GROUP + gi for gi in range(GROUP)]
            sink = _group_row([jnp.full((1, ATTN_BLOCK), sink_ref[h], F32) for h in heads])
            deltas = [jnp.sum(dof[h * HEAD_DIM:(h + 1) * HEAD_DIM], axis=0, keepdims=True) for h in heads]
            k_lanes = slice(hk * HEAD_DIM, (hk + 1) * HEAD_DIM)
            v_lanes = slice(KV_WIDTH + hk * HEAD_DIM, KV_WIDTH + (hk + 1) * HEAD_DIM)
            for b in range(blocks):
                cols = slice(b * ATTN_BLOCK, (b + 1) * ATTN_BLOCK)
                window = slice(b * ATTN_BLOCK, (b + 2) * ATTN_BLOCK)
                qt = _group_cols(qt_ref, hk, cols)
                dot = _group_cols(do_s, hk, cols)
                lse_row = _group_row([lse_ref[h:h + 1, cols] for h in heads])
                delta = _group_row([d[:, cols] for d in deltas])
                s = _select_window(_dot(_window_rows(kv_ref, kvh_ref, b, k_lanes), qt), tri, b, first_penalty)
                p = jnp.exp(s - lse_row)
                dp = _select_window(_dot(_window_rows(kv_ref, kvh_ref, b, v_lanes), dot), tri)
                ds = _split_window((p * (dp - delta)).astype(BF16), tri)
                dq = _dot(_window_cols(kvt_ref, kvth_ref, b, k_lanes), ds) * (1.0 / math.sqrt(HEAD_DIM))
                dkv_s[window, k_lanes] += _dot_nt(ds, qt)
                dkv_s[window, v_lanes] += _dot_nt(_split_window(p.astype(BF16), tri), dot)
                dsink = -jnp.exp(sink - lse_row) * delta
                for gi, h in enumerate(heads):
                    part = slice(gi * ATTN_BLOCK, (gi + 1) * ATTN_BLOCK)
                    dqt_ref[h * HEAD_DIM:(h + 1) * HEAD_DIM, cols] = dq[:, part].astype(BF16)
                    dsink_ref[0:1, h:h + 1] += jnp.sum(dsink[:, part], axis=1, keepdims=True)
        dkv_ref[...] = dkv_s[ATTN_BLOCK:]
        dkvh_ref[0] = dkv_s[:ATTN_BLOCK]

        @pl.when(pl.program_id(0) == tiles - 1)
        def _():
            dwout_out[...] = dwout_ref[...].astype(BF16)

    return _grid_call(
        "attn_bwd_core", body, tiles,
        in_specs=[_rows(D_MODEL), _cols(D_MODEL), _rows(2 * KV_WIDTH), _halo_prev_rows(2 * KV_WIDTH),
                  _cols(2 * KV_WIDTH), _halo_prev_cols(2 * KV_WIDTH), _cols(D_MODEL), _cols(D_MODEL), _cols(N_HEADS),
                  _vmem(), pl.BlockSpec(memory_space=pltpu.SMEM)],
        out_specs=[_cols(D_MODEL), _rows(2 * KV_WIDTH), pl.BlockSpec((1, ATTN_BLOCK, 2 * KV_WIDTH), lambda i: (i, 0, 0)),
                   _cols(D_MODEL), _vmem(), _vmem()],
        out_shape=[
            jax.ShapeDtypeStruct((D_MODEL, T), BF16),
            jax.ShapeDtypeStruct((T, 2 * KV_WIDTH), F32),
            jax.ShapeDtypeStruct((tiles, ATTN_BLOCK, 2 * KV_WIDTH), F32),
            jax.ShapeDtypeStruct((D_MODEL, T), BF16),
            jax.ShapeDtypeStruct((D_MODEL, D_MODEL), BF16),
            jax.ShapeDtypeStruct((8, 128), F32),
        ],
        scratch_shapes=[
            pltpu.VMEM((D_MODEL, ROW_TILE), BF16),
            pltpu.VMEM((ROW_TILE + ATTN_BLOCK, 2 * KV_WIDTH), F32),
            pltpu.VMEM((D_MODEL, D_MODEL), F32),
        ],
        args=(dy, q_t, kv, kv, kv_t, kv_t, z_t, o_t, lse, w_out, sinks), semantics="arbitrary", carry=carry)


def attn_bwd_in(dq_t, dkv, dkv_halo, dz_t, x, dy, g, w_t, carry=None):
    T = x.shape[0]
    tiles = T // ROW_TILE
    kv_at, z_at = KV_AT, Z_AT

    def body(dqt_ref, dkv_ref, dkvh_ref, dzt_ref, x_ref, dy_ref, g_ref, w_ref, dx_ref, dwt_out, dg_ref, dwt_ref):
        i = pl.program_id(0)

        @pl.when(i == 0)
        def _():
            dwt_ref[...] = jnp.zeros_like(dwt_ref)
            dg_ref[...] = jnp.zeros_like(dg_ref)

        more = (i < tiles - 1).astype(F32)
        tail = jnp.concatenate([jnp.zeros((ROW_TILE - ATTN_BLOCK, 2 * KV_WIDTH), F32), dkvh_ref[0] * more], axis=0)
        dkvb = (dkv_ref[...] + tail).astype(BF16)
        gv = g_ref[...]
        xhat, rstd, h = _norm(x_ref[...], gv)
        h = h.astype(BF16)
        dqt = dqt_ref[...]
        dzt = dzt_ref[...]
        dwt_ref[:kv_at] += _dot(dqt, h)
        dwt_ref[kv_at:z_at] += _dot_tn(dkvb, h)
        dwt_ref[z_at:] += _dot(dzt, h)
        dh = _dot_tn(dqt, w_ref[:kv_at]) + _dot_tn(dzt, w_ref[z_at:]) + _dot(dkvb, w_ref[kv_at:z_at])
        dx, dg = _norm_bwd(dh, xhat, rstd, gv)
        dx_ref[...] = dx + dy_ref[...]
        dg_ref[...] += dg

        @pl.when(i == tiles - 1)
        def _():
            dwt_out[...] = dwt_ref[...].astype(BF16)

    halo_next = pl.BlockSpec((1, ATTN_BLOCK, 2 * KV_WIDTH), lambda i: (jnp.minimum(i + 1, tiles - 1), 0, 0))
    return _grid_call(
        "attn_bwd_in", body, tiles,
        in_specs=[_cols(D_MODEL), _rows(2 * KV_WIDTH), halo_next, _cols(D_MODEL), _rows(D_MODEL), _rows(D_MODEL),
                  _vmem(), _vmem()],
        out_specs=[_rows(D_MODEL), _vmem(), _vmem()],
        out_shape=[
            jax.ShapeDtypeStruct((T, D_MODEL), F32),
            jax.ShapeDtypeStruct((N_CHIPS * ATTN_CHUNK, D_MODEL), BF16),
            jax.ShapeDtypeStruct((1, D_MODEL), F32),
        ],
        scratch_shapes=[pltpu.VMEM((N_CHIPS * ATTN_CHUNK, D_MODEL), F32)],
        args=(dq_t, dkv, dkv_halo, dz_t, x, dy, g, w_t), semantics="arbitrary", carry=carry)


BIG = ("attn_w_in", "attn_w_out", "pool_w_in", "pool_w_mix", "pool_w_out")
SMALL_ROWS = 8


def _half(ref, h, axis=0):
    rows = ref.shape[axis] // 2
    index = (slice(None),) * axis + (pl.ds(h * rows, rows),)
    return ref.at[index]


def _mesh_position():
    return lax.axis_index("x"), lax.axis_index("y"), lax.axis_index("c")


def _other_chips(x, y):
    return [(1 - x, y), (x, 1 - y), (1 - x, 1 - y)]


def _any():
    return pl.BlockSpec(memory_space=pl.ANY)


def _remote(src, dst, sems, index, to):
    send_sems, recv_sems = sems
    return pltpu.make_async_remote_copy(src_ref=src, dst_ref=dst, send_sem=send_sems.at[index], recv_sem=recv_sems.at[index],
                                        device_id=to, device_id_type=MESH_ID)


class Exchange:
    def __init__(self, inputs, out_shapes, n_remote, start, finish, mid=None):
        self.inputs, self.out_shapes, self.n_remote = list(inputs), list(out_shapes), n_remote
        self.start, self.finish, self.mid = start, finish, mid


def _run_exchange(name, ex):
    n_in, n_out = len(ex.inputs), len(ex.out_shapes)

    def kernel_body(*refs):
        ins, outs, sems = refs[:n_in], refs[n_in:n_in + n_out], refs[n_in + n_out:]
        ex.start(ins, outs, sems)
        if ex.mid is not None:
            ex.mid(ins, outs, sems)
        ex.finish(ins, outs, sems)

    return pl.pallas_call(
        kernel_body,
        name=name,
        in_specs=[_any()] * n_in,
        out_specs=[_any()] * n_out,
        out_shape=ex.out_shapes,
        scratch_shapes=[pltpu.SemaphoreType.DMA((ex.n_remote,))] * 2,
    )(*ex.inputs)


def gather_exchange(blocks, whole):
    def copies(ins, outs, sems):
        x, y, c = _mesh_position()
        me = 2 * x + y
        sibling = (x, y, 1 - c)
        table = []
        for t, (w_ref, out_ref) in enumerate(zip(ins, outs)):
            for j, (px, py) in enumerate(_other_chips(x, y)):
                peer = 2 * px + py
                if t in whole:
                    table.append((_remote(w_ref, out_ref.at[me], sems, 6 * t + j, (px, py, c)),
                                  _remote(w_ref, out_ref.at[peer], sems, 6 * t + j, (px, py, c)), None, None))
                    continue
                landed = _half(out_ref.at[peer], c)
                from_sibling = _half(out_ref.at[peer], 1 - c)
                table.append((_remote(_half(w_ref, c), _half(out_ref.at[me], c), sems, 6 * t + j, (px, py, c)),
                              _remote(landed, landed, sems, 6 * t + j, (px, py, c)),
                              _remote(landed, landed, sems, 6 * t + 3 + j, sibling),
                              _remote(from_sibling, from_sibling, sems, 6 * t + 3 + j, sibling)))
        return table

    def start(ins, outs, sems):
        for send, _, _, _ in copies(ins, outs, sems):
            send.start()

    def mid(ins, outs, sems):
        for _, arrival, forward, _ in copies(ins, outs, sems):
            if forward is not None:
                arrival.wait_recv()
                forward.start()

    def finish(ins, outs, sems):
        for send, arrival, forward, from_sibling in copies(ins, outs, sems):
            send.wait_send()
            if forward is None:
                arrival.wait_recv()
            else:
                forward.wait_send()
                from_sibling.wait_recv()

    shapes = [jax.ShapeDtypeStruct((N_CHIPS,) + b.shape, b.dtype) for b in blocks]
    return Exchange(blocks, shapes, 6 * len(blocks), start, finish, mid)


N_SENDERS = 7


def reduce_exchange(grads):
    def copies(ins, outs, sems):
        send_sems, recv_sems = sems
        x, y, c = _mesh_position()
        me = 2 * x + y
        sends, arrivals = [], []
        for t, (g_ref, out_ref) in enumerate(zip(ins, outs)):
            base = N_SENDERS * t

            def copy(src, slot, send_index, to):
                return pltpu.make_async_remote_copy(
                    src_ref=src, dst_ref=out_ref.at[slot], send_sem=send_sems.at[base + send_index],
                    recv_sem=recv_sems.at[base + slot], device_id=to, device_id_type=MESH_ID)

            mine = _half(g_ref.at[me], c)
            sends.append(copy(_half(g_ref.at[me], 1 - c), 0, 0, (x, y, 1 - c)))
            arrivals.append(copy(mine, 0, 0, (x, y, 1 - c)))
            for j, (px, py) in enumerate(_other_chips(x, y)):
                for h in range(2):
                    sends.append(copy(_half(g_ref.at[2 * px + py], h), 1 + 2 * j + c, 1 + 2 * j + h, (px, py, h)))
                    arrivals.append(copy(mine, 1 + 2 * j + h, 1 + 2 * j + h, (px, py, h)))
        return sends, arrivals

    def start(ins, outs, sems):
        for cp in copies(ins, outs, sems)[0]:
            cp.start()

    def finish(ins, outs, sems):
        sends, arrivals = copies(ins, outs, sems)
        for cp in sends:
            cp.wait_send()
        for cp in arrivals:
            cp.wait_recv()

    shapes = [jax.ShapeDtypeStruct((N_SENDERS, g.shape[1] // 2, g.shape[2]), g.dtype) for g in grads]
    return Exchange(grads, shapes, N_SENDERS * len(grads), start, finish)


def sum_landed(name, grads, landed, chip, core):
    n = len(grads)
    steps = 4

    def body(where_ref, *refs):
        for g_ref, l_ref, o_ref in zip(refs[:n], refs[n:2 * n], refs[2 * n:]):
            total = g_ref[0, 0].astype(F32)
            for s in range(N_SENDERS):
                total = total + l_ref[s].astype(F32)
            o_ref[...] = total

    halves = [g.reshape(N_CHIPS, 2, g.shape[1] // 2, g.shape[2]) for g in grads]

    def rows(h):
        return h.shape[2] // steps

    return pl.pallas_call(
        body,
        name=name,
        grid_spec=pltpu.PrefetchScalarGridSpec(
            num_scalar_prefetch=1, grid=(steps,),
            in_specs=[pl.BlockSpec((1, 1, rows(h), h.shape[3]), lambda r, where_ref: (where_ref[0], where_ref[1], r, 0))
                      for h in halves]
            + [pl.BlockSpec((N_SENDERS, rows(h), h.shape[3]), lambda r, where_ref: (0, r, 0)) for h in halves],
            out_specs=[pl.BlockSpec((rows(h), h.shape[3]), lambda r, where_ref: (r, 0)) for h in halves]),
        out_shape=[jax.ShapeDtypeStruct(h.shape[2:], F32) for h in halves],
        compiler_params=_params("parallel"),
    )(jnp.stack([chip, core]).astype(jnp.int32), *halves, *landed)


def swap_exchange(halves):
    def copies(ins, outs, sems):
        x, y, c = _mesh_position()
        return [_remote(h_ref, out_ref, sems, i, (x, y, 1 - c)) for i, (h_ref, out_ref) in enumerate(zip(ins, outs))]

    def start(ins, outs, sems):
        for cp in copies(ins, outs, sems):
            cp.start()

    def finish(ins, outs, sems):
        for cp in copies(ins, outs, sems):
            cp.wait()

    return Exchange(halves, [jax.ShapeDtypeStruct(h.shape, h.dtype) for h in halves], len(halves), start, finish)


def all_reduce_small(v):
    n_dev = 8

    def body(v_ref, out_ref, slots, send_sems, recv_sems):
        x, y, c = _mesh_position()
        me = 4 * x + 2 * y + c
        slots[me] = v_ref[...]
        peers = []
        for r in range(1, n_dev):
            fx, fy, fc = (r >> 2) & 1, (r >> 1) & 1, r & 1
            peers.append((1 - x if fx else x, 1 - y if fy else y, 1 - c if fc else c))
        sends = [
            pltpu.make_async_remote_copy(
                src_ref=v_ref, dst_ref=slots.at[me], send_sem=send_sems.at[r], recv_sem=recv_sems.at[r],
                device_id=peer, device_id_type=MESH_ID)
            for r, peer in enumerate(peers)
        ]
        for cp in sends:
            cp.start()
        for r, (px, py, pc) in enumerate(peers):
            pltpu.make_async_remote_copy(
                src_ref=v_ref, dst_ref=slots.at[4 * px + 2 * py + pc], send_sem=send_sems.at[r], recv_sem=recv_sems.at[r],
                device_id=(px, py, pc), device_id_type=MESH_ID).wait_recv()
        for cp in sends:
            cp.wait_send()
        total = slots[0]
        for d in range(1, n_dev):
            total = total + slots[d]
        out_ref[...] = total

    return pl.pallas_call(
        body,
        name="all_reduce_small",
        in_specs=[_vmem()],
        out_specs=_vmem(),
        out_shape=jax.ShapeDtypeStruct(v.shape, v.dtype),
        scratch_shapes=[pltpu.VMEM((n_dev,) + v.shape, v.dtype), pltpu.SemaphoreType.DMA((n_dev - 1,)),
                        pltpu.SemaphoreType.DMA((n_dev - 1,))],
    )(v)


def adamw(name, ws, gs, ms, vs, steps):
    n = len(ws)

    def body(*refs):
        ins, outs = refs[:4 * n], refs[4 * n:]
        for t in range(n):
            w_ref, g_ref, m_ref, v_ref = (ins[q * n + t] for q in range(4))
            d_ref, mo_ref, vo_ref = (outs[q * n + t] for q in range(3))
            gv = g_ref[...]
            m2 = ADAM_B1 * m_ref[...] + (1.0 - ADAM_B1) * gv
            v2 = ADAM_B2 * v_ref[...] + (1.0 - ADAM_B2) * (gv * gv)
            m_hat = m2 / (1.0 - ADAM_B1 ** ADAM_STEP)
            v_hat = v2 / (1.0 - ADAM_B2 ** ADAM_STEP)
            d_ref[...] = -ADAM_LR * (m_hat / (jnp.sqrt(v_hat) + ADAM_EPS) + ADAM_WD * w_ref[...])
            mo_ref[...] = m2
            vo_ref[...] = v2

    specs = [pl.BlockSpec((1, w.shape[1] // steps, w.shape[2]), lambda j, r: (j, r, 0)) for w in ws]
    shapes = [jax.ShapeDtypeStruct(w.shape, F32) for w in ws]
    out = pl.pallas_call(
        body, name=name, grid=(ws[0].shape[0], steps), in_specs=specs * 4, out_specs=specs * 3, out_shape=shapes * 3,
        compiler_params=pltpu.CompilerParams(dimension_semantics=("parallel", "parallel"), vmem_limit_bytes=VMEM_LIMIT_BYTES),
    )(*ws, *gs, *ms, *vs)
    return out[:n], out[n:2 * n], out[2 * n:]


def kernel(x, norm_g, attn_w_in, attn_sinks, attn_w_out, pool_w_in, pool_w_mix, pool_scale, pool_w_out, final_g, loss_target, m_norm_g, m_attn_w_in, m_attn_sinks, m_attn_w_out, m_pool_w_in, m_pool_w_mix, m_pool_scale, m_pool_w_out, m_final_g, v_norm_g, v_attn_w_in, v_attn_sinks, v_attn_w_out, v_pool_w_in, v_pool_w_mix, v_pool_scale, v_pool_w_out, v_final_g):
    cx, cy, cc = _mesh_position()
    chip = 2 * cx + cy

    def blocks_2d(a_in, a_out, p_in, p_mix, p_out):
        return [a_in, a_out, p_in, p_mix.reshape(2, POOL_GC, POOL_GC), p_out]

    w_blocks = blocks_2d(attn_w_in, attn_w_out, pool_w_in, pool_w_mix, pool_w_out)

    def gather(layer):
        j = layer // 2
        if layer % 2 == 0:
            own = [w_blocks[0][j].T.astype(BF16), w_blocks[1][j].astype(BF16)]
            return own, gather_exchange(own, whole=())
        own = [w_blocks[t][j].astype(BF16) for t in (2, 3, 4)] + [pool_scale[j][None, :]]
        return own, gather_exchange(own, whole=(3,))

    def placed(own, others):
        return [lax.dynamic_update_slice(o, mine[None], (chip, 0, 0)) for o, mine in zip(others, own)]

    own = gather(0)[0]
    weights = placed(own[:1], _run_exchange("gather_layer_0", gather_exchange(own[:1], whole=())))
    xs = x[0]
    saved = []
    for layer in range(4):
        g_row = norm_g[layer][None, :]
        if layer % 2 == 0:
            w_t = weights[0].reshape(N_CHIPS * ATTN_CHUNK, D_MODEL)
            if len(weights) == 1:
                (q_t, kv, kv_t, z_t), landed = attn_in_fwd(xs, g_row, w_t, carry=gather_exchange(own[1:], whole=()))
                weights += placed(own[1:], landed)
            else:
                q_t, kv, kv_t, z_t = attn_in_fwd(xs, g_row, w_t)
            w_out = weights[1].reshape(D_MODEL, D_MODEL)
            own, exchange = gather(layer + 1)
            (x_new, o_t, lse), landed = attn_core_fwd(q_t, kv, kv_t, z_t, xs, w_out, attn_sinks[layer // 2], carry=exchange)
            saved.append((xs, g_row, q_t, kv, kv_t, z_t, o_t, lse, w_t, w_out))
        else:
            p_in, p_mix, p_out, scale = weights
            scale, w_out = scale.reshape(1, D_MODEL), p_out.reshape(D_MODEL, D_MODEL)
            own, exchange = gather(layer + 1) if layer < 3 else (None, None)
            result = pool_fwd(xs, g_row, p_in, p_mix, scale, w_out, carry=exchange)
            (x_new, p, z), landed = result if exchange is not None else (result, None)
            saved.append((xs, g_row, p, z, p_in, p_mix, scale, w_out))
        if exchange is not None:
            weights = placed(own, landed)
        xs = x_new

    dx, sq, d_final_g = loss_head(xs, final_g[None, :], loss_target[0])
    d_norm, d_scale, d_sinks, sent, pending = [None] * 4, [None] * 2, [None] * 2, {}, None
    for layer in reversed(range(4)):
        exchange = reduce_exchange(pending[1]) if pending is not None else None
        if layer % 2 == 0:
            x_in, g_row, q_t, kv, kv_t, z_t, o_t, lse, w_t, w_out = saved[layer]
            (dq_t, dkv, dkv_halo, dz_t, dwout, dsink), landed = attn_bwd_core(
                dx, q_t, kv, kv_t, z_t, o_t, lse, w_out, attn_sinks[layer // 2], carry=exchange)
            dx, dwin_t, d_norm[layer] = attn_bwd_in(dq_t, dkv, dkv_halo, dz_t, x_in, dx, g_row, w_t)
            d_sinks[layer // 2] = dsink[0, :N_HEADS]
            mine = [dwin_t.reshape(N_CHIPS, ATTN_CHUNK, D_MODEL), dwout.reshape(N_CHIPS, -1, D_MODEL)]
        else:
            x_in, g_row, p, z, p_in, p_mix, scale, w_out = saved[layer]
            dp, dz, dwout, dwmix, d_scale[layer // 2] = pool_bwd_mix(dx, p, z, p_mix, scale, w_out)
            result = pool_bwd_in(dp, dz, x_in, dx, g_row, p_in, carry=exchange)
            (dx, dwin, d_norm[layer]), landed = result if exchange is not None else (result, None)
            mine = [dwin, dwmix, dwout.reshape(N_CHIPS, -1, D_MODEL)]
        if pending is not None:
            sent[pending[0]] = (pending[1], landed)
        pending = (layer, mine)
    sent[0] = (pending[1], _run_exchange("reduce_layer_0", reduce_exchange(pending[1])))
    grad_x = dx

    last = jnp.concatenate([d_sinks[0], d_sinks[1], jnp.sum(sq).reshape(1),
                            jnp.zeros((D_MODEL - 2 * N_HEADS - 1,), F32)])[None, :]
    small = jnp.concatenate(d_norm + [d_final_g] + d_scale + [last], axis=0)
    small = all_reduce_small(small)
    loss = 0.5 * small[7, 2 * N_HEADS] / D_MODEL
    g_norm = small[0:4]
    g_final = small[4]
    g_scale = lax.dynamic_slice(small[5:7], (0, chip * 256), (2, 256))
    g_sinks = small[7, :2 * N_HEADS].reshape(2, N_HEADS)

    shares, landed = [], []
    for layer in (0, 1, 2, 3):
        shares += sent[layer][0]
        landed += sent[layer][1]
    reduced = sum_landed("sum_landed", shares, landed, chip, cc)
    from_sibling = _run_exchange("swap_reduced", swap_exchange(reduced))
    whole = [jnp.where(cc == 0, jnp.concatenate([mine, theirs]), jnp.concatenate([theirs, mine]))
             for mine, theirs in zip(reduced, from_sibling)]
    g_blocks = [jnp.stack([whole[t], whole[len(BIG) + t]]) for t in range(len(BIG))]
    g_blocks[0] = g_blocks[0].transpose(0, 2, 1)

    shapes = [w.shape for w in (attn_w_in, attn_w_out, pool_w_in, pool_w_mix, pool_w_out)]
    d_blocks, m_blocks, v_blocks = adamw(
        "adamw", w_blocks, g_blocks, blocks_2d(m_attn_w_in, m_attn_w_out, m_pool_w_in, m_pool_w_mix, m_pool_w_out),
        blocks_2d(v_attn_w_in, v_attn_w_out, v_pool_w_in, v_pool_w_mix, v_pool_w_out), steps=4)
    g_big, d_big, m_big, v_big = (
        {name: b.reshape(s) for name, b, s in zip(BIG, blocks, shapes)} for blocks in (g_blocks, d_blocks, m_blocks, v_blocks))

    def small_pack(ng, fg, sc, sk):
        row7 = jnp.concatenate([sk.reshape(2 * N_HEADS), jnp.zeros((D_MODEL - 2 * N_HEADS,), F32)])[None, :]
        sc = jnp.concatenate([sc, jnp.zeros((2, D_MODEL - 256), F32)], axis=1)
        return jnp.concatenate([ng, fg[None, :], sc, row7], axis=0)

    packs = [small_pack(norm_g, final_g, pool_scale, attn_sinks), small_pack(g_norm, g_final, g_scale, g_sinks),
             small_pack(m_norm_g, m_final_g, m_pool_scale, m_attn_sinks), small_pack(v_norm_g, v_final_g, v_pool_scale, v_attn_sinks)]
    small_out = [out[0] for out in adamw("adamw_small", *[[p[None]] for p in packs], steps=1)]

    def small_unpack(t):
        t = t[0]
        return {"norm_g": t[0:4], "final_g": t[4], "pool_scale": t[5:7, :256], "attn_sinks": t[7, :2 * N_HEADS].reshape(2, N_HEADS)}

    d_small, m_small, v_small = (small_unpack(t) for t in small_out)
    g_small = {"norm_g": g_norm, "final_g": g_final, "pool_scale": g_scale, "attn_sinks": g_sinks}

    names = ["norm_g", "attn_w_in", "attn_sinks", "attn_w_out", "pool_w_in", "pool_w_mix", "pool_scale", "pool_w_out", "final_g"]

    def ordered(bigs, smalls):
        return [bigs[n] if n in bigs else smalls[n] for n in names]

    return (loss, grad_x[None], *ordered(g_big, g_small), *ordered(d_big, d_small), *ordered(m_big, m_small),
            *ordered(v_big, v_small))
```

```python
import math

import jax
import jax.numpy as jnp
from jax import lax
from jax.experimental import pallas as pl
from jax.experimental.pallas import tpu as pltpu

F32 = jnp.float32
BF16 = jnp.bfloat16

D_MODEL = 1024
N_HEADS = 16
N_KV_HEADS = 4
GROUP = N_HEADS // N_KV_HEADS
HEAD_DIM = 64
KV_WIDTH = N_KV_HEADS * HEAD_DIM
ATTN_BLOCK = 128
POOL_WINDOWS = (2, 4, 8, 16)
POOL_GC = 256
POOL_HALO = 16
EPS = 1e-6
N_CHIPS = 4
ATTN_CHUNK = 640
POOL_CHUNK = 512
ROW_TILE = 512
NEG_BIG = -1e30
VMEM_LIMIT_BYTES = 60 * 1024 * 1024

ADAM_LR = 0.001
ADAM_B1 = 0.9
ADAM_B2 = 0.999
ADAM_EPS = 1e-08
ADAM_WD = 0.01
ADAM_STEP = 10

MESH_ID = pl.DeviceIdType.MESH


def _dot(a, b):
    return jnp.dot(a, b, preferred_element_type=F32)


def _dot_nt(a, b):
    return lax.dot_general(a, b, (((1,), (1,)), ((), ())), preferred_element_type=F32)


def _dot_tn(a, b):
    return lax.dot_general(a, b, (((0,), (0,)), ((), ())), preferred_element_type=F32)


def _vmem():
    return pl.BlockSpec(memory_space=pltpu.VMEM)


def _rows(width, tile=ROW_TILE):
    return pl.BlockSpec((tile, width), lambda i: (i, 0))


def _params(semantics):
    return pltpu.CompilerParams(dimension_semantics=(semantics,), vmem_limit_bytes=VMEM_LIMIT_BYTES)


def _grid_call(name, body, tiles, in_specs, out_specs, out_shape, scratch_shapes, args, semantics, carry=None):
    if carry is None:
        return pl.pallas_call(body, name=name, grid=(tiles,), in_specs=in_specs, out_specs=out_specs, out_shape=out_shape,
                              scratch_shapes=scratch_shapes, compiler_params=_params(semantics))(*args)
    counts = [len(args), len(carry.inputs), len(out_shape), len(carry.out_shapes), len(scratch_shapes), 2]

    def wrapped(*refs):
        groups, at = [], 0
        for n in counts:
            groups.append(refs[at:at + n])
            at += n
        ins, c_in, outs, c_out, scratch, sems = groups
        step = pl.program_id(0)
        pl.when(step == 0)(lambda: carry.start(c_in, c_out, sems))
        if carry.mid is not None:
            pl.when(step == (3 * tiles) // 4)(lambda: carry.mid(c_in, c_out, sems))
        body(*ins, *outs, *scratch)
        pl.when(step == tiles - 1)(lambda: carry.finish(c_in, c_out, sems))

    results = pl.pallas_call(
        wrapped, name=name, grid=(tiles,),
        in_specs=list(in_specs) + [_any()] * len(carry.inputs),
        out_specs=list(out_specs) + [_any()] * len(carry.out_shapes),
        out_shape=list(out_shape) + carry.out_shapes,
        scratch_shapes=list(scratch_shapes) + [pltpu.SemaphoreType.DMA((carry.n_remote,))] * 2,
        compiler_params=_params("arbitrary"))(*args, *carry.inputs)
    return results[:len(out_shape)], results[len(out_shape):]


def _norm(xf, g):
    rstd = lax.rsqrt(jnp.mean(xf * xf, axis=-1, keepdims=True) + EPS)
    xhat = xf * rstd
    return xhat, rstd, xhat * g


def _norm_bwd(dh, xhat, rstd, g):
    dg = jnp.sum(dh * xhat, axis=0, keepdims=True)
    dxhat = dh * g
    dx = rstd * (dxhat - xhat * jnp.mean(dxhat * xhat, axis=-1, keepdims=True))
    return dx, dg


def _silu_parts(zf):
    sig = jax.nn.sigmoid(zf)
    return zf * sig, sig * (1.0 + zf * (1.0 - sig))


def _cols(height, tile=ROW_TILE):
    return pl.BlockSpec((height, tile), lambda i: (0, i))


def _halo_prev_rows(width):
    per_tile = ROW_TILE // ATTN_BLOCK
    return pl.BlockSpec((ATTN_BLOCK, width), lambda i: (jnp.maximum(i * per_tile - 1, 0), 0))


def _halo_prev_cols(height):
    per_tile = ROW_TILE // ATTN_BLOCK
    return pl.BlockSpec((height, ATTN_BLOCK), lambda i: (0, jnp.maximum(i * per_tile - 1, 0)))


KV_AT = D_MODEL
Z_AT = D_MODEL + 2 * KV_WIDTH


def attn_in_fwd(x, g, w_t, carry=None):
    T = x.shape[0]

    def body(x_ref, g_ref, w_ref, qt_ref, kv_ref, kvt_ref, zt_ref):
        _, _, h = _norm(x_ref[...], g_ref[...])
        h = h.astype(BF16)
        scale = 1.0 / math.sqrt(HEAD_DIM)
        qt_ref[...] = (_dot_nt(w_ref[:KV_AT], h) * scale).astype(BF16)
        zt_ref[...] = _dot_nt(w_ref[Z_AT:], h).astype(BF16)
        kvt_ref[...] = _dot_nt(w_ref[KV_AT:Z_AT], h).astype(BF16)
        kv_ref[...] = _dot_nt(h, w_ref[KV_AT:Z_AT]).astype(BF16)

    return _grid_call(
        "attn_in_fwd", body, T // ROW_TILE,
        in_specs=[_rows(D_MODEL), _vmem(), _vmem()],
        out_specs=[_cols(D_MODEL), _rows(2 * KV_WIDTH), _cols(2 * KV_WIDTH), _cols(D_MODEL)],
        out_shape=[
            jax.ShapeDtypeStruct((D_MODEL, T), BF16),
            jax.ShapeDtypeStruct((T, 2 * KV_WIDTH), BF16),
            jax.ShapeDtypeStruct((2 * KV_WIDTH, T), BF16),
            jax.ShapeDtypeStruct((D_MODEL, T), BF16),
        ],
        scratch_shapes=[], args=(x, g, w_t), semantics="parallel", carry=carry)


def _causal_triangle():
    shape = (ATTN_BLOCK, GROUP * ATTN_BLOCK)
    kj = lax.broadcasted_iota(jnp.int32, shape, 0)
    qi = lax.broadcasted_iota(jnp.int32, shape, 1) & (ATTN_BLOCK - 1)
    return kj <= qi


def _group_cols(ref, hk, cols):
    return jnp.concatenate(
        [ref[(hk * GROUP + gi) * HEAD_DIM:(hk * GROUP + gi + 1) * HEAD_DIM, cols] for gi in range(GROUP)], axis=1)


def _group_row(values):
    return jnp.concatenate(values, axis=1)


def _window_rows(ref, halo_ref, b, lanes):
    if b == 0:
        return jnp.concatenate([halo_ref[:, lanes], ref[:ATTN_BLOCK, lanes]], axis=0)
    return ref[(b - 1) * ATTN_BLOCK:(b + 1) * ATTN_BLOCK, lanes]


def _window_cols(ref, halo_ref, b, rows):
    if b == 0:
        return jnp.concatenate([halo_ref[rows, :], ref[rows, :ATTN_BLOCK]], axis=1)
    return ref[rows, (b - 1) * ATTN_BLOCK:(b + 1) * ATTN_BLOCK]


def _select_window(both, tri, b=None, first_penalty=None):
    prev = both[:ATTN_BLOCK]
    if b == 0:
        prev = prev + first_penalty
    return jnp.where(tri, both[ATTN_BLOCK:], prev)


def _split_window(slots, tri):
    zero = jnp.zeros_like(slots)
    return jnp.concatenate([jnp.where(tri, zero, slots), jnp.where(tri, slots, zero)], axis=0)


def attn_core_fwd(q_t, kv, kv_t, z_t, x, w_out, sinks, carry=None):
    T = x.shape[0]
    blocks = ROW_TILE // ATTN_BLOCK

    def body(qt_ref, kv_ref, kvh_ref, kvt_ref, kvth_ref, zt_ref, x_ref, w_ref, sink_ref, xo_ref, ot_ref, lse_ref, oacc):
        tri = _causal_triangle()
        first_penalty = jnp.where(pl.program_id(0) == 0, NEG_BIG, 0.0)
        for hk in range(N_KV_HEADS):
            heads = [hk * GROUP + gi for gi in range(GROUP)]
            sink = _group_row([jnp.full((1, ATTN_BLOCK), sink_ref[h], F32) for h in heads])
            k_lanes = slice(hk * HEAD_DIM, (hk + 1) * HEAD_DIM)
            v_rows = slice(KV_WIDTH + hk * HEAD_DIM, KV_WIDTH + (hk + 1) * HEAD_DIM)
            for b in range(blocks):
                cols = slice(b * ATTN_BLOCK, (b + 1) * ATTN_BLOCK)
                qt = _group_cols(qt_ref, hk, cols)
                s = _select_window(_dot(_window_rows(kv_ref, kvh_ref, b, k_lanes), qt), tri, b, first_penalty)
                m = jnp.maximum(jnp.max(s, axis=0, keepdims=True), sink)
                p = jnp.exp(s - m)
                denom = jnp.sum(p, axis=0, keepdims=True) + jnp.exp(sink - m)
                o = _dot(_window_cols(kvt_ref, kvth_ref, b, v_rows), _split_window(p.astype(BF16), tri)) * (1.0 / denom)
                lse = m + jnp.log(denom)
                for gi, h in enumerate(heads):
                    part = slice(gi * ATTN_BLOCK, (gi + 1) * ATTN_BLOCK)
                    oacc[h * HEAD_DIM:(h + 1) * HEAD_DIM, cols] = o[:, part]
                    lse_ref[h:h + 1, cols] = lse[:, part]
        of = oacc[...]
        silu, _ = _silu_parts(zt_ref[...].astype(F32))
        y = _dot_tn((of * silu).astype(BF16), w_ref[...])
        xo_ref[...] = x_ref[...] + y
        ot_ref[...] = of.astype(BF16)

    return _grid_call(
        "attn_core_fwd", body, T // ROW_TILE,
        in_specs=[_cols(D_MODEL), _rows(2 * KV_WIDTH), _halo_prev_rows(2 * KV_WIDTH), _cols(2 * KV_WIDTH),
                  _halo_prev_cols(2 * KV_WIDTH), _cols(D_MODEL), _rows(D_MODEL), _vmem(),
                  pl.BlockSpec(memory_space=pltpu.SMEM)],
        out_specs=[_rows(D_MODEL), _cols(D_MODEL), _cols(N_HEADS)],
        out_shape=[
            jax.ShapeDtypeStruct((T, D_MODEL), F32),
            jax.ShapeDtypeStruct((D_MODEL, T), BF16),
            jax.ShapeDtypeStruct((N_HEADS, T), F32),
        ],
        scratch_shapes=[pltpu.VMEM((D_MODEL, ROW_TILE), F32)],
        args=(q_t, kv, kv, kv_t, kv_t, z_t, x, w_out, sinks), semantics="parallel", carry=carry)


def _inv_count(first_row, rows, window):
    t = first_row + lax.broadcasted_iota(jnp.int32, (rows, 1), 0)
    return 1.0 / jnp.minimum(t + 1, window).astype(F32)


MIX_ROWS = POOL_GC // N_CHIPS


def _mix_groups(wmix_ref):
    return [jnp.concatenate([wmix_ref[k, gi * MIX_ROWS:(gi + 1) * MIX_ROWS, :] for k in range(N_CHIPS)], axis=0)
            for gi in range(len(POOL_WINDOWS))]


def pool_fwd(x, g, w_in, w_mix, scale, w_out, carry=None):
    T = x.shape[0]

    def body(x_ref, g_ref, win_ref, wmix_ref, scale_ref, wout_ref, xo_ref, p_ref, z_ref, carry):
        i = pl.program_id(0)
        wmix = _mix_groups(wmix_ref)

        @pl.when(i == 0)
        def _():
            carry[...] = jnp.zeros_like(carry)

        _, _, h = _norm(x_ref[...], g_ref[...])
        h = h.astype(BF16)
        u = jnp.concatenate([_dot(h, win_ref[0]), _dot(h, win_ref[1])], axis=1)
        z = jnp.concatenate([_dot(h, win_ref[2]), _dot(h, win_ref[3])], axis=1)
        ext = jnp.concatenate([carry[...], u], axis=0)
        carry[...] = u[ROW_TILE - POOL_HALO:]
        mixed = []
        for gi, window in enumerate(POOL_WINDOWS):
            cols = slice(gi * POOL_GC, (gi + 1) * POOL_GC)
            s = ext[:, cols]
            shift = 1
            while shift < window:
                s = s + pltpu.roll(s, shift, 0)
                shift *= 2
            p = s[POOL_HALO:] * _inv_count(i * ROW_TILE, ROW_TILE, window) - u[:, cols]
            p = p.astype(BF16)
            p_ref[:, cols] = p
            mixed.append(_dot(p, wmix[gi]))
        m = jnp.concatenate(mixed, axis=1) * scale_ref[...]
        silu, _ = _silu_parts(z)
        y = _dot((m * silu).astype(BF16), wout_ref[...])
        xo_ref[...] = x_ref[...] + y
        z_ref[...] = z.astype(BF16)

    return _grid_call(
        "pool_fwd", body, T // ROW_TILE,
        in_specs=[_rows(D_MODEL), _vmem(), _vmem(), _vmem(), _vmem(), _vmem()],
        out_specs=[_rows(D_MODEL), _rows(D_MODEL), _rows(D_MODEL)],
        out_shape=[
            jax.ShapeDtypeStruct((T, D_MODEL), F32),
            jax.ShapeDtypeStruct((T, D_MODEL), BF16),
            jax.ShapeDtypeStruct((T, D_MODEL), BF16),
        ],
        scratch_shapes=[pltpu.VMEM((POOL_HALO, D_MODEL), F32)],
        args=(x, g, w_in, w_mix, scale, w_out), semantics="arbitrary", carry=carry)


def loss_head(x, g, target):
    T = x.shape[0]

    def body(x_ref, g_ref, t_ref, dx_ref, sq_ref, dg_ref):
        @pl.when(pl.program_id(0) == 0)
        def _():
            sq_ref[...] = jnp.zeros_like(sq_ref)
            dg_ref[...] = jnp.zeros_like(dg_ref)

        gv = g_ref[...]
        xhat, rstd, out = _norm(x_ref[...], gv)
        err = out - t_ref[...]
        sq_ref[...] += jnp.sum(err * err, axis=0, keepdims=True)
        dx, dg = _norm_bwd(err * (1.0 / D_MODEL), xhat, rstd, gv)
        dx_ref[...] = dx
        dg_ref[...] += dg

    return pl.pallas_call(
        body,
        name="loss_head",
        grid=(T // ROW_TILE,),
        in_specs=[_rows(D_MODEL), _vmem(), _rows(D_MODEL)],
        out_specs=[_rows(D_MODEL), _vmem(), _vmem()],
        out_shape=[
            jax.ShapeDtypeStruct((T, D_MODEL), F32),
            jax.ShapeDtypeStruct((1, D_MODEL), F32),
            jax.ShapeDtypeStruct((1, D_MODEL), F32),
        ],
        compiler_params=_params("arbitrary"),
    )(x, g, target)


def pool_bwd_mix(dy, p, z, w_mix, scale, w_out):
    T = dy.shape[0]
    tiles = T // ROW_TILE

    def body(dy_ref, p_ref, z_ref, wmix_ref, scale_ref, wout_ref, dp_ref, dz_ref, dwout_out, dwmix_out, dscale_ref,
             dwout_ref, dwmix_ref):
        @pl.when(pl.program_id(0) == 0)
        def _():
            dwout_ref[...] = jnp.zeros_like(dwout_ref)
            dwmix_ref[...] = jnp.zeros_like(dwmix_ref)
            dscale_ref[...] = jnp.zeros_like(dscale_ref)

        wmix = _mix_groups(wmix_ref)
        dyb = dy_ref[...].astype(BF16)
        da = _dot_nt(dyb, wout_ref[...])
        m_pre = jnp.concatenate(
            [_dot(p_ref[:, gi * POOL_GC:(gi + 1) * POOL_GC], wmix[gi]) for gi in range(len(POOL_WINDOWS))], axis=1)
        sc = scale_ref[...]
        m = m_pre * sc
        zf = z_ref[...].astype(F32)
        silu, dsilu = _silu_parts(zf)
        dwout_ref[...] += _dot_tn((m * silu).astype(BF16), dyb)
        dm = da * silu
        dz_ref[...] = (da * m * dsilu).astype(BF16)
        dscale_ref[...] += jnp.sum(dm * m_pre, axis=0, keepdims=True)
        dmp = (dm * sc).astype(BF16)
        for gi in range(len(POOL_WINDOWS)):
            cols = slice(gi * POOL_GC, (gi + 1) * POOL_GC)
            dw = _dot_tn(p_ref[:, cols], dmp[:, cols])
            for k in range(N_CHIPS):
                dwmix_ref[k, gi * MIX_ROWS:(gi + 1) * MIX_ROWS, :] += dw[k * MIX_ROWS:(k + 1) * MIX_ROWS]
            dp_ref[:, cols] = _dot_nt(dmp[:, cols], wmix[gi]).astype(BF16)

        @pl.when(pl.program_id(0) == tiles - 1)
        def _():
            dwout_out[...] = dwout_ref[...].astype(BF16)
            dwmix_out[...] = dwmix_ref[...].astype(BF16)

    return pl.pallas_call(
        body,
        name="pool_bwd_mix",
        grid=(tiles,),
        in_specs=[_rows(D_MODEL), _rows(D_MODEL), _rows(D_MODEL), _vmem(), _vmem(), _vmem()],
        out_specs=[_rows(D_MODEL), _rows(D_MODEL), _vmem(), _vmem(), _vmem()],
        out_shape=[
            jax.ShapeDtypeStruct((T, D_MODEL), BF16),
            jax.ShapeDtypeStruct((T, D_MODEL), BF16),
            jax.ShapeDtypeStruct((D_MODEL, D_MODEL), BF16),
            jax.ShapeDtypeStruct((N_CHIPS, POOL_GC, POOL_GC), BF16),
            jax.ShapeDtypeStruct((1, D_MODEL), F32),
        ],
        scratch_shapes=[pltpu.VMEM((D_MODEL, D_MODEL), F32), pltpu.VMEM((N_CHIPS, POOL_GC, POOL_GC), F32)],
        compiler_params=_params("arbitrary"),
    )(dy, p, z, w_mix, scale, w_out)


def pool_bwd_in(dp, dz, x, dy, g, w_in, carry=None):
    T = x.shape[0]
    tiles = T // ROW_TILE
    halo_blocks = ROW_TILE // POOL_HALO
    last_halo = T // POOL_HALO - 1

    def body(dp_ref, dph_ref, dz_ref, x_ref, dy_ref, g_ref, win_ref, dx_ref, dwin_out, dg_ref, dwin_ref):
        i = pl.program_id(0)

        @pl.when(i == 0)
        def _():
            dwin_ref[...] = jnp.zeros_like(dwin_ref)
            dg_ref[...] = jnp.zeros_like(dg_ref)

        rows = ROW_TILE + POOL_HALO
        ext = jnp.concatenate([dp_ref[...], dph_ref[...]], axis=0).astype(F32)
        t = i * ROW_TILE + lax.broadcasted_iota(jnp.int32, (rows, 1), 0)
        inside = (t < T).astype(F32)
        du = []
        for gi, window in enumerate(POOL_WINDOWS):
            cols = slice(gi * POOL_GC, (gi + 1) * POOL_GC)
            s = ext[:, cols] * (_inv_count(i * ROW_TILE, rows, window) * inside)
            shift = 1
            while shift < window:
                s = s + pltpu.roll(s, rows - shift, 0)
                shift *= 2
            du.append(s[:ROW_TILE] - ext[:ROW_TILE, cols])
        du = jnp.concatenate(du, axis=1).astype(BF16)
        chunks = [du[:, :POOL_CHUNK], du[:, POOL_CHUNK:], dz_ref[:, :POOL_CHUNK], dz_ref[:, POOL_CHUNK:]]
        gv = g_ref[...]
        xhat, rstd, h = _norm(x_ref[...], gv)
        h = h.astype(BF16)
        dh = jnp.zeros((ROW_TILE, D_MODEL), F32)
        for c in range(N_CHIPS):
            dwin_ref[c] += _dot_tn(h, chunks[c])
            dh = dh + _dot_nt(chunks[c], win_ref[c])
        dx, dg = _norm_bwd(dh, xhat, rstd, gv)
        dx_ref[...] = dx + dy_ref[...]
        dg_ref[...] += dg

        @pl.when(i == tiles - 1)
        def _():
            dwin_out[...] = dwin_ref[...].astype(BF16)

    return _grid_call(
        "pool_bwd_in", body, tiles,
        in_specs=[_rows(D_MODEL),
                  pl.BlockSpec((POOL_HALO, D_MODEL), lambda i: (jnp.minimum((i + 1) * halo_blocks, last_halo), 0)),
                  _rows(D_MODEL), _rows(D_MODEL), _rows(D_MODEL), _vmem(), _vmem()],
        out_specs=[_rows(D_MODEL), _vmem(), _vmem()],
        out_shape=[
            jax.ShapeDtypeStruct((T, D_MODEL), F32),
            jax.ShapeDtypeStruct((N_CHIPS, D_MODEL, POOL_CHUNK), BF16),
            jax.ShapeDtypeStruct((1, D_MODEL), F32),
        ],
        scratch_shapes=[pltpu.VMEM((N_CHIPS, D_MODEL, POOL_CHUNK), F32)],
        args=(dp, dp, dz, x, dy, g, w_in), semantics="arbitrary", carry=carry)


def attn_bwd_core(dy, q_t, kv, kv_t, z_t, o_t, lse, w_out, sinks, carry=None):
    T = dy.shape[0]
    tiles = T // ROW_TILE
    blocks = ROW_TILE // ATTN_BLOCK

    def body(dy_ref, qt_ref, kv_ref, kvh_ref, kvt_ref, kvth_ref, zt_ref, ot_ref, lse_ref, w_ref, sink_ref,
             dqt_ref, dkv_ref, dkvh_ref, dzt_ref, dwout_out, dsink_ref, do_s, dkv_s, dwout_ref):
        @pl.when(pl.program_id(0) == 0)
        def _():
            dwout_ref[...] = jnp.zeros_like(dwout_ref)
            dsink_ref[...] = jnp.zeros_like(dsink_ref)

        dyb = dy_ref[...].astype(BF16)
        da = _dot_nt(w_ref[...], dyb)
        of = ot_ref[...].astype(F32)
        silu, dsilu = _silu_parts(zt_ref[...].astype(F32))
        dwout_ref[...] += _dot((of * silu).astype(BF16), dyb)
        do = da * silu
        dzt_ref[...] = (da * of * dsilu).astype(BF16)
        do_s[...] = do.astype(BF16)
        dof = do * of
        dkv_s[...] = jnp.zeros_like(dkv_s)
        tri = _causal_triangle()
        first_penalty = jnp.where(pl.program_id(0) == 0, NEG_BIG, 0.0)
        for hk in range(N_KV_HEADS):
            heads = [hk * GROUP + gi for gi in range(GROUP)]
            sink = _group_row([jnp.full((1, ATTN_BLOCK), sink_ref[h], F32) for h in heads])
            deltas = [jnp.sum(dof[h * HEAD_DIM:(h + 1) * HEAD_DIM], axis=0, keepdims=True) for h in heads]
            k_lanes = slice(hk * HEAD_DIM, (hk + 1) * HEAD_DIM)
            v_lanes = slice(KV_WIDTH + hk * HEAD_DIM, KV_WIDTH + (hk + 1) * HEAD_DIM)
            for b in range(blocks):
                cols = slice(b * ATTN_BLOCK, (b + 1) * ATTN_BLOCK)
                window = slice(b * ATTN_BLOCK, (b + 2) * ATTN_BLOCK)
                qt = _group_cols(qt_ref, hk, cols)
                dot = _group_cols(do_s, hk, cols)
                lse_row = _group_row([lse_ref[h:h + 1, cols] for h in heads])
                delta = _group_row([d[:, cols] for d in deltas])
                s = _select_window(_dot(_window_rows(kv_ref, kvh_ref, b, k_lanes), qt), tri, b, first_penalty)
                p = jnp.exp(s - lse_row)
                dp = _select_window(_dot(_window_rows(kv_ref, kvh_ref, b, v_lanes), dot), tri)
                ds = _split_window((p * (dp - delta)).astype(BF16), tri)
                dq = _dot(_window_cols(kvt_ref, kvth_ref, b, k_lanes), ds) * (1.0 / math.sqrt(HEAD_DIM))
                dkv_s[window, k_lanes] += _dot_nt(ds, qt)
                dkv_s[window, v_lanes] += _dot_nt(_split_window(p.astype(BF16), tri), dot)
                dsink = -jnp.exp(sink - lse_row) * delta
                for gi, h in enumerate(heads):
                    part = slice(gi * ATTN_BLOCK, (gi + 1) * ATTN_BLOCK)
                    dqt_ref[h * HEAD_DIM:(h + 1) * HEAD_DIM, cols] = dq[:, part].astype(BF16)
                    dsink_ref[0:1, h:h + 1] += jnp.sum(dsink[:, part], axis=1, keepdims=True)
        dkv_ref[...] = dkv_s[ATTN_BLOCK:]
        dkvh_ref[0] = dkv_s[:ATTN_BLOCK]

        @pl.when(pl.program_id(0) == tiles - 1)
        def _():
            dwout_out[...] = dwout_ref[...].astype(BF16)

    return _grid_call(
        "attn_bwd_core", body, tiles,
        in_specs=[_rows(D_MODEL), _cols(D_MODEL), _rows(2 * KV_WIDTH), _halo_prev_rows(2 * KV_WIDTH),
                  _cols(2 * KV_WIDTH), _halo_prev_cols(2 * KV_WIDTH), _cols(D_MODEL), _cols(D_MODEL), _cols(N_HEADS),
                  _vmem(), pl.BlockSpec(memory_space=pltpu.SMEM)],
        out_specs=[_cols(D_MODEL), _rows(2 * KV_WIDTH), pl.BlockSpec((1, ATTN_BLOCK, 2 * KV_WIDTH), lambda i: (i, 0, 0)),
                   _cols(D_MODEL), _vmem(), _vmem()],
        out_shape=[
            jax.ShapeDtypeStruct((D_MODEL, T), BF16),
            jax.ShapeDtypeStruct((T, 2 * KV_WIDTH), F32),
            jax.ShapeDtypeStruct((tiles, ATTN_BLOCK, 2 * KV_WIDTH), F32),
            jax.ShapeDtypeStruct((D_MODEL, T), BF16),
            jax.ShapeDtypeStruct((D_MODEL, D_MODEL), BF16),
            jax.ShapeDtypeStruct((8, 128), F32),
        ],
        scratch_shapes=[
            pltpu.VMEM((D_MODEL, ROW_TILE), BF16),
            pltpu.VMEM((ROW_TILE + ATTN_BLOCK, 2 * KV_WIDTH), F32),
            pltpu.VMEM((D_MODEL, D_MODEL), F32),
        ],
        args=(dy, q_t, kv, kv, kv_t, kv_t, z_t, o_t, lse, w_out, sinks), semantics="arbitrary", carry=carry)


def attn_bwd_in(dq_t, dkv, dkv_halo, dz_t, x, dy, g, w_t, carry=None):
    T = x.shape[0]
    tiles = T // ROW_TILE
    kv_at, z_at = KV_AT, Z_AT

    def body(dqt_ref, dkv_ref, dkvh_ref, dzt_ref, x_ref, dy_ref, g_ref, w_ref, dx_ref, dwt_out, dg_ref, dwt_ref):
        i = pl.program_id(0)

        @pl.when(i == 0)
        def _():
            dwt_ref[...] = jnp.zeros_like(dwt_ref)
            dg_ref[...] = jnp.zeros_like(dg_ref)

        more = (i < tiles - 1).astype(F32)
        tail = jnp.concatenate([jnp.zeros((ROW_TILE - ATTN_BLOCK, 2 * KV_WIDTH), F32), dkvh_ref[0] * more], axis=0)
        dkvb = (dkv_ref[...] + tail).astype(BF16)
        gv = g_ref[...]
        xhat, rstd, h = _norm(x_ref[...], gv)
        h = h.astype(BF16)
        dqt = dqt_ref[...]
        dzt = dzt_ref[...]
        dwt_ref[:kv_at] += _dot(dqt, h)
        dwt_ref[kv_at:z_at] += _dot_tn(dkvb, h)
        dwt_ref[z_at:] += _dot(dzt, h)
        dh = _dot_tn(dqt, w_ref[:kv_at]) + _dot_tn(dzt, w_ref[z_at:]) + _dot(dkvb, w_ref[kv_at:z_at])
        dx, dg = _norm_bwd(dh, xhat, rstd, gv)
        dx_ref[...] = dx + dy_ref[...]
        dg_ref[...] += dg

        @pl.when(i == tiles - 1)
        def _():
            dwt_out[...] = dwt_ref[...].astype(BF16)

    halo_next = pl.BlockSpec((1, ATTN_BLOCK, 2 * KV_WIDTH), lambda i: (jnp.minimum(i + 1, tiles - 1), 0, 0))
    return _grid_call(
        "attn_bwd_in", body, tiles,
        in_specs=[_cols(D_MODEL), _rows(2 * KV_WIDTH), halo_next, _cols(D_MODEL), _rows(D_MODEL), _rows(D_MODEL),
                  _vmem(), _vmem()],
        out_specs=[_rows(D_MODEL), _vmem(), _vmem()],
        out_shape=[
            jax.ShapeDtypeStruct((T, D_MODEL), F32),
            jax.ShapeDtypeStruct((N_CHIPS * ATTN_CHUNK, D_MODEL), BF16),
            jax.ShapeDtypeStruct((1, D_MODEL), F32),
        ],
        scratch_shapes=[pltpu.VMEM((N_CHIPS * ATTN_CHUNK, D_MODEL), F32)],
        args=(dq_t, dkv, dkv_halo, dz_t, x, dy, g, w_t), semantics="arbitrary", carry=carry)


BIG = ("attn_w_in", "attn_w_out", "pool_w_in", "pool_w_mix", "pool_w_out")


def _half(ref, h, axis=0):
    rows = ref.shape[axis] // 2
    index = (slice(None),) * axis + (pl.ds(h * rows, rows),)
    return ref.at[index]


def _mesh_position():
    return lax.axis_index("x"), lax.axis_index("y"), lax.axis_index("c")


def _other_chips(x, y):
    return [(1 - x, y), (x, 1 - y), (1 - x, 1 - y)]


def _any():
    return pl.BlockSpec(memory_space=pl.ANY)


def _remote(src, dst, sems, index, to):
    send_sems, recv_sems = sems
    return pltpu.make_async_remote_copy(src_ref=src, dst_ref=dst, send_sem=send_sems.at[index], recv_sem=recv_sems.at[index],
                                        device_id=to, device_id_type=MESH_ID)


class Exchange:
    def __init__(self, inputs, out_shapes, n_remote, start, finish, mid=None):
        self.inputs, self.out_shapes, self.n_remote = list(inputs), list(out_shapes), n_remote
        self.start, self.finish, self.mid = start, finish, mid


def _run_exchange(name, ex):
    n_in, n_out = len(ex.inputs), len(ex.out_shapes)

    def kernel_body(*refs):
        ins, outs, sems = refs[:n_in], refs[n_in:n_in + n_out], refs[n_in + n_out:]
        ex.start(ins, outs, sems)
        if ex.mid is not None:
            ex.mid(ins, outs, sems)
        ex.finish(ins, outs, sems)

    return pl.pallas_call(
        kernel_body,
        name=name,
        in_specs=[_any()] * n_in,
        out_specs=[_any()] * n_out,
        out_shape=ex.out_shapes,
        scratch_shapes=[pltpu.SemaphoreType.DMA((ex.n_remote,))] * 2,
    )(*ex.inputs)


def gather_exchange(blocks, whole):
    def copies(ins, outs, sems):
        x, y, c = _mesh_position()
        me = 2 * x + y
        sibling = (x, y, 1 - c)
        table = []
        for t, (w_ref, out_ref) in enumerate(zip(ins, outs)):
            for j, (px, py) in enumerate(_other_chips(x, y)):
                peer = 2 * px + py
                if t in whole:
                    table.append((_remote(w_ref, out_ref.at[me], sems, 6 * t + j, (px, py, c)),
                                  _remote(w_ref, out_ref.at[peer], sems, 6 * t + j, (px, py, c)), None, None))
                    continue
                landed = _half(out_ref.at[peer], c)
                from_sibling = _half(out_ref.at[peer], 1 - c)
                table.append((_remote(_half(w_ref, c), _half(out_ref.at[me], c), sems, 6 * t + j, (px, py, c)),
                              _remote(landed, landed, sems, 6 * t + j, (px, py, c)),
                              _remote(landed, landed, sems, 6 * t + 3 + j, sibling),
                              _remote(from_sibling, from_sibling, sems, 6 * t + 3 + j, sibling)))
        return table

    def start(ins, outs, sems):
        for send, _, _, _ in copies(ins, outs, sems):
            send.start()

    def mid(ins, outs, sems):
        for _, arrival, forward, _ in copies(ins, outs, sems):
            if forward is not None:
                arrival.wait_recv()
                forward.start()

    def finish(ins, outs, sems):
        for send, arrival, forward, from_sibling in copies(ins, outs, sems):
            send.wait_send()
            if forward is None:
                arrival.wait_recv()
            else:
                forward.wait_send()
                from_sibling.wait_recv()

    shapes = [jax.ShapeDtypeStruct((N_CHIPS,) + b.shape, b.dtype) for b in blocks]
    return Exchange(blocks, shapes, 6 * len(blocks), start, finish, mid)


N_SENDERS = 7


def reduce_exchange(grads):
    def copies(ins, outs, sems):
        send_sems, recv_sems = sems
        x, y, c = _mesh_position()
        me = 2 * x + y
        sends, arrivals = [], []
        for t, (g_ref, out_ref) in enumerate(zip(ins, outs)):
            base = N_SENDERS * t

            def copy(src, slot, send_index, to):
                return pltpu.make_async_remote_copy(
                    src_ref=src, dst_ref=out_ref.at[slot], send_sem=send_sems.at[base + send_index],
                    recv_sem=recv_sems.at[base + slot], device_id=to, device_id_type=MESH_ID)

            mine = _half(g_ref.at[me], c)
            sends.append(copy(_half(g_ref.at[me], 1 - c), 0, 0, (x, y, 1 - c)))
            arrivals.append(copy(mine, 0, 0, (x, y, 1 - c)))
            for j, (px, py) in enumerate(_other_chips(x, y)):
                for h in range(2):
                    sends.append(copy(_half(g_ref.at[2 * px + py], h), 1 + 2 * j + c, 1 + 2 * j + h, (px, py, h)))
                    arrivals.append(copy(mine, 1 + 2 * j + h, 1 + 2 * j + h, (px, py, h)))
        return sends, arrivals

    def start(ins, outs, sems):
        for cp in copies(ins, outs, sems)[0]:
            cp.start()

    def finish(ins, outs, sems):
        sends, arrivals = copies(ins, outs, sems)
        for cp in sends:
            cp.wait_send()
        for cp in arrivals:
            cp.wait_recv()

    shapes = [jax.ShapeDtypeStruct((N_SENDERS, g.shape[1] // 2, g.shape[2]), g.dtype) for g in grads]
    return Exchange(grads, shapes, N_SENDERS * len(grads), start, finish)


def sum_landed(name, grads, landed, chip, core):
    n = len(grads)
    steps = 4

    def body(where_ref, *refs):
        for g_ref, l_ref, o_ref in zip(refs[:n], refs[n:2 * n], refs[2 * n:]):
            total = g_ref[0, 0].astype(F32)
            for s in range(N_SENDERS):
                total = total + l_ref[s].astype(F32)
            o_ref[...] = total

    halves = [g.reshape(N_CHIPS, 2, g.shape[1] // 2, g.shape[2]) for g in grads]

    def rows(h):
        return h.shape[2] // steps

    return pl.pallas_call(
        body,
        name=name,
        grid_spec=pltpu.PrefetchScalarGridSpec(
            num_scalar_prefetch=1, grid=(steps,),
            in_specs=[pl.BlockSpec((1, 1, rows(h), h.shape[3]), lambda r, where_ref: (where_ref[0], where_ref[1], r, 0))
                      for h in halves]
            + [pl.BlockSpec((N_SENDERS, rows(h), h.shape[3]), lambda r, where_ref: (0, r, 0)) for h in halves],
            out_specs=[pl.BlockSpec((rows(h), h.shape[3]), lambda r, where_ref: (r, 0)) for h in halves]),
        out_shape=[jax.ShapeDtypeStruct(h.shape[2:], F32) for h in halves],
        compiler_params=_params("parallel"),
    )(jnp.stack([chip, core]).astype(jnp.int32), *halves, *landed)


def swap_exchange(halves):
    def copies(ins, outs, sems):
        x, y, c = _mesh_position()
        return [_remote(h_ref, out_ref, sems, i, (x, y, 1 - c)) for i, (h_ref, out_ref) in enumerate(zip(ins, outs))]

    def start(ins, outs, sems):
        for cp in copies(ins, outs, sems):
            cp.start()

    def finish(ins, outs, sems):
        for cp in copies(ins, outs, sems):
            cp.wait()

    return Exchange(halves, [jax.ShapeDtypeStruct(h.shape, h.dtype) for h in halves], len(halves), start, finish)


def all_reduce_small(v):
    n_dev = 8

    def body(v_ref, out_ref, slots, send_sems, recv_sems):
        x, y, c = _mesh_position()
        me = 4 * x + 2 * y + c
        slots[me] = v_ref[...]
        peers = []
        for r in range(1, n_dev):
            fx, fy, fc = (r >> 2) & 1, (r >> 1) & 1, r & 1
            peers.append((1 - x if fx else x, 1 - y if fy else y, 1 - c if fc else c))
        sends = [
            pltpu.make_async_remote_copy(
                src_ref=v_ref, dst_ref=slots.at[me], send_sem=send_sems.at[r], recv_sem=recv_sems.at[r],
                device_id=peer, device_id_type=MESH_ID)
            for r, peer in enumerate(peers)
        ]
        for cp in sends:
            cp.start()
        for r, (px, py, pc) in enumerate(peers):
            pltpu.make_async_remote_copy(
                src_ref=v_ref, dst_ref=slots.at[4 * px + 2 * py + pc], send_sem=send_sems.at[r], recv_sem=recv_sems.at[r],
                device_id=(px, py, pc), device_id_type=MESH_ID).wait_recv()
        for cp in sends:
            cp.wait_send()
        total = slots[0]
        for d in range(1, n_dev):
            total = total + slots[d]
        out_ref[...] = total

    return pl.pallas_call(
        body,
        name="all_reduce_small",
        in_specs=[_vmem()],
        out_specs=_vmem(),
        out_shape=jax.ShapeDtypeStruct(v.shape, v.dtype),
        scratch_shapes=[pltpu.VMEM((n_dev,) + v.shape, v.dtype), pltpu.SemaphoreType.DMA((n_dev - 1,)),
                        pltpu.SemaphoreType.DMA((n_dev - 1,))],
    )(v)


def adamw(name, ws, gs, ms, vs, steps):
    n = len(ws)

    def body(*refs):
        ins, outs = refs[:4 * n], refs[4 * n:]
        for t in range(n):
            w_ref, g_ref, m_ref, v_ref = (ins[q * n + t] for q in range(4))
            d_ref, mo_ref, vo_ref = (outs[q * n + t] for q in range(3))
            gv = g_ref[...]
            m2 = ADAM_B1 * m_ref[...] + (1.0 - ADAM_B1) * gv
            v2 = ADAM_B2 * v_ref[...] + (1.0 - ADAM_B2) * (gv * gv)
            m_hat = m2 / (1.0 - ADAM_B1 ** ADAM_STEP)
            v_hat = v2 / (1.0 - ADAM_B2 ** ADAM_STEP)
            d_ref[...] = -ADAM_LR * (m_hat / (jnp.sqrt(v_hat) + ADAM_EPS) + ADAM_WD * w_ref[...])
            mo_ref[...] = m2
            vo_ref[...] = v2

    specs = [pl.BlockSpec((1, w.shape[1] // steps, w.shape[2]), lambda j, r: (j, r, 0)) for w in ws]
    shapes = [jax.ShapeDtypeStruct(w.shape, F32) for w in ws]
    out = pl.pallas_call(
        body, name=name, grid=(ws[0].shape[0], steps), in_specs=specs * 4, out_specs=specs * 3, out_shape=shapes * 3,
        compiler_params=pltpu.CompilerParams(dimension_semantics=("parallel", "parallel"), vmem_limit_bytes=VMEM_LIMIT_BYTES),
    )(*ws, *gs, *ms, *vs)
    return out[:n], out[n:2 * n], out[2 * n:]


def kernel(x, norm_g, attn_w_in, attn_sinks, attn_w_out, pool_w_in, pool_w_mix, pool_scale, pool_w_out, final_g, loss_target, m_norm_g, m_attn_w_in, m_attn_sinks, m_attn_w_out, m_pool_w_in, m_pool_w_mix, m_pool_scale, m_pool_w_out, m_final_g, v_norm_g, v_attn_w_in, v_attn_sinks, v_attn_w_out, v_pool_w_in, v_pool_w_mix, v_pool_scale, v_pool_w_out, v_final_g):
    cx, cy, cc = _mesh_position()
    chip = 2 * cx + cy

    def blocks_2d(a_in, a_out, p_in, p_mix, p_out):
        return [a_in, a_out, p_in, p_mix.reshape(2, POOL_GC, POOL_GC), p_out]

    w_blocks = blocks_2d(attn_w_in, attn_w_out, pool_w_in, pool_w_mix, pool_w_out)

    def gather(layer):
        j = layer // 2
        if layer % 2 == 0:
            own = [w_blocks[0][j].T.astype(BF16), w_blocks[1][j].astype(BF16)]
            return own, gather_exchange(own, whole=())
        own = [w_blocks[t][j].astype(BF16) for t in (2, 3, 4)] + [pool_scale[j][None, :]]
        return own, gather_exchange(own, whole=(3,))

    def placed(own, others):
        return [lax.dynamic_update_slice(o, mine[None], (chip, 0, 0)) for o, mine in zip(others, own)]

    own = gather(0)[0]
    weights = placed(own[:1], _run_exchange("gather_layer_0", gather_exchange(own[:1], whole=())))
    xs = x[0]
    saved = []
    for layer in range(4):
        g_row = norm_g[layer][None, :]
        if layer % 2 == 0:
            w_t = weights[0].reshape(N_CHIPS * ATTN_CHUNK, D_MODEL)
            if len(weights) == 1:
                (q_t, kv, kv_t, z_t), landed = attn_in_fwd(xs, g_row, w_t, carry=gather_exchange(own[1:], whole=()))
                weights += placed(own[1:], landed)
            else:
                q_t, kv, kv_t, z_t = attn_in_fwd(xs, g_row, w_t)
            w_out = weights[1].reshape(D_MODEL, D_MODEL)
            own, exchange = gather(layer + 1)
            (x_new, o_t, lse), landed = attn_core_fwd(q_t, kv, kv_t, z_t, xs, w_out, attn_sinks[layer // 2], carry=exchange)
            saved.append((xs, g_row, q_t, kv, kv_t, z_t, o_t, lse, w_t, w_out))
        else:
            p_in, p_mix, p_out, scale = weights
            scale, w_out = scale.reshape(1, D_MODEL), p_out.reshape(D_MODEL, D_MODEL)
            own, exchange = gather(layer + 1) if layer < 3 else (None, None)
            result = pool_fwd(xs, g_row, p_in, p_mix, scale, w_out, carry=exchange)
            (x_new, p, z), landed = result if exchange is not None else (result, None)
            saved.append((xs, g_row, p, z, p_in, p_mix, scale, w_out))
        if exchange is not None:
            weights = placed(own, landed)
        xs = x_new

    dx, sq, d_final_g = loss_head(xs, final_g[None, :], loss_target[0])
    d_norm, d_scale, d_sinks, sent, pending = [None] * 4, [None] * 2, [None] * 2, {}, None
    for layer in reversed(range(4)):
        exchange = reduce_exchange(pending[1]) if pending is not None else None
        if layer % 2 == 0:
            x_in, g_row, q_t, kv, kv_t, z_t, o_t, lse, w_t, w_out = saved[layer]
            (dq_t, dkv, dkv_halo, dz_t, dwout, dsink), landed = attn_bwd_core(
                dx, q_t, kv, kv_t, z_t, o_t, lse, w_out, attn_sinks[layer // 2], carry=exchange)
            dx, dwin_t, d_norm[layer] = attn_bwd_in(dq_t, dkv, dkv_halo, dz_t, x_in, dx, g_row, w_t)
            d_sinks[layer // 2] = dsink[0, :N_HEADS]
            mine = [dwin_t.reshape(N_CHIPS, ATTN_CHUNK, D_MODEL), dwout.reshape(N_CHIPS, -1, D_MODEL)]
        else:
            x_in, g_row, p, z, p_in, p_mix, scale, w_out = saved[layer]
            dp, dz, dwout, dwmix, d_scale[layer // 2] = pool_bwd_mix(dx, p, z, p_mix, scale, w_out)
            result = pool_bwd_in(dp, dz, x_in, dx, g_row, p_in, carry=exchange)
            (dx, dwin, d_norm[layer]), landed = result if exchange is not None else (result, None)
            mine = [dwin, dwmix, dwout.reshape(N_CHIPS, -1, D_MODEL)]
        if pending is not None:
            sent[pending[0]] = (pending[1], landed)
        pending = (layer, mine)
    sent[0] = (pending[1], _run_exchange("reduce_layer_0", reduce_exchange(pending[1])))
    grad_x = dx

    last = jnp.concatenate([d_sinks[0], d_sinks[1], jnp.sum(sq).reshape(1),
                            jnp.zeros((D_MODEL - 2 * N_HEADS - 1,), F32)])[None, :]
    small = jnp.concatenate(d_norm + [d_final_g] + d_scale + [last], axis=0)
    small = all_reduce_small(small)
    loss = 0.5 * small[7, 2 * N_HEADS] / D_MODEL
    g_norm = small[0:4]
    g_final = small[4]
    g_scale = lax.dynamic_slice(small[5:7], (0, chip * 256), (2, 256))
    g_sinks = small[7, :2 * N_HEADS].reshape(2, N_HEADS)

    shares, landed = [], []
    for layer in (0, 1, 2, 3):
        shares += sent[layer][0]
        landed += sent[layer][1]
    reduced = sum_landed("sum_landed", shares, landed, chip, cc)
    from_sibling = _run_exchange("swap_reduced", swap_exchange(reduced))
    whole = [jnp.where(cc == 0, jnp.concatenate([mine, theirs]), jnp.concatenate([theirs, mine]))
             for mine, theirs in zip(reduced, from_sibling)]
    g_blocks = [jnp.stack([whole[t], whole[len(BIG) + t]]) for t in range(len(BIG))]
    g_blocks[0] = g_blocks[0].transpose(0, 2, 1)

    shapes = [w.shape for w in (attn_w_in, attn_w_out, pool_w_in, pool_w_mix, pool_w_out)]
    d_blocks, m_blocks, v_blocks = adamw(
        "adamw", w_blocks, g_blocks, blocks_2d(m_attn_w_in, m_attn_w_out, m_pool_w_in, m_pool_w_mix, m_pool_w_out),
        blocks_2d(v_attn_w_in, v_attn_w_out, v_pool_w_in, v_pool_w_mix, v_pool_w_out), steps=4)
    g_big, d_big, m_big, v_big = (
        {name: b.reshape(s) for name, b, s in zip(BIG, blocks, shapes)} for blocks in (g_blocks, d_blocks, m_blocks, v_blocks))

    def small_pack(ng, fg, sc, sk):
        row7 = jnp.concatenate([sk.reshape(2 * N_HEADS), jnp.zeros((D_MODEL - 2 * N_HEADS,), F32)])[None, :]
        sc = jnp.concatenate([sc, jnp.zeros((2, D_MODEL - 256), F32)], axis=1)
        return jnp.concatenate([ng, fg[None, :], sc, row7], axis=0)

    packs = [small_pack(norm_g, final_g, pool_scale, attn_sinks), small_pack(g_norm, g_final, g_scale, g_sinks),
             small_pack(m_norm_g, m_final_g, m_pool_scale, m_attn_sinks), small_pack(v_norm_g, v_final_g, v_pool_scale, v_attn_sinks)]
    small_out = [out[0] for out in adamw("adamw_small", *[[p[None]] for p in packs], steps=1)]

    def small_unpack(t):
        t = t[0]
        return {"norm_g": t[0:4], "final_g": t[4], "pool_scale": t[5:7, :256], "attn_sinks": t[7, :2 * N_HEADS].reshape(2, N_HEADS)}

    d_small, m_small, v_small = (small_unpack(t) for t in small_out)
    g_small = {"norm_g": g_norm, "final_g": g_final, "pool_scale": g_scale, "attn_sinks": g_sinks}

    names = ["norm_g", "attn_w_in", "attn_sinks", "attn_w_out", "pool_w_in", "pool_w_mix", "pool_scale", "pool_w_out", "final_g"]

    def ordered(bigs, smalls):
        return [bigs[n] if n in bigs else smalls[n] for n in names]

    return (loss, grad_x[None], *ordered(g_big, g_small), *ordered(d_big, d_small), *ordered(m_big, m_small),
            *ordered(v_big, v_small))
```

```python
import math

import jax
import jax.numpy as jnp
from jax import lax
from jax.experimental import pallas as pl
from jax.experimental.pallas import tpu as pltpu

F32 = jnp.float32
BF16 = jnp.bfloat16

D_MODEL = 1024
N_HEADS = 16
N_KV_HEADS = 4
GROUP = N_HEADS // N_KV_HEADS
HEAD_DIM = 64
KV_WIDTH = N_KV_HEADS * HEAD_DIM
ATTN_BLOCK = 128
POOL_WINDOWS = (2, 4, 8, 16)
POOL_GC = 256
POOL_HALO = 16
EPS = 1e-6
N_CHIPS = 4
ATTN_CHUNK = 640
POOL_CHUNK = 512
ROW_TILE = 512
NEG_BIG = -1e30
VMEM_LIMIT_BYTES = 60 * 1024 * 1024

ADAM_LR = 0.001
ADAM_B1 = 0.9
ADAM_B2 = 0.999
ADAM_EPS = 1e-08
ADAM_WD = 0.01
ADAM_STEP = 10

MESH_ID = pl.DeviceIdType.MESH


def _dot(a, b):
    return jnp.dot(a, b, preferred_element_type=F32)


def _dot_nt(a, b):
    return lax.dot_general(a, b, (((1,), (1,)), ((), ())), preferred_element_type=F32)


def _dot_tn(a, b):
    return lax.dot_general(a, b, (((0,), (0,)), ((), ())), preferred_element_type=F32)


def _vmem():
    return pl.BlockSpec(memory_space=pltpu.VMEM)


def _rows(width, tile=ROW_TILE):
    return pl.BlockSpec((tile, width), lambda i: (i, 0))


def _params(semantics):
    return pltpu.CompilerParams(dimension_semantics=(semantics,), vmem_limit_bytes=VMEM_LIMIT_BYTES)


def _grid_call(name, body, tiles, in_specs, out_specs, out_shape, scratch_shapes, args, semantics, carry=None):
    if carry is None:
        return pl.pallas_call(body, name=name, grid=(tiles,), in_specs=in_specs, out_specs=out_specs, out_shape=out_shape,
                              scratch_shapes=scratch_shapes, compiler_params=_params(semantics))(*args)
    counts = [len(args), len(carry.inputs), len(out_shape), len(carry.out_shapes), len(scratch_shapes), 2]

    def wrapped(*refs):
        groups, at = [], 0
        for n in counts:
            groups.append(refs[at:at + n])
            at += n
        ins, c_in, outs, c_out, scratch, sems = groups
        step = pl.program_id(0)
        pl.when(step == 0)(lambda: carry.start(c_in, c_out, sems))
        if carry.mid is not None:
            pl.when(step == (3 * tiles) // 4)(lambda: carry.mid(c_in, c_out, sems))
        body(*ins, *outs, *scratch)
        pl.when(step == tiles - 1)(lambda: carry.finish(c_in, c_out, sems))

    results = pl.pallas_call(
        wrapped, name=name, grid=(tiles,),
        in_specs=list(in_specs) + [_any()] * len(carry.inputs),
        out_specs=list(out_specs) + [_any()] * len(carry.out_shapes),
        out_shape=list(out_shape) + carry.out_shapes,
        scratch_shapes=list(scratch_shapes) + [pltpu.SemaphoreType.DMA((carry.n_remote,))] * 2,
        compiler_params=_params("arbitrary"))(*args, *carry.inputs)
    return results[:len(out_shape)], results[len(out_shape):]


def _norm(xf, g):
    rstd = lax.rsqrt(jnp.mean(xf * xf, axis=-1, keepdims=True) + EPS)
    xhat = xf * rstd
    return xhat, rstd, xhat * g


def _norm_bwd(dh, xhat, rstd, g):
    dg = jnp.sum(dh * xhat, axis=0, keepdims=True)
    dxhat = dh * g
    dx = rstd * (dxhat - xhat * jnp.mean(dxhat * xhat, axis=-1, keepdims=True))
    return dx, dg


def _silu_parts(zf):
    sig = jax.nn.sigmoid(zf)
    return zf * sig, sig * (1.0 + zf * (1.0 - sig))


def _cols(height, tile=ROW_TILE):
    return pl.BlockSpec((height, tile), lambda i: (0, i))


def _halo_prev_rows(width):
    per_tile = ROW_TILE // ATTN_BLOCK
    return pl.BlockSpec((ATTN_BLOCK, width), lambda i: (jnp.maximum(i * per_tile - 1, 0), 0))


def _halo_prev_cols(height):
    per_tile = ROW_TILE // ATTN_BLOCK
    return pl.BlockSpec((height, ATTN_BLOCK), lambda i: (0, jnp.maximum(i * per_tile - 1, 0)))


KV_AT = D_MODEL
Z_AT = D_MODEL + 2 * KV_WIDTH


def attn_in_fwd(x, g, w_t, carry=None):
    T = x.shape[0]

    def body(x_ref, g_ref, w_ref, qt_ref, kv_ref, kvt_ref, zt_ref):
        _, _, h = _norm(x_ref[...], g_ref[...])
        h = h.astype(BF16)
        scale = 1.0 / math.sqrt(HEAD_DIM)
        qt_ref[...] = (_dot_nt(w_ref[:KV_AT], h) * scale).astype(BF16)
        zt_ref[...] = _dot_nt(w_ref[Z_AT:], h).astype(BF16)
        kvt_ref[...] = _dot_nt(w_ref[KV_AT:Z_AT], h).astype(BF16)
        kv_ref[...] = _dot_nt(h, w_ref[KV_AT:Z_AT]).astype(BF16)

    return _grid_call(
        "attn_in_fwd", body, T // ROW_TILE,
        in_specs=[_rows(D_MODEL), _vmem(), _vmem()],
        out_specs=[_cols(D_MODEL), _rows(2 * KV_WIDTH), _cols(2 * KV_WIDTH), _cols(D_MODEL)],
        out_shape=[
            jax.ShapeDtypeStruct((D_MODEL, T), BF16),
            jax.ShapeDtypeStruct((T, 2 * KV_WIDTH), BF16),
            jax.ShapeDtypeStruct((2 * KV_WIDTH, T), BF16),
            jax.ShapeDtypeStruct((D_MODEL, T), BF16),
        ],
        scratch_shapes=[], args=(x, g, w_t), semantics="parallel", carry=carry)


def _causal_triangle():
    shape = (ATTN_BLOCK, GROUP * ATTN_BLOCK)
    kj = lax.broadcasted_iota(jnp.int32, shape, 0)
    qi = lax.broadcasted_iota(jnp.int32, shape, 1) & (ATTN_BLOCK - 1)
    return kj <= qi


def _group_cols(ref, hk, cols):
    return jnp.concatenate(
        [ref[(hk * GROUP + gi) * HEAD_DIM:(hk * GROUP + gi + 1) * HEAD_DIM, cols] for gi in range(GROUP)], axis=1)


def _group_row(values):
    return jnp.concatenate(values, axis=1)


def _window_rows(ref, halo_ref, b, lanes):
    if b == 0:
        return jnp.concatenate([halo_ref[:, lanes], ref[:ATTN_BLOCK, lanes]], axis=0)
    return ref[(b - 1) * ATTN_BLOCK:(b + 1) * ATTN_BLOCK, lanes]


def _window_cols(ref, halo_ref, b, rows):
    if b == 0:
        return jnp.concatenate([halo_ref[rows, :], ref[rows, :ATTN_BLOCK]], axis=1)
    return ref[rows, (b - 1) * ATTN_BLOCK:(b + 1) * ATTN_BLOCK]


def _select_window(both, tri, b=None, first_penalty=None):
    prev = both[:ATTN_BLOCK]
    if b == 0:
        prev = prev + first_penalty
    return jnp.where(tri, both[ATTN_BLOCK:], prev)


def _split_window(slots, tri):
    zero = jnp.zeros_like(slots)
    return jnp.concatenate([jnp.where(tri, zero, slots), jnp.where(tri, slots, zero)], axis=0)


def attn_core_fwd(q_t, kv, kv_t, z_t, x, w_out, sinks, carry=None):
    T = x.shape[0]
    blocks = ROW_TILE // ATTN_BLOCK

    def body(qt_ref, kv_ref, kvh_ref, kvt_ref, kvth_ref, zt_ref, x_ref, w_ref, sink_ref, xo_ref, ot_ref, lse_ref, oacc):
        tri = _causal_triangle()
        first_penalty = jnp.where(pl.program_id(0) == 0, NEG_BIG, 0.0)
        for hk in range(N_KV_HEADS):
            heads = [hk * GROUP + gi for gi in range(GROUP)]
            sink = _group_row([jnp.full((1, ATTN_BLOCK), sink_ref[h], F32) for h in heads])
            k_lanes = slice(hk * HEAD_DIM, (hk + 1) * HEAD_DIM)
            v_rows = slice(KV_WIDTH + hk * HEAD_DIM, KV_WIDTH + (hk + 1) * HEAD_DIM)
            for b in range(blocks):
                cols = slice(b * ATTN_BLOCK, (b + 1) * ATTN_BLOCK)
                qt = _group_cols(qt_ref, hk, cols)
                s = _select_window(_dot(_window_rows(kv_ref, kvh_ref, b, k_lanes), qt), tri, b, first_penalty)
                m = jnp.maximum(jnp.max(s, axis=0, keepdims=True), sink)
                p = jnp.exp(s - m)
                denom = jnp.sum(p, axis=0, keepdims=True) + jnp.exp(sink - m)
                o = _dot(_window_cols(kvt_ref, kvth_ref, b, v_rows), _split_window(p.astype(BF16), tri)) * (1.0 / denom)
                lse = m + jnp.log(denom)
                for gi, h in enumerate(heads):
                    part = slice(gi * ATTN_BLOCK, (gi + 1) * ATTN_BLOCK)
                    oacc[h * HEAD_DIM:(h + 1) * HEAD_DIM, cols] = o[:, part]
                    lse_ref[h:h + 1, cols] = lse[:, part]
        of = oacc[...]
        silu, _ = _silu_parts(zt_ref[...].astype(F32))
        y = _dot_tn((of * silu).astype(BF16), w_ref[...])
        xo_ref[...] = x_ref[...] + y
        ot_ref[...] = of.astype(BF16)

    return _grid_call(
        "attn_core_fwd", body, T // ROW_TILE,
        in_specs=[_cols(D_MODEL), _rows(2 * KV_WIDTH), _halo_prev_rows(2 * KV_WIDTH), _cols(2 * KV_WIDTH),
                  _halo_prev_cols(2 * KV_WIDTH), _cols(D_MODEL), _rows(D_MODEL), _vmem(),
                  pl.BlockSpec(memory_space=pltpu.SMEM)],
        out_specs=[_rows(D_MODEL), _cols(D_MODEL), _cols(N_HEADS)],
        out_shape=[
            jax.ShapeDtypeStruct((T, D_MODEL), F32),
            jax.ShapeDtypeStruct((D_MODEL, T), BF16),
            jax.ShapeDtypeStruct((N_HEADS, T), F32),
        ],
        scratch_shapes=[pltpu.VMEM((D_MODEL, ROW_TILE), F32)],
        args=(q_t, kv, kv, kv_t, kv_t, z_t, x, w_out, sinks), semantics="parallel", carry=carry)


def _inv_count(first_row, rows, window):
    t = first_row + lax.broadcasted_iota(jnp.int32, (rows, 1), 0)
    return 1.0 / jnp.minimum(t + 1, window).astype(F32)


MIX_ROWS = POOL_GC // N_CHIPS


def _mix_groups(wmix_ref):
    return [jnp.concatenate([wmix_ref[k, gi * MIX_ROWS:(gi + 1) * MIX_ROWS, :] for k in range(N_CHIPS)], axis=0)
            for gi in range(len(POOL_WINDOWS))]


def pool_fwd(x, g, w_in, w_mix, scale, w_out, carry=None):
    T = x.shape[0]

    def body(x_ref, g_ref, win_ref, wmix_ref, scale_ref, wout_ref, xo_ref, p_ref, z_ref, carry):
        i = pl.program_id(0)
        wmix = _mix_groups(wmix_ref)

        @pl.when(i == 0)
        def _():
            carry[...] = jnp.zeros_like(carry)

        _, _, h = _norm(x_ref[...], g_ref[...])
        h = h.astype(BF16)
        u = jnp.concatenate([_dot(h, win_ref[0]), _dot(h, win_ref[1])], axis=1)
        z = jnp.concatenate([_dot(h, win_ref[2]), _dot(h, win_ref[3])], axis=1)
        ext = jnp.concatenate([carry[...], u], axis=0)
        carry[...] = u[ROW_TILE - POOL_HALO:]
        mixed = []
        for gi, window in enumerate(POOL_WINDOWS):
            cols = slice(gi * POOL_GC, (gi + 1) * POOL_GC)
            s = ext[:, cols]
            shift = 1
            while shift < window:
                s = s + pltpu.roll(s, shift, 0)
                shift *= 2
            p = s[POOL_HALO:] * _inv_count(i * ROW_TILE, ROW_TILE, window) - u[:, cols]
            p = p.astype(BF16)
            p_ref[:, cols] = p
            mixed.append(_dot(p, wmix[gi]))
        m = jnp.concatenate(mixed, axis=1) * scale_ref[...]
        silu, _ = _silu_parts(z)
        y = _dot((m * silu).astype(BF16), wout_ref[...])
        xo_ref[...] = x_ref[...] + y
        z_ref[...] = z.astype(BF16)

    return _grid_call(
        "pool_fwd", body, T // ROW_TILE,
        in_specs=[_rows(D_MODEL), _vmem(), _vmem(), _vmem(), _vmem(), _vmem()],
        out_specs=[_rows(D_MODEL), _rows(D_MODEL), _rows(D_MODEL)],
        out_shape=[
            jax.ShapeDtypeStruct((T, D_MODEL), F32),
            jax.ShapeDtypeStruct((T, D_MODEL), BF16),
            jax.ShapeDtypeStruct((T, D_MODEL), BF16),
        ],
        scratch_shapes=[pltpu.VMEM((POOL_HALO, D_MODEL), F32)],
        args=(x, g, w_in, w_mix, scale, w_out), semantics="arbitrary", carry=carry)


def loss_head(x, g, target):
    T = x.shape[0]

    def body(x_ref, g_ref, t_ref, dx_ref, sq_ref, dg_ref):
        @pl.when(pl.program_id(0) == 0)
        def _():
            sq_ref[...] = jnp.zeros_like(sq_ref)
            dg_ref[...] = jnp.zeros_like(dg_ref)

        gv = g_ref[...]
        xhat, rstd, out = _norm(x_ref[...], gv)
        err = out - t_ref[...]
        sq_ref[...] += jnp.sum(err * err, axis=0, keepdims=True)
        dx, dg = _norm_bwd(err * (1.0 / D_MODEL), xhat, rstd, gv)
        dx_ref[...] = dx
        dg_ref[...] += dg

    return pl.pallas_call(
        body,
        name="loss_head",
        grid=(T // ROW_TILE,),
        in_specs=[_rows(D_MODEL), _vmem(), _rows(D_MODEL)],
        out_specs=[_rows(D_MODEL), _vmem(), _vmem()],
        out_shape=[
            jax.ShapeDtypeStruct((T, D_MODEL), F32),
            jax.ShapeDtypeStruct((1, D_MODEL), F32),
            jax.ShapeDtypeStruct((1, D_MODEL), F32),
        ],
        compiler_params=_params("arbitrary"),
    )(x, g, target)


def pool_bwd_mix(dy, p, z, w_mix, scale, w_out):
    T = dy.shape[0]
    tiles = T // ROW_TILE

    def body(dy_ref, p_ref, z_ref, wmix_ref, scale_ref, wout_ref, dp_ref, dz_ref, dwout_out, dwmix_out, dscale_ref,
             dwout_ref, dwmix_ref):
        @pl.when(pl.program_id(0) == 0)
        def _():
            dwout_ref[...] = jnp.zeros_like(dwout_ref)
            dwmix_ref[...] = jnp.zeros_like(dwmix_ref)
            dscale_ref[...] = jnp.zeros_like(dscale_ref)

        wmix = _mix_groups(wmix_ref)
        dyb = dy_ref[...].astype(BF16)
        da = _dot_nt(dyb, wout_ref[...])
        m_pre = jnp.concatenate(
            [_dot(p_ref[:, gi * POOL_GC:(gi + 1) * POOL_GC], wmix[gi]) for gi in range(len(POOL_WINDOWS))], axis=1)
        sc = scale_ref[...]
        m = m_pre * sc
        zf = z_ref[...].astype(F32)
        silu, dsilu = _silu_parts(zf)
        dwout_ref[...] += _dot_tn((m * silu).astype(BF16), dyb)
        dm = da * silu
        dz_ref[...] = (da * m * dsilu).astype(BF16)
        dscale_ref[...] += jnp.sum(dm * m_pre, axis=0, keepdims=True)
        dmp = (dm * sc).astype(BF16)
        for gi in range(len(POOL_WINDOWS)):
            cols = slice(gi * POOL_GC, (gi + 1) * POOL_GC)
            dw = _dot_tn(p_ref[:, cols], dmp[:, cols])
            for k in range(N_CHIPS):
                dwmix_ref[k, gi * MIX_ROWS:(gi + 1) * MIX_ROWS, :] += dw[k * MIX_ROWS:(k + 1) * MIX_ROWS]
            dp_ref[:, cols] = _dot_nt(dmp[:, cols], wmix[gi]).astype(BF16)

        @pl.when(pl.program_id(0) == tiles - 1)
        def _():
            dwout_out[...] = dwout_ref[...].astype(BF16)
            dwmix_out[...] = dwmix_ref[...].astype(BF16)

    return pl.pallas_call(
        body,
        name="pool_bwd_mix",
        grid=(tiles,),
        in_specs=[_rows(D_MODEL), _rows(D_MODEL), _rows(D_MODEL), _vmem(), _vmem(), _vmem()],
        out_specs=[_rows(D_MODEL), _rows(D_MODEL), _vmem(), _vmem(), _vmem()],
        out_shape=[
            jax.ShapeDtypeStruct((T, D_MODEL), BF16),
            jax.ShapeDtypeStruct((T, D_MODEL), BF16),
            jax.ShapeDtypeStruct((D_MODEL, D_MODEL), BF16),
            jax.ShapeDtypeStruct((N_CHIPS, POOL_GC, POOL_GC), BF16),
            jax.ShapeDtypeStruct((1, D_MODEL), F32),
        ],
        scratch_shapes=[pltpu.VMEM((D_MODEL, D_MODEL), F32), pltpu.VMEM((N_CHIPS, POOL_GC, POOL_GC), F32)],
        compiler_params=_params("arbitrary"),
    )(dy, p, z, w_mix, scale, w_out)


def pool_bwd_in(dp, dz, x, dy, g, w_in, carry=None):
    T = x.shape[0]
    tiles = T // ROW_TILE
    halo_blocks = ROW_TILE // POOL_HALO
    last_halo = T // POOL_HALO - 1

    def body(dp_ref, dph_ref, dz_ref, x_ref, dy_ref, g_ref, win_ref, dx_ref, dwin_out, dg_ref, dwin_ref):
        i = pl.program_id(0)

        @pl.when(i == 0)
        def _():
            dwin_ref[...] = jnp.zeros_like(dwin_ref)
            dg_ref[...] = jnp.zeros_like(dg_ref)

        rows = ROW_TILE + POOL_HALO
        ext = jnp.concatenate([dp_ref[...], dph_ref[...]], axis=0).astype(F32)
        t = i * ROW_TILE + lax.broadcasted_iota(jnp.int32, (rows, 1), 0)
        inside = (t < T).astype(F32)
        du = []
        for gi, window in enumerate(POOL_WINDOWS):
            cols = slice(gi * POOL_GC, (gi + 1) * POOL_GC)
            s = ext[:, cols] * (_inv_count(i * ROW_TILE, rows, window) * inside)
            shift = 1
            while shift < window:
                s = s + pltpu.roll(s, rows - shift, 0)
                shift *= 2
            du.append(s[:ROW_TILE] - ext[:ROW_TILE, cols])
        du = jnp.concatenate(du, axis=1).astype(BF16)
        chunks = [du[:, :POOL_CHUNK], du[:, POOL_CHUNK:], dz_ref[:, :POOL_CHUNK], dz_ref[:, POOL_CHUNK:]]
        gv = g_ref[...]
        xhat, rstd, h = _norm(x_ref[...], gv)
        h = h.astype(BF16)
        dh = jnp.zeros((ROW_TILE, D_MODEL), F32)
        for c in range(N_CHIPS):
            dwin_ref[c] += _dot_tn(h, chunks[c])
            dh = dh + _dot_nt(chunks[c], win_ref[c])
        dx, dg = _norm_bwd(dh, xhat, rstd, gv)
        dx_ref[...] = dx + dy_ref[...]
        dg_ref[...] += dg

        @pl.when(i == tiles - 1)
        def _():
            dwin_out[...] = dwin_ref[...].astype(BF16)

    return _grid_call(
        "pool_bwd_in", body, tiles,
        in_specs=[_rows(D_MODEL),
                  pl.BlockSpec((POOL_HALO, D_MODEL), lambda i: (jnp.minimum((i + 1) * halo_blocks, last_halo), 0)),
                  _rows(D_MODEL), _rows(D_MODEL), _rows(D_MODEL), _vmem(), _vmem()],
        out_specs=[_rows(D_MODEL), _vmem(), _vmem()],
        out_shape=[
            jax.ShapeDtypeStruct((T, D_MODEL), F32),
            jax.ShapeDtypeStruct((N_CHIPS, D_MODEL, POOL_CHUNK), BF16),
            jax.ShapeDtypeStruct((1, D_MODEL), F32),
        ],
        scratch_shapes=[pltpu.VMEM((N_CHIPS, D_MODEL, POOL_CHUNK), F32)],
        args=(dp, dp, dz, x, dy, g, w_in), semantics="arbitrary", carry=carry)


def attn_bwd_core(dy, q_t, kv, kv_t, z_t, o_t, lse, w_out, sinks, carry=None):
    T = dy.shape[0]
    tiles = T // ROW_TILE
    blocks = ROW_TILE // ATTN_BLOCK

    def body(dy_ref, qt_ref, kv_ref, kvh_ref, kvt_ref, kvth_ref, zt_ref, ot_ref, lse_ref, w_ref, sink_ref,
             dqt_ref, dkv_ref, dkvh_ref, dzt_ref, dwout_out, dsink_ref, do_s, dkv_s, dwout_ref):
        @pl.when(pl.program_id(0) == 0)
        def _():
            dwout_ref[...] = jnp.zeros_like(dwout_ref)
            dsink_ref[...] = jnp.zeros_like(dsink_ref)

        dyb = dy_ref[...].astype(BF16)
        da = _dot_nt(w_ref[...], dyb)
        of = ot_ref[...].astype(F32)
        silu, dsilu = _silu_parts(zt_ref[...].astype(F32))
        dwout_ref[...] += _dot((of * silu).astype(BF16), dyb)
        do = da * silu
        dzt_ref[...] = (da * of * dsilu).astype(BF16)
        do_s[...] = do.astype(BF16)
        dof = do * of
        dkv_s[...] = jnp.zeros_like(dkv_s)
        tri = _causal_triangle()
        first_penalty = jnp.where(pl.program_id(0) == 0, NEG_BIG, 0.0)
        for hk in range(N_KV_HEADS):
            heads = [hk * GROUP + gi for gi in range(GROUP)]
            sink = _group_row([jnp.full((1, ATTN_BLOCK), sink_ref[h], F32) for h in heads])
            deltas = [jnp.sum(dof[h * HEAD_DIM:(h + 1) * HEAD_DIM], axis=0, keepdims=True) for h in heads]
            k_lanes = slice(hk * HEAD_DIM, (hk + 1) * HEAD_DIM)
            v_lanes = slice(KV_WIDTH + hk * HEAD_DIM, KV_WIDTH + (hk + 1) * HEAD_DIM)
            for b in range(blocks):
                cols = slice(b * ATTN_BLOCK, (b + 1) * ATTN_BLOCK)
                window = slice(b * ATTN_BLOCK, (b + 2) * ATTN_BLOCK)
                qt = _group_cols(qt_ref, hk, cols)
                dot = _group_cols(do_s, hk, cols)
                lse_row = _group_row([lse_ref[h:h + 1, cols] for h in heads])
                delta = _group_row([d[:, cols] for d in deltas])
                s = _select_window(_dot(_window_rows(kv_ref, kvh_ref, b, k_lanes), qt), tri, b, first_penalty)
                p = jnp.exp(s - lse_row)
                dp = _select_window(_dot(_window_rows(kv_ref, kvh_ref, b, v_lanes), dot), tri)
                ds = _split_window((p * (dp - delta)).astype(BF16), tri)
                dq = _dot(_window_cols(kvt_ref, kvth_ref, b, k_lanes), ds) * (1.0 / math.sqrt(HEAD_DIM))
                dkv_s[window, k_lanes] += _dot_nt(ds, qt)
                dkv_s[window, v_lanes] += _dot_nt(_split_window(p.astype(BF16), tri), dot)
                dsink = -jnp.exp(sink - lse_row) * delta
                for gi, h in enumerate(heads):
                    part = slice(gi * ATTN_BLOCK, (gi + 1) * ATTN_BLOCK)
                    dqt_ref[h * HEAD_DIM:(h + 1) * HEAD_DIM, cols] = dq[:, part].astype(BF16)
                    dsink_ref[0:1, h:h + 1] += jnp.sum(dsink[:, part], axis=1, keepdims=True)
        dkv_ref[...] = dkv_s[ATTN_BLOCK:]
        dkvh_ref[0] = dkv_s[:ATTN_BLOCK]

        @pl.when(pl.program_id(0) == tiles - 1)
        def _():
            dwout_out[...] = dwout_ref[...].astype(BF16)

    return _grid_call(
        "attn_bwd_core", body, tiles,
        in_specs=[_rows(D_MODEL), _cols(D_MODEL), _rows(2 * KV_WIDTH), _halo_prev_rows(2 * KV_WIDTH),
                  _cols(2 * KV_WIDTH), _halo_prev_cols(2 * KV_WIDTH), _cols(D_MODEL), _cols(D_MODEL), _cols(N_HEADS),
                  _vmem(), pl.BlockSpec(memory_space=pltpu.SMEM)],
        out_specs=[_cols(D_MODEL), _rows(2 * KV_WIDTH), pl.BlockSpec((1, ATTN_BLOCK, 2 * KV_WIDTH), lambda i: (i, 0, 0)),
                   _cols(D_MODEL), _vmem(), _vmem()],
        out_shape=[
            jax.ShapeDtypeStruct((D_MODEL, T), BF16),
            jax.ShapeDtypeStruct((T, 2 * KV_WIDTH), F32),
            jax.ShapeDtypeStruct((tiles, ATTN_BLOCK, 2 * KV_WIDTH), F32),
            jax.ShapeDtypeStruct((D_MODEL, T), BF16),
            jax.ShapeDtypeStruct((D_MODEL, D_MODEL), BF16),
            jax.ShapeDtypeStruct((8, 128), F32),
        ],
        scratch_shapes=[
            pltpu.VMEM((D_MODEL, ROW_TILE), BF16),
            pltpu.VMEM((ROW_TILE + ATTN_BLOCK, 2 * KV_WIDTH), F32),
            pltpu.VMEM((D_MODEL, D_MODEL), F32),
        ],
        args=(dy, q_t, kv, kv, kv_t, kv_t, z_t, o_t, lse, w_out, sinks), semantics="arbitrary", carry=carry)


def attn_bwd_in(dq_t, dkv, dkv_halo, dz_t, x, dy, g, w_t, weights=True, inputs=True, carry=None):
    T = x.shape[0]
    tiles = T // ROW_TILE
    kv_at, z_at = KV_AT, Z_AT

    def body(dqt_ref, dkv_ref, dkvh_ref, dzt_ref, x_ref, dy_ref, g_ref, w_ref, *results):
        i = pl.program_id(0)
        results = list(results)
        dx_ref, dg_ref = (results.pop(0), results.pop(0)) if inputs else (None, None)
        dwt_out, dwt_ref = results if weights else (None, None)

        @pl.when(i == 0)
        def _():
            if weights:
                dwt_ref[...] = jnp.zeros_like(dwt_ref)
            if inputs:
                dg_ref[...] = jnp.zeros_like(dg_ref)

        more = (i < tiles - 1).astype(F32)
        tail = jnp.concatenate([jnp.zeros((ROW_TILE - ATTN_BLOCK, 2 * KV_WIDTH), F32), dkvh_ref[0] * more], axis=0)
        dkvb = (dkv_ref[...] + tail).astype(BF16)
        gv = g_ref[...]
        xhat, rstd, h = _norm(x_ref[...], gv)
        h = h.astype(BF16)
        dqt = dqt_ref[...]
        dzt = dzt_ref[...]
        if weights:
            dwt_ref[:kv_at] += _dot(dqt, h)
            dwt_ref[kv_at:z_at] += _dot_tn(dkvb, h)
            dwt_ref[z_at:] += _dot(dzt, h)
        if inputs:
            dh = _dot_tn(dqt, w_ref[:kv_at]) + _dot_tn(dzt, w_ref[z_at:]) + _dot(dkvb, w_ref[kv_at:z_at])
            dx, dg = _norm_bwd(dh, xhat, rstd, gv)
            dx_ref[...] = dx + dy_ref[...]
            dg_ref[...] += dg

        if weights:
            @pl.when(i == tiles - 1)
            def _():
                dwt_out[...] = dwt_ref[...].astype(BF16)

    halo_next = pl.BlockSpec((1, ATTN_BLOCK, 2 * KV_WIDTH), lambda i: (jnp.minimum(i + 1, tiles - 1), 0, 0))
    out_specs, out_shape, scratch = [], [], []
    if inputs:
        out_specs += [_rows(D_MODEL), _vmem()]
        out_shape += [jax.ShapeDtypeStruct((T, D_MODEL), F32), jax.ShapeDtypeStruct((1, D_MODEL), F32)]
    if weights:
        out_specs += [_vmem()]
        out_shape += [jax.ShapeDtypeStruct((N_CHIPS * ATTN_CHUNK, D_MODEL), BF16)]
        scratch += [pltpu.VMEM((N_CHIPS * ATTN_CHUNK, D_MODEL), F32)]
    name = "attn_bwd_in" + ("" if weights and inputs else "_weights" if weights else "_inputs")
    return _grid_call(
        name, body, tiles,
        in_specs=[_cols(D_MODEL), _rows(2 * KV_WIDTH), halo_next, _cols(D_MODEL), _rows(D_MODEL), _rows(D_MODEL),
                  _vmem(), _vmem()],
        out_specs=out_specs, out_shape=out_shape, scratch_shapes=scratch,
        args=(dq_t, dkv, dkv_halo, dz_t, x, dy, g, w_t), semantics="arbitrary", carry=carry)


BIG = ("attn_w_in", "attn_w_out", "pool_w_in", "pool_w_mix", "pool_w_out")


def _half(ref, h, axis=0):
    rows = ref.shape[axis] // 2
    index = (slice(None),) * axis + (pl.ds(h * rows, rows),)
    return ref.at[index]


def _mesh_position():
    return lax.axis_index("x"), lax.axis_index("y"), lax.axis_index("c")


def _other_chips(x, y):
    return [(1 - x, y), (x, 1 - y), (1 - x, 1 - y)]


def _any():
    return pl.BlockSpec(memory_space=pl.ANY)


def _remote(src, dst, sems, index, to):
    send_sems, recv_sems = sems
    return pltpu.make_async_remote_copy(src_ref=src, dst_ref=dst, send_sem=send_sems.at[index], recv_sem=recv_sems.at[index],
                                        device_id=to, device_id_type=MESH_ID)


class Exchange:
    def __init__(self, inputs, out_shapes, n_remote, start, finish, mid=None):
        self.inputs, self.out_shapes, self.n_remote = list(inputs), list(out_shapes), n_remote
        self.start, self.finish, self.mid = start, finish, mid


def _run_exchange(name, ex):
    n_in, n_out = len(ex.inputs), len(ex.out_shapes)

    def kernel_body(*refs):
        ins, outs, sems = refs[:n_in], refs[n_in:n_in + n_out], refs[n_in + n_out:]
        ex.start(ins, outs, sems)
        if ex.mid is not None:
            ex.mid(ins, outs, sems)
        ex.finish(ins, outs, sems)

    return pl.pallas_call(
        kernel_body,
        name=name,
        in_specs=[_any()] * n_in,
        out_specs=[_any()] * n_out,
        out_shape=ex.out_shapes,
        scratch_shapes=[pltpu.SemaphoreType.DMA((ex.n_remote,))] * 2,
    )(*ex.inputs)


def gather_exchange(blocks, whole):
    def copies(ins, outs, sems):
        x, y, c = _mesh_position()
        me = 2 * x + y
        sibling = (x, y, 1 - c)
        table = []
        for t, (w_ref, out_ref) in enumerate(zip(ins, outs)):
            for j, (px, py) in enumerate(_other_chips(x, y)):
                peer = 2 * px + py
                if t in whole:
                    table.append((_remote(w_ref, out_ref.at[me], sems, 6 * t + j, (px, py, c)),
                                  _remote(w_ref, out_ref.at[peer], sems, 6 * t + j, (px, py, c)), None, None))
                    continue
                landed = _half(out_ref.at[peer], c)
                from_sibling = _half(out_ref.at[peer], 1 - c)
                table.append((_remote(_half(w_ref, c), _half(out_ref.at[me], c), sems, 6 * t + j, (px, py, c)),
                              _remote(landed, landed, sems, 6 * t + j, (px, py, c)),
                              _remote(landed, landed, sems, 6 * t + 3 + j, sibling),
                              _remote(from_sibling, from_sibling, sems, 6 * t + 3 + j, sibling)))
        return table

    def start(ins, outs, sems):
        for send, _, _, _ in copies(ins, outs, sems):
            send.start()

    def mid(ins, outs, sems):
        for _, arrival, forward, _ in copies(ins, outs, sems):
            if forward is not None:
                arrival.wait_recv()
                forward.start()

    def finish(ins, outs, sems):
        for send, arrival, forward, from_sibling in copies(ins, outs, sems):
            send.wait_send()
            if forward is None:
                arrival.wait_recv()
            else:
                forward.wait_send()
                from_sibling.wait_recv()

    shapes = [jax.ShapeDtypeStruct((N_CHIPS,) + b.shape, b.dtype) for b in blocks]
    return Exchange(blocks, shapes, 6 * len(blocks), start, finish, mid)


N_SENDERS = 7


def reduce_exchange(grads):
    def copies(ins, outs, sems):
        send_sems, recv_sems = sems
        x, y, c = _mesh_position()
        me = 2 * x + y
        sends, arrivals = [], []
        for t, (g_ref, out_ref) in enumerate(zip(ins, outs)):
            base = N_SENDERS * t

            def copy(src, slot, send_index, to):
                return pltpu.make_async_remote_copy(
                    src_ref=src, dst_ref=out_ref.at[slot], send_sem=send_sems.at[base + send_index],
                    recv_sem=recv_sems.at[base + slot], device_id=to, device_id_type=MESH_ID)

            mine = _half(g_ref.at[me], c)
            sends.append(copy(_half(g_ref.at[me], 1 - c), 0, 0, (x, y, 1 - c)))
            arrivals.append(copy(mine, 0, 0, (x, y, 1 - c)))
            for j, (px, py) in enumerate(_other_chips(x, y)):
                for h in range(2):
                    sends.append(copy(_half(g_ref.at[2 * px + py], h), 1 + 2 * j + c, 1 + 2 * j + h, (px, py, h)))
                    arrivals.append(copy(mine, 1 + 2 * j + h, 1 + 2 * j + h, (px, py, h)))
        return sends, arrivals

    def start(ins, outs, sems):
        for cp in copies(ins, outs, sems)[0]:
            cp.start()

    def finish(ins, outs, sems):
        sends, arrivals = copies(ins, outs, sems)
        for cp in sends:
            cp.wait_send()
        for cp in arrivals:
            cp.wait_recv()

    shapes = [jax.ShapeDtypeStruct((N_SENDERS, g.shape[1] // 2, g.shape[2]), g.dtype) for g in grads]
    return Exchange(grads, shapes, N_SENDERS * len(grads), start, finish)


def sum_landed(name, grads, landed, chip, core):
    n = len(grads)
    steps = 4

    def body(where_ref, *refs):
        for g_ref, l_ref, o_ref in zip(refs[:n], refs[n:2 * n], refs[2 * n:]):
            total = g_ref[0, 0].astype(F32)
            for s in range(N_SENDERS):
                total = total + l_ref[s].astype(F32)
            o_ref[...] = total

    halves = [g.reshape(N_CHIPS, 2, g.shape[1] // 2, g.shape[2]) for g in grads]

    def rows(h):
        return h.shape[2] // steps

    return pl.pallas_call(
        body,
        name=name,
        grid_spec=pltpu.PrefetchScalarGridSpec(
            num_scalar_prefetch=1, grid=(steps,),
            in_specs=[pl.BlockSpec((1, 1, rows(h), h.shape[3]), lambda r, where_ref: (where_ref[0], where_ref[1], r, 0))
                      for h in halves]
            + [pl.BlockSpec((N_SENDERS, rows(h), h.shape[3]), lambda r, where_ref: (0, r, 0)) for h in halves],
            out_specs=[pl.BlockSpec((rows(h), h.shape[3]), lambda r, where_ref: (r, 0)) for h in halves]),
        out_shape=[jax.ShapeDtypeStruct(h.shape[2:], F32) for h in halves],
        compiler_params=_params("parallel"),
    )(jnp.stack([chip, core]).astype(jnp.int32), *halves, *landed)


def swap_exchange(halves):
    def copies(ins, outs, sems):
        x, y, c = _mesh_position()
        return [_remote(h_ref, out_ref, sems, i, (x, y, 1 - c)) for i, (h_ref, out_ref) in enumerate(zip(ins, outs))]

    def start(ins, outs, sems):
        for cp in copies(ins, outs, sems):
            cp.start()

    def finish(ins, outs, sems):
        for cp in copies(ins, outs, sems):
            cp.wait()

    return Exchange(halves, [jax.ShapeDtypeStruct(h.shape, h.dtype) for h in halves], len(halves), start, finish)


def all_reduce_small(v):
    n_dev = 8

    def body(v_ref, out_ref, slots, send_sems, recv_sems):
        x, y, c = _mesh_position()
        me = 4 * x + 2 * y + c
        slots[me] = v_ref[...]
        peers = []
        for r in range(1, n_dev):
            fx, fy, fc = (r >> 2) & 1, (r >> 1) & 1, r & 1
            peers.append((1 - x if fx else x, 1 - y if fy else y, 1 - c if fc else c))
        sends = [
            pltpu.make_async_remote_copy(
                src_ref=v_ref, dst_ref=slots.at[me], send_sem=send_sems.at[r], recv_sem=recv_sems.at[r],
                device_id=peer, device_id_type=MESH_ID)
            for r, peer in enumerate(peers)
        ]
        for cp in sends:
            cp.start()
        for r, (px, py, pc) in enumerate(peers):
            pltpu.make_async_remote_copy(
                src_ref=v_ref, dst_ref=slots.at[4 * px + 2 * py + pc], send_sem=send_sems.at[r], recv_sem=recv_sems.at[r],
                device_id=(px, py, pc), device_id_type=MESH_ID).wait_recv()
        for cp in sends:
            cp.wait_send()
        total = slots[0]
        for d in range(1, n_dev):
            total = total + slots[d]
        out_ref[...] = total

    return pl.pallas_call(
        body,
        name="all_reduce_small",
        in_specs=[_vmem()],
        out_specs=_vmem(),
        out_shape=jax.ShapeDtypeStruct(v.shape, v.dtype),
        scratch_shapes=[pltpu.VMEM((n_dev,) + v.shape, v.dtype), pltpu.SemaphoreType.DMA((n_dev - 1,)),
                        pltpu.SemaphoreType.DMA((n_dev - 1,))],
    )(v)


def adamw(name, ws, gs, ms, vs, steps):
    n = len(ws)

    def body(*refs):
        ins, outs = refs[:4 * n], refs[4 * n:]
        for t in range(n):
            w_ref, g_ref, m_ref, v_ref = (ins[q * n + t] for q in range(4))
            d_ref, mo_ref, vo_ref = (outs[q * n + t] for q in range(3))
            gv = g_ref[...]
            m2 = ADAM_B1 * m_ref[...] + (1.0 - ADAM_B1) * gv
            v2 = ADAM_B2 * v_ref[...] + (1.0 - ADAM_B2) * (gv * gv)
            m_hat = m2 / (1.0 - ADAM_B1 ** ADAM_STEP)
            v_hat = v2 / (1.0 - ADAM_B2 ** ADAM_STEP)
            d_ref[...] = -ADAM_LR * (m_hat / (jnp.sqrt(v_hat) + ADAM_EPS) + ADAM_WD * w_ref[...])
            mo_ref[...] = m2
            vo_ref[...] = v2

    specs = [pl.BlockSpec((1, w.shape[1] // steps, w.shape[2]), lambda j, r: (j, r, 0)) for w in ws]
    shapes = [jax.ShapeDtypeStruct(w.shape, F32) for w in ws]
    out = pl.pallas_call(
        body, name=name, grid=(ws[0].shape[0], steps), in_specs=specs * 4, out_specs=specs * 3, out_shape=shapes * 3,
        compiler_params=pltpu.CompilerParams(dimension_semantics=("parallel", "parallel"), vmem_limit_bytes=VMEM_LIMIT_BYTES),
    )(*ws, *gs, *ms, *vs)
    return out[:n], out[n:2 * n], out[2 * n:]


def kernel(x, norm_g, attn_w_in, attn_sinks, attn_w_out, pool_w_in, pool_w_mix, pool_scale, pool_w_out, final_g, loss_target, m_norm_g, m_attn_w_in, m_attn_sinks, m_attn_w_out, m_pool_w_in, m_pool_w_mix, m_pool_scale, m_pool_w_out, m_final_g, v_norm_g, v_attn_w_in, v_attn_sinks, v_attn_w_out, v_pool_w_in, v_pool_w_mix, v_pool_scale, v_pool_w_out, v_final_g):
    cx, cy, cc = _mesh_position()
    chip = 2 * cx + cy

    def blocks_2d(a_in, a_out, p_in, p_mix, p_out):
        return [a_in, a_out, p_in, p_mix.reshape(2, POOL_GC, POOL_GC), p_out]

    w_blocks = blocks_2d(attn_w_in, attn_w_out, pool_w_in, pool_w_mix, pool_w_out)

    def gather(layer):
        j = layer // 2
        if layer % 2 == 0:
            own = [w_blocks[0][j].T.astype(BF16), w_blocks[1][j].astype(BF16)]
            return own, gather_exchange(own, whole=())
        own = [w_blocks[t][j].astype(BF16) for t in (2, 3, 4)] + [pool_scale[j][None, :]]
        return own, gather_exchange(own, whole=(3,))

    def placed(own, others):
        return [lax.dynamic_update_slice(o, mine[None], (chip, 0, 0)) for o, mine in zip(others, own)]

    own = gather(0)[0]
    weights = placed(own[:1], _run_exchange("gather_layer_0", gather_exchange(own[:1], whole=())))
    xs = x[0]
    saved = []
    for layer in range(4):
        g_row = norm_g[layer][None, :]
        if layer % 2 == 0:
            w_t = weights[0].reshape(N_CHIPS * ATTN_CHUNK, D_MODEL)
            if len(weights) == 1:
                (q_t, kv, kv_t, z_t), landed = attn_in_fwd(xs, g_row, w_t, carry=gather_exchange(own[1:], whole=()))
                weights += placed(own[1:], landed)
            else:
                q_t, kv, kv_t, z_t = attn_in_fwd(xs, g_row, w_t)
            w_out = weights[1].reshape(D_MODEL, D_MODEL)
            own, exchange = gather(layer + 1)
            (x_new, o_t, lse), landed = attn_core_fwd(q_t, kv, kv_t, z_t, xs, w_out, attn_sinks[layer // 2], carry=exchange)
            saved.append((xs, g_row, q_t, kv, kv_t, z_t, o_t, lse, w_t, w_out))
        else:
            p_in, p_mix, p_out, scale = weights
            scale, w_out = scale.reshape(1, D_MODEL), p_out.reshape(D_MODEL, D_MODEL)
            own, exchange = gather(layer + 1) if layer < 3 else (None, None)
            result = pool_fwd(xs, g_row, p_in, p_mix, scale, w_out, carry=exchange)
            (x_new, p, z), landed = result if exchange is not None else (result, None)
            saved.append((xs, g_row, p, z, p_in, p_mix, scale, w_out))
        if exchange is not None:
            weights = placed(own, landed)
        xs = x_new

    dx, sq, d_final_g = loss_head(xs, final_g[None, :], loss_target[0])
    d_norm, d_scale, d_sinks, sent, pending = [None] * 4, [None] * 2, [None] * 2, {}, None
    for layer in reversed(range(4)):
        exchange = reduce_exchange(pending[1]) if pending is not None else None
        if layer % 2 == 0:
            x_in, g_row, q_t, kv, kv_t, z_t, o_t, lse, w_t, w_out = saved[layer]
            (dq_t, dkv, dkv_halo, dz_t, dwout, dsink), landed = attn_bwd_core(
                dx, q_t, kv, kv_t, z_t, o_t, lse, w_out, attn_sinks[layer // 2], carry=exchange)
            dwout = dwout.reshape(N_CHIPS, -1, D_MODEL)
            d_sinks[layer // 2] = dsink[0, :N_HEADS]
            grads_in = (dq_t, dkv, dkv_halo, dz_t, x_in, dx, g_row, w_t)
            if layer > 0:
                dx, d_norm[layer], dwin_t = attn_bwd_in(*grads_in)
                mine = [dwin_t.reshape(N_CHIPS, ATTN_CHUNK, D_MODEL), dwout]
            else:
                (dwin_t,), landed_out = attn_bwd_in(*grads_in, inputs=False, carry=reduce_exchange([dwout]))
                dwin = dwin_t.reshape(N_CHIPS, ATTN_CHUNK, D_MODEL)
                (dx, d_norm[layer]), landed_in = attn_bwd_in(*grads_in, weights=False, carry=reduce_exchange([dwin]))
                sent[0] = ([dwin, dwout], [landed_in[0], landed_out[0]])
                mine = None
        else:
            x_in, g_row, p, z, p_in, p_mix, scale, w_out = saved[layer]
            dp, dz, dwout, dwmix, d_scale[layer // 2] = pool_bwd_mix(dx, p, z, p_mix, scale, w_out)
            result = pool_bwd_in(dp, dz, x_in, dx, g_row, p_in, carry=exchange)
            (dx, dwin, d_norm[layer]), landed = result if exchange is not None else (result, None)
            mine = [dwin, dwmix, dwout.reshape(N_CHIPS, -1, D_MODEL)]
        if pending is not None:
            sent[pending[0]] = (pending[1], landed)
        pending = (layer, mine)
    grad_x = dx

    last = jnp.concatenate([d_sinks[0], d_sinks[1], jnp.sum(sq).reshape(1),
                            jnp.zeros((D_MODEL - 2 * N_HEADS - 1,), F32)])[None, :]
    small = jnp.concatenate(d_norm + [d_final_g] + d_scale + [last], axis=0)
    small = all_reduce_small(small)
    loss = 0.5 * small[7, 2 * N_HEADS] / D_MODEL
    g_norm = small[0:4]
    g_final = small[4]
    g_scale = lax.dynamic_slice(small[5:7], (0, chip * 256), (2, 256))
    g_sinks = small[7, :2 * N_HEADS].reshape(2, N_HEADS)

    shares, landed = [], []
    for layer in (0, 1, 2, 3):
        shares += sent[layer][0]
        landed += sent[layer][1]
    reduced = sum_landed("sum_landed", shares, landed, chip, cc)
    from_sibling = _run_exchange("swap_reduced", swap_exchange(reduced))
    whole = [jnp.where(cc == 0, jnp.concatenate([mine, theirs]), jnp.concatenate([theirs, mine]))
             for mine, theirs in zip(reduced, from_sibling)]
    g_blocks = [jnp.stack([whole[t], whole[len(BIG) + t]]) for t in range(len(BIG))]
    g_blocks[0] = g_blocks[0].transpose(0, 2, 1)

    shapes = [w.shape for w in (attn_w_in, attn_w_out, pool_w_in, pool_w_mix, pool_w_out)]
    d_blocks, m_blocks, v_blocks = adamw(
        "adamw", w_blocks, g_blocks, blocks_2d(m_attn_w_in, m_attn_w_out, m_pool_w_in, m_pool_w_mix, m_pool_w_out),
        blocks_2d(v_attn_w_in, v_attn_w_out, v_pool_w_in, v_pool_w_mix, v_pool_w_out), steps=4)
    g_big, d_big, m_big, v_big = (
        {name: b.reshape(s) for name, b, s in zip(BIG, blocks, shapes)} for blocks in (g_blocks, d_blocks, m_blocks, v_blocks))

    def small_pack(ng, fg, sc, sk):
        row7 = jnp.concatenate([sk.reshape(2 * N_HEADS), jnp.zeros((D_MODEL - 2 * N_HEADS,), F32)])[None, :]
        sc = jnp.concatenate([sc, jnp.zeros((2, D_MODEL - 256), F32)], axis=1)
        return jnp.concatenate([ng, fg[None, :], sc, row7], axis=0)

    packs = [small_pack(norm_g, final_g, pool_scale, attn_sinks), small_pack(g_norm, g_final, g_scale, g_sinks),
             small_pack(m_norm_g, m_final_g, m_pool_scale, m_attn_sinks), small_pack(v_norm_g, v_final_g, v_pool_scale, v_attn_sinks)]
    small_out = [out[0] for out in adamw("adamw_small", *[[p[None]] for p in packs], steps=1)]

    def small_unpack(t):
        t = t[0]
        return {"norm_g": t[0:4], "final_g": t[4], "pool_scale": t[5:7, :256], "attn_sinks": t[7, :2 * N_HEADS].reshape(2, N_HEADS)}

    d_small, m_small, v_small = (small_unpack(t) for t in small_out)
    g_small = {"norm_g": g_norm, "final_g": g_final, "pool_scale": g_scale, "attn_sinks": g_sinks}

    names = ["norm_g", "attn_w_in", "attn_sinks", "attn_w_out", "pool_w_in", "pool_w_mix", "pool_scale", "pool_w_out", "final_g"]

    def ordered(bigs, smalls):
        return [bigs[n] if n in bigs else smalls[n] for n in names]

    return (loss, grad_x[None], *ordered(g_big, g_small), *ordered(d_big, d_small), *ordered(m_big, m_small),
            *ordered(v_big, v_small))
```
